```python
import math
import jax, jax.numpy as jnp
from jax import lax
import numpy as np

D_MODEL = 1024
BATCH = 4
SEQ = 4096
DEPTH = 2

HEAD_DIM = 64
N_GROUPS = 4
GROUP_WIDTH = D_MODEL // N_GROUPS
N_GROUP_HEADS = GROUP_WIDTH // HEAD_DIM
N_IN_SLICES = 13
D_IN = N_IN_SLICES * GROUP_WIDTH
D_FF = 4 * D_MODEL
CONV_WIDTH = 4
RG_LRU_C = 8.0
RET_CHUNK = 128
HGRN_CHUNK = 64
SB_BLOCK = 128
ROPE_BASE = 10000.0
LN_EPS = 1e-5
NORM_EPS = 1e-6
GATE_FLOOR = 1e-30
DEEPNORM_ALPHA = (2 * DEPTH) ** 0.25
DEEPNORM_BETA = (8 * DEPTH) ** -0.25

kernel_name = "hybrid_rglru_retention_stickbreak_hgrn2"


def layer_norm(x, g, b):
    x = x.astype(jnp.float32)
    mu = jnp.mean(x, axis=-1, keepdims=True)
    var = jnp.mean(jnp.square(x - mu), axis=-1, keepdims=True)
    return (x - mu) * lax.rsqrt(var + LN_EPS) * g + b


def split_heads(t):
    b, s, _ = t.shape
    return t.reshape(b, s, N_GROUP_HEADS, HEAD_DIM).transpose(0, 2, 1, 3)


def merge_heads(t):
    b, h, s, d = t.shape
    return t.transpose(0, 2, 1, 3).reshape(b, s, h * d)


def rotary(t, cos, sin):
    half = HEAD_DIM // 2
    t1, t2 = t[..., :half], t[..., half:]
    return jnp.concatenate([t1 * cos - t2 * sin, t1 * sin + t2 * cos], axis=-1)


def rglru_mixer(xa, ga, conv_w, conv_b, wa, ba, wx, bx, lam):
    b_, s_, _ = xa.shape
    xc = lax.conv_general_dilated(
        xa, conv_w.astype(xa.dtype)[:, None, :], window_strides=(1,),
        padding=[(CONV_WIDTH - 1, 0)], dimension_numbers=("NWC", "WIO", "NWC"),
        feature_group_count=GROUP_WIDTH) + conv_b
    xh = xc.reshape(b_, s_, N_GROUP_HEADS, HEAD_DIM)
    r = jax.nn.sigmoid(jnp.einsum("bshi,hij->bshj", xh, wa) + ba).reshape(b_, s_, GROUP_WIDTH)
    i = jax.nn.sigmoid(jnp.einsum("bshi,hij->bshj", xh, wx) + bx).reshape(b_, s_, GROUP_WIDTH)
    log_a = RG_LRU_C * r * jax.nn.log_sigmoid(lam)
    a = jnp.exp(log_a)
    u = jnp.sqrt(jnp.maximum(-jnp.expm1(2.0 * log_a), 0.0)) * (i * xc)

    def combine(left, right):
        a1, b1 = left
        a2, b2 = right
        return a1 * a2, a2 * b1 + b2

    _, h = lax.associative_scan(combine, (a, u), axis=1)
    return jax.nn.gelu(ga) * h


def retention_mixer(q, k, v, g, norm_g, cos, sin):
    q = rotary(split_heads(q), cos, sin)
    k = rotary(split_heads(k), cos, sin) * HEAD_DIM ** -0.5
    v = split_heads(v)
    b_, h_, s_, d_ = q.shape
    c_ = RET_CHUNK
    n_ = s_ // c_
    log_gamma = jnp.log1p(-jnp.exp2(-5.0 - jnp.arange(N_GROUP_HEADS, dtype=jnp.float32)))
    qc = q.reshape(b_, h_, n_, c_, d_)
    kc = k.reshape(b_, h_, n_, c_, d_)
    vc = v.reshape(b_, h_, n_, c_, d_)
    pos = jnp.arange(c_, dtype=jnp.float32)
    diff = pos[:, None] - pos[None, :]
    decay = jnp.where(diff >= 0, jnp.exp(log_gamma[:, None, None] * jnp.maximum(diff, 0.0)), 0.0)
    scores = jnp.einsum("bhnid,bhnjd->bhnij", qc, kc) * decay[:, None]
    o_intra = jnp.einsum("bhnij,bhnjd->bhnid", scores, vc)
    k_decay = jnp.exp(log_gamma[:, None] * (c_ - 1.0 - pos))
    chunk_kv = jnp.einsum("bhnjd,bhnje->bhnde", kc * k_decay[:, None, :, None], vc)
    chunk_decay = jnp.exp(log_gamma * c_)[None, :, None, None]

    def step(state, kv):
        return chunk_decay * state + kv, state

    init = jnp.zeros((b_, h_, d_, d_), chunk_kv.dtype)
    _, prev = lax.scan(step, init, jnp.moveaxis(chunk_kv, 2, 0))
    prev = jnp.moveaxis(prev, 0, 2)
    q_decay = jnp.exp(log_gamma[:, None] * (pos + 1.0))
    o_inter = jnp.einsum("bhnid,bhnde->bhnie", qc * q_decay[:, None, :, None], prev)
    o = (o_intra + o_inter).reshape(b_, h_, s_, d_).astype(jnp.float32)
    mu = jnp.mean(o, axis=-1, keepdims=True)
    var = jnp.mean(jnp.square(o - mu), axis=-1, keepdims=True)
    o = merge_heads((o - mu) * lax.rsqrt(var + NORM_EPS)) * norm_g
    return jax.nn.silu(g) * o


def stick_breaking_mixer(q, k, v):
    q, k, v = split_heads(q), split_heads(k), split_heads(v)
    b_, h_, s_, d_ = q.shape
    nb = s_ // SB_BLOCK
    q_blocks = jnp.moveaxis(q.reshape(b_, h_, nb, SB_BLOCK, d_), 2, 0)
    key_pos = jnp.arange(s_)
    scale = d_ ** -0.5

    def block(args):
        qb, n = args
        q_pos = n * SB_BLOCK + jnp.arange(SB_BLOCK)
        z = (jnp.einsum("bhqd,bhkd->bhqk", qb, k) * scale).astype(jnp.float32)
        mask = key_pos[None, :] < q_pos[:, None]
        log_keep = jnp.where(mask, jax.nn.log_sigmoid(-z), 0.0)
        later = lax.cumsum(log_keep, axis=3, reverse=True) - log_keep
        log_w = jnp.where(mask, jax.nn.log_sigmoid(z) + later, -1e4)
        w = jnp.where(mask, jnp.exp(log_w), 0.0)
        return jnp.einsum("bhqk,bhkd->bhqd", w.astype(v.dtype), v)

    o = lax.map(block, (q_blocks, jnp.arange(nb)))
    o = jnp.moveaxis(o, 0, 2).reshape(b_, h_, s_, d_)
    return merge_heads(o)


def hgrn2_mixer(q, f_pre, v, g, lower_bound, norm_g):
    q, f_pre, v = split_heads(q), split_heads(f_pre), split_heads(v)
    b_, h_, s_, d_ = q.shape
    c_ = HGRN_CHUNK
    n_ = s_ // c_
    lb = lower_bound.astype(jnp.float32).reshape(h_, d_)[None, :, None, :]
    f_pre = f_pre.astype(jnp.float32)
    f_gate = lb + (1.0 - lb) * jax.nn.sigmoid(f_pre)
    log_f = jnp.log(jnp.maximum(f_gate, GATE_FLOOR))
    k = (1.0 - lb) * jax.nn.sigmoid(-f_pre)
    to_chunks = lambda t: jnp.moveaxis(t.reshape(b_, h_, n_, c_, d_), 2, 0)
    bcum = jnp.cumsum(log_f.reshape(b_, h_, n_, c_, d_), axis=3)
    qs, ks, vs, bs = to_chunks(q), to_chunks(k), to_chunks(v), jnp.moveaxis(bcum, 2, 0)
    causal = jnp.tril(jnp.ones((c_, c_), dtype=bool))[:, :, None]

    def step(state, inp):
        qc, kc, vc, bc = inp
        diff = bc[:, :, :, None, :] - bc[:, :, None, :, :]
        w = jnp.where(causal, jnp.exp(jnp.minimum(diff, 0.0)), 0.0)
        attn = jnp.einsum("bhtd,bhtsd,bhsd->bhts", qc, w, kc)
        o = jnp.einsum("bhts,bhse->bhte", attn, vc) + jnp.einsum("bhtd,bhde->bhte", qc * jnp.exp(bc), state)
        b_last = bc[:, :, -1:, :]
        new_state = jnp.exp(b_last[:, :, 0, :])[..., None] * state + jnp.einsum(
            "bhsd,bhse->bhde", kc * jnp.exp(b_last - bc), vc)
        return new_state, o

    init = jnp.zeros((b_, h_, d_, d_), jnp.float32)
    _, o = lax.scan(step, init, (qs.astype(jnp.float32), ks, vs.astype(jnp.float32), bs))
    o = jnp.moveaxis(o, 0, 2).reshape(b_, h_, s_, d_).astype(jnp.float32)
    o = o * lax.rsqrt(jnp.mean(jnp.square(o), axis=-1, keepdims=True) + NORM_EPS)
    return merge_heads(o) * norm_g * jax.nn.silu(g)


def setup_inputs(seed: int = 0) -> dict:
    key = jax.random.key(seed)
    ks = jax.random.split(key, 24)
    f32 = jnp.float32
    nrm = lambda k, shape, s: jax.random.normal(k, shape, f32) * s
    lam_u = jax.random.uniform(ks[10], (DEPTH, GROUP_WIDTH), f32, minval=0.9, maxval=0.999)
    lam_p = lam_u ** (1.0 / RG_LRU_C)
    return {
        "x": jax.random.normal(ks[0], (BATCH, SEQ, D_MODEL), f32),
        "ln_in_g": 1.0 + nrm(ks[1], (D_MODEL,), 0.02),
        "ln_in_b": nrm(ks[2], (D_MODEL,), 0.02),
        "w_in": nrm(ks[3], (DEPTH, D_MODEL, D_IN), D_MODEL ** -0.5),
        "conv_w": nrm(ks[4], (DEPTH, CONV_WIDTH, GROUP_WIDTH), CONV_WIDTH ** -0.5),
        "conv_b": nrm(ks[5], (DEPTH, GROUP_WIDTH), 0.02),
        "rg_wa": nrm(ks[6], (DEPTH, N_GROUP_HEADS, HEAD_DIM, HEAD_DIM), HEAD_DIM ** -0.5),
        "rg_ba": nrm(ks[7], (DEPTH, N_GROUP_HEADS, HEAD_DIM), 0.1),
        "rg_wx": nrm(ks[8], (DEPTH, N_GROUP_HEADS, HEAD_DIM, HEAD_DIM), HEAD_DIM ** -0.5),
        "rg_bx": nrm(ks[9], (DEPTH, N_GROUP_HEADS, HEAD_DIM), 0.1),
        "rg_lambda": jnp.log(lam_p) - jnp.log1p(-lam_p),
        "ret_norm_g": 1.0 + nrm(ks[11], (DEPTH, GROUP_WIDTH), 0.02),
        "hgrn_lb_logits": nrm(ks[12], (DEPTH, GROUP_WIDTH), 0.5),
        "hgrn_norm_g": 1.0 + nrm(ks[13], (DEPTH, GROUP_WIDTH), 0.02),
        "w_out": nrm(ks[14], (DEPTH, D_MODEL, D_MODEL), D_MODEL ** -0.5 * DEEPNORM_BETA),
        "ln1_g": 1.0 + nrm(ks[15], (DEPTH, D_MODEL), 0.02),
        "ln1_b": nrm(ks[16], (DEPTH, D_MODEL), 0.02),
        "w_up": nrm(ks[17], (DEPTH, D_MODEL, D_FF), D_MODEL ** -0.5),
        "w_down": nrm(ks[18], (DEPTH, D_FF, D_MODEL), D_FF ** -0.5 * DEEPNORM_BETA),
        "ln2_g": 1.0 + nrm(ks[19], (DEPTH, D_MODEL), 0.02),
        "ln2_b": nrm(ks[20], (DEPTH, D_MODEL), 0.02),
    }


def reference(x, ln_in_g, ln_in_b, w_in, conv_w, conv_b, rg_wa, rg_ba, rg_wx, rg_bx, rg_lambda,
              ret_norm_g, hgrn_lb_logits, hgrn_norm_g, w_out, ln1_g, ln1_b, w_up, w_down,
              ln2_g, ln2_b):
    s_ = x.shape[1]
    inv_freq = ROPE_BASE ** (-jnp.arange(0, HEAD_DIM, 2, dtype=jnp.float32) / HEAD_DIM)
    ang = jnp.arange(s_, dtype=jnp.float32)[:, None] * inv_freq[None, :]
    cos, sin = jnp.cos(ang), jnp.sin(ang)
    lb_p = jax.nn.softmax(hgrn_lb_logits.astype(jnp.float32), axis=0)
    lower_bounds = jnp.cumsum(lb_p, axis=0) - lb_p[0]

    h = layer_norm(x, ln_in_g, ln_in_b)
    for l in range(DEPTH):
        proj = h @ w_in[l]
        (a_x, a_g, r_q, r_k, r_v, r_g, s_q, s_k, s_v,
         d_q, d_f, d_v, d_g) = jnp.split(proj, N_IN_SLICES, axis=-1)
        y_a = rglru_mixer(a_x, a_g, conv_w[l], conv_b[l], rg_wa[l], rg_ba[l], rg_wx[l], rg_bx[l], rg_lambda[l])
        y_b = retention_mixer(r_q, r_k, r_v, r_g, ret_norm_g[l], cos, sin)
        y_c = stick_breaking_mixer(s_q, s_k, s_v)
        y_d = hgrn2_mixer(d_q, d_f, d_v, d_g, lower_bounds[l], hgrn_norm_g[l])
        mix = jnp.concatenate([y_a, y_b, y_c, y_d], axis=-1) @ w_out[l]
        h = layer_norm(DEEPNORM_ALPHA * h + mix, ln1_g[l], ln1_b[l])
        ff = jnp.square(jax.nn.relu(h @ w_up[l])) @ w_down[l]
        h = layer_norm(DEEPNORM_ALPHA * h + ff, ln2_g[l], ln2_b[l])
    return h.astype(x.dtype)
```

```python
import functools
import math

import jax
import jax.numpy as jnp
from jax import lax
from jax.experimental import pallas as pl
from jax.experimental.pallas import tpu as pltpu

F32 = jnp.float32
BF16 = jnp.bfloat16

HEAD_DIM = 64
N_GROUP_HEADS = 4
GROUP_WIDTH = HEAD_DIM * N_GROUP_HEADS
N_IN_SLICES = 13
CONV_WIDTH = 4
RG_LRU_C = 8.0
RET_CHUNK = 128
SB_BLOCK = 128
HGRN_CHUNK = 32
ROPE_BASE = 10000.0
LN_EPS = 1e-5
NORM_EPS = 1e-6
GATE_FLOOR = 1e-30

VMEM_LIMIT_BYTES = 56 * 1024 * 1024


def _params(*semantics):
    return pltpu.CompilerParams(dimension_semantics=semantics, vmem_limit_bytes=VMEM_LIMIT_BYTES)


def _const_spec(shape):
    zeros = (0,) * len(shape)
    return pl.BlockSpec(shape, lambda *_: zeros)


def _layer_norm_rows(x, g, b):
    mu = jnp.mean(x, axis=-1, keepdims=True)
    xc = x - mu
    var = jnp.mean(xc * xc, axis=-1, keepdims=True)
    return xc * lax.rsqrt(var + LN_EPS) * g + b


def _split2(x):
    hi = x.astype(BF16)
    lo = (x - hi.astype(F32)).astype(BF16)
    return hi, lo


def _head_mean(x, eb):
    hi, lo = _split2(x)
    s = jnp.dot(hi, eb, preferred_element_type=F32) + jnp.dot(lo, eb, preferred_element_type=F32)
    return s * (1.0 / HEAD_DIM)


def _ln_in_kernel(x_ref, g_ref, b_ref, h_ref, hb_ref):
    h = _layer_norm_rows(x_ref[...], g_ref[...], b_ref[...])
    h_ref[...] = h
    hb_ref[...] = h.astype(BF16)


def _ln_in(x2, g, b, tm):
    t, d = x2.shape
    return pl.pallas_call(
        _ln_in_kernel,
        grid=(t // tm,),
        in_specs=[pl.BlockSpec((tm, d), lambda i: (i, 0)), _const_spec((1, d)), _const_spec((1, d))],
        out_specs=[pl.BlockSpec((tm, d), lambda i: (i, 0)), pl.BlockSpec((tm, d), lambda i: (i, 0))],
        out_shape=[jax.ShapeDtypeStruct((t, d), F32), jax.ShapeDtypeStruct((t, d), BF16)],
        compiler_params=_params("parallel"),
        name="ln_in",
    )(x2, g.reshape(1, d), b.reshape(1, d))


def _proj_kernel(h_ref, w_ref, pf_ref, pb_ref):
    p = jnp.dot(h_ref[...], w_ref[...], preferred_element_type=F32)
    pf_ref[...] = p
    pb_ref[...] = p.astype(BF16)


def _proj(hb, w, tm, n_col_blocks):
    t, d = hb.shape
    n = w.shape[1]
    tn = n // n_col_blocks
    return pl.pallas_call(
        _proj_kernel,
        grid=(n_col_blocks, t // tm),
        in_specs=[pl.BlockSpec((tm, d), lambda j, i: (i, 0)), pl.BlockSpec((d, tn), lambda j, i: (0, j))],
        out_specs=[pl.BlockSpec((tm, tn), lambda j, i: (i, j)), pl.BlockSpec((tm, tn), lambda j, i: (i, j))],
        out_shape=[jax.ShapeDtypeStruct((t, n), F32), jax.ShapeDtypeStruct((t, n), BF16)],
        compiler_params=_params("parallel", "parallel"),
        name="in_proj",
    )(hb, w)


def _outproj_kernel(alpha, ya_ref, yb_ref, yc_ref, yd_ref, w_ref, h_ref, g_ref, b_ref, o_ref, ob_ref):
    gw = GROUP_WIDTH
    mix = jnp.dot(ya_ref[...], w_ref[0 * gw:1 * gw, :], preferred_element_type=F32)
    mix += jnp.dot(yb_ref[...], w_ref[1 * gw:2 * gw, :], preferred_element_type=F32)
    mix += jnp.dot(yc_ref[...], w_ref[2 * gw:3 * gw, :], preferred_element_type=F32)
    mix += jnp.dot(yd_ref[...], w_ref[3 * gw:4 * gw, :], preferred_element_type=F32)
    h = _layer_norm_rows(alpha * h_ref[...] + mix, g_ref[...], b_ref[...])
    o_ref[...] = h
    ob_ref[...] = h.astype(BF16)


def _outproj(ys, w, h, g, b, alpha, tm):
    t, d = h.shape
    gw = GROUP_WIDTH
    y_spec = pl.BlockSpec((tm, gw), lambda i: (i, 0))
    row_spec = pl.BlockSpec((tm, d), lambda i: (i, 0))
    return pl.pallas_call(
        functools.partial(_outproj_kernel, alpha),
        grid=(t // tm,),
        in_specs=[y_spec, y_spec, y_spec, y_spec, _const_spec((d, d)), row_spec,
                  _const_spec((1, d)), _const_spec((1, d))],
        out_specs=[row_spec, row_spec],
        out_shape=[jax.ShapeDtypeStruct((t, d), F32), jax.ShapeDtypeStruct((t, d), BF16)],
        compiler_params=_params("parallel"),
        name="out_proj_ln",
    )(*ys, w, h, g.reshape(1, d), b.reshape(1, d))


def _mlp_kernel(alpha, ff_chunk, hb_ref, h_ref, wu_ref, wd_ref, g_ref, b_ref, o_ref, ob_ref):
    hb = hb_ref[...]
    d_ff = wu_ref.shape[1]
    acc = jnp.zeros(h_ref.shape, F32)
    for c in range(d_ff // ff_chunk):
        cols = slice(c * ff_chunk, (c + 1) * ff_chunk)
        u = jnp.dot(hb, wu_ref[:, cols], preferred_element_type=F32)
        u = jnp.square(jnp.maximum(u, 0.0)).astype(BF16)
        acc += jnp.dot(u, wd_ref[cols, :], preferred_element_type=F32)
    h = _layer_norm_rows(alpha * h_ref[...] + acc, g_ref[...], b_ref[...])
    o_ref[...] = h
    ob_ref[...] = h.astype(BF16)


def _mlp(hb, h, wu, wd, g, b, alpha, tm, ff_chunk):
    t, d = h.shape
    d_ff = wu.shape[1]
    row_spec = pl.BlockSpec((tm, d), lambda i: (i, 0))
    resident = lambda shape: pl.BlockSpec(shape, lambda i: (0, 0), pipeline_mode=pl.Buffered(1))
    return pl.pallas_call(
        functools.partial(_mlp_kernel, alpha, ff_chunk),
        grid=(t // tm,),
        in_specs=[row_spec, row_spec, resident((d, d_ff)), resident((d_ff, d)),
                  _const_spec((1, d)), _const_spec((1, d))],
        out_specs=[row_spec, row_spec],
        out_shape=[jax.ShapeDtypeStruct((t, d), F32), jax.ShapeDtypeStruct((t, d), BF16)],
        compiler_params=_params("parallel"),
        name="mlp_ln",
    )(hb, h, wu, wd, g.reshape(1, d), b.reshape(1, d))


def _rglru_kernel(xa_ref, ga_ref, cw_ref, cb_ref, wg_ref, bg_ref, lam_ref, o_ref, ext_ref, hprev_ref):
    ts, gw = xa_ref.shape
    si = pl.program_id(1)

    @pl.when(si == 0)
    def _():
        ext_ref[0:8, :] = jnp.zeros((8, gw), F32)
        hprev_ref[...] = jnp.zeros((8, gw), F32)

    @pl.when(si > 0)
    def _():
        ext_ref[0:8, :] = ext_ref[ts:ts + 8, :]

    x = xa_ref[...]
    ext_ref[8:ts + 8, :] = x
    cw = cw_ref[...]
    xc = x * cw[3:4, :] + cb_ref[...]
    for k in range(1, CONV_WIDTH):
        xc += ext_ref[8 - k:8 - k + ts, :] * cw[3 - k:4 - k, :]

    gates = jnp.dot(xc.astype(BF16), wg_ref[...], preferred_element_type=F32) + bg_ref[...]
    r = jax.nn.sigmoid(gates[:, :gw])
    i = jax.nn.sigmoid(gates[:, gw:])
    lam = lam_ref[...]
    log_sig_lam = -(jnp.maximum(-lam, 0.0) + jnp.log1p(jnp.exp(-jnp.abs(lam))))
    log_a = RG_LRU_C * r * log_sig_lam
    a = jnp.exp(log_a)
    th = jnp.tanh(log_a)
    one_minus_a2 = -2.0 * th / (1.0 - th)
    u = jnp.sqrt(jnp.maximum(one_minus_a2, 0.0)) * (i * xc)

    row = lax.broadcasted_iota(jnp.int32, (ts, gw), 0)
    k = 1
    while k < ts:
        live = row >= k
        a_sh = jnp.where(live, pltpu.roll(a, k, 0), 1.0)
        u_sh = jnp.where(live, pltpu.roll(u, k, 0), 0.0)
        u = a * u_sh + u
        a = a * a_sh
        k *= 2
    h = u + a * hprev_ref[7:8, :]
    hprev_ref[...] = h[ts - 8:ts, :]
    o_ref[...] = (jax.nn.gelu(ga_ref[...], approximate=True) * h).astype(BF16)


def _blockdiag_heads(w):
    h, di, dj = w.shape
    eye = jnp.eye(h, dtype=w.dtype)
    return (eye[:, None, :, None] * w[:, :, None, :]).reshape(h * di, h * dj)


def _rglru(proj, conv_w, conv_b, wa, ba, wx, bx, lam, ts):
    b_, s_, _ = proj.shape
    gw = GROUP_WIDTH
    wg = jnp.concatenate([_blockdiag_heads(wa), _blockdiag_heads(wx)], axis=1).astype(BF16)
    bg = jnp.concatenate([ba.reshape(1, gw), bx.reshape(1, gw)], axis=1)
    col = lambda j: pl.BlockSpec((None, ts, gw), lambda b, s: (b, s, j))
    return pl.pallas_call(
        _rglru_kernel,
        grid=(b_, s_ // ts),
        in_specs=[col(0), col(1), _const_spec((CONV_WIDTH, gw)), _const_spec((1, gw)),
                  _const_spec((gw, 2 * gw)), _const_spec((1, 2 * gw)), _const_spec((1, gw))],
        out_specs=pl.BlockSpec((None, ts, gw), lambda b, s: (b, s, 0)),
        out_shape=jax.ShapeDtypeStruct((b_, s_, gw), BF16),
        scratch_shapes=[pltpu.VMEM((ts + 8, gw), F32), pltpu.VMEM((8, gw), F32)],
        compiler_params=_params("parallel", "arbitrary"),
        name="rglru",
    )(proj, proj, conv_w, conv_b.reshape(1, gw), wg, bg, lam.reshape(1, gw))


def _retention_kernel(q_ref, k_ref, v_ref, g_ref, cos_ref, sin_ref, qdec_ref, kdec_ref, cdec_ref,
                      dmask_ref, ng_ref, eb_ref, o_ref, state_ref):
    c, gw = q_ref.shape

    @pl.when(pl.program_id(1) == 0)
    def _():
        state_ref[...] = jnp.zeros((gw, gw), F32)

    lane = lax.broadcasted_iota(jnp.int32, (c, gw), 1)
    first_half = (lane % HEAD_DIM) < (HEAD_DIM // 2)
    cos = cos_ref[...]
    sin = sin_ref[...]

    def rotary(t):
        partner = jnp.where(first_half, pltpu.roll(t, gw - HEAD_DIM // 2, 1), pltpu.roll(t, HEAD_DIM // 2, 1))
        return t * cos + partner * sin

    q = rotary(q_ref[...])
    k = rotary(k_ref[...]) * (HEAD_DIM ** -0.5)
    vb = v_ref[...]
    eb = eb_ref[...]
    kb = k.astype(BF16)

    o = jnp.dot((q * qdec_ref[...]).astype(BF16), state_ref[...].astype(BF16), preferred_element_type=F32)
    for h in range(N_GROUP_HEADS):
        in_head = (lane // HEAD_DIM) == h
        qh = jnp.where(in_head, q, 0.0).astype(BF16)
        scores = lax.dot_general(qh, kb, (((1,), (1,)), ((), ())), preferred_element_type=F32)
        p = (scores * dmask_ref[h]).astype(BF16)
        o += jnp.where(in_head, jnp.dot(p, vb, preferred_element_type=F32), 0.0)

    kv = lax.dot_general((k * kdec_ref[...]).astype(BF16), vb, (((0,), (0,)), ((), ())),
                         preferred_element_type=F32)
    state_ref[...] = state_ref[...] * cdec_ref[...] + jnp.where(eb > 0, kv, 0.0)

    mu = _head_mean(o, eb)
    oc = o - mu
    var = _head_mean(oc * oc, eb)
    o = oc * lax.rsqrt(var + NORM_EPS) * ng_ref[...]
    o_ref[...] = (jax.nn.silu(g_ref[...]) * o).astype(BF16)


def _retention_tables(s_):
    half = HEAD_DIM // 2
    inv_freq = ROPE_BASE ** (-jnp.arange(0, HEAD_DIM, 2, dtype=F32) / HEAD_DIM)
    ang = jnp.arange(s_, dtype=F32)[:, None] * inv_freq[None, :]
    cos, sin = jnp.cos(ang), jnp.sin(ang)
    cos_t = jnp.tile(jnp.concatenate([cos, cos], axis=-1), (1, N_GROUP_HEADS))
    sin_t = jnp.tile(jnp.concatenate([-sin, sin], axis=-1), (1, N_GROUP_HEADS))
    c_ = RET_CHUNK
    log_gamma = jnp.log1p(-jnp.exp2(-5.0 - jnp.arange(N_GROUP_HEADS, dtype=F32)))
    pos = jnp.arange(c_, dtype=F32)
    diff = pos[:, None] - pos[None, :]
    dmask = jnp.where(diff >= 0, jnp.exp(log_gamma[:, None, None] * jnp.maximum(diff, 0.0)), 0.0)
    lanes = lambda per_head: jnp.repeat(per_head, HEAD_DIM, axis=-1)
    kdec = lanes(jnp.exp(log_gamma[None, :] * (c_ - 1.0 - pos)[:, None]))
    qdec = lanes(jnp.exp(log_gamma[None, :] * (pos + 1.0)[:, None]))
    cdec = lanes(jnp.exp(log_gamma * c_)[None, :])
    del half
    return cos_t, sin_t, qdec, kdec, cdec, dmask


def _retention(proj, proj_b, norm_g, eb):
    b_, s_, _ = proj.shape
    gw = GROUP_WIDTH
    c_ = RET_CHUNK
    cos_t, sin_t, qdec, kdec, cdec, dmask = _retention_tables(s_)
    col = lambda j: pl.BlockSpec((None, c_, gw), lambda b, n: (b, n, j))
    pos_spec = pl.BlockSpec((c_, gw), lambda b, n: (n, 0))
    return pl.pallas_call(
        _retention_kernel,
        grid=(b_, s_ // c_),
        in_specs=[col(2), col(3), col(4), col(5), pos_spec, pos_spec,
                  _const_spec((c_, gw)), _const_spec((c_, gw)), _const_spec((1, gw)),
                  _const_spec((N_GROUP_HEADS, c_, c_)), _const_spec((1, gw)), _const_spec((gw, gw))],
        out_specs=pl.BlockSpec((None, c_, gw), lambda b, n: (b, n, 0)),
        out_shape=jax.ShapeDtypeStruct((b_, s_, gw), BF16),
        scratch_shapes=[pltpu.VMEM((gw, gw), F32)],
        compiler_params=_params("parallel", "arbitrary"),
        name="retention",
    )(proj, proj, proj_b, proj, cos_t, sin_t, qdec, kdec, cdec, dmask, norm_g.reshape(1, gw), eb)


def _pair_blockdiag(x, lane):
    zero = jnp.zeros_like(x)
    return jnp.concatenate([jnp.where(lane < HEAD_DIM, x, zero), jnp.where(lane >= HEAD_DIM, x, zero)], axis=0)


def _stickbreak_kernel(q_ref, k_ref, v_ref, m_ref, o_ref):
    blk = SB_BLOCK
    pw = 2 * HEAD_DIM
    i = pl.program_id(1)
    scale = HEAD_DIM ** -0.5
    lane_kv = lax.broadcasted_iota(jnp.int32, (blk, pw), 1)
    row = lax.broadcasted_iota(jnp.int32, (blk, 2 * blk), 0)
    col = lax.broadcasted_iota(jnp.int32, (blk, 2 * blk), 1)
    first = col < blk
    strictly_before = (col % blk) < row
    m = m_ref[...]

    for p in range(N_GROUP_HEADS // 2):
        lanes = slice(p * pw, (p + 1) * pw)
        qp = q_ref[:, lanes]

        def block(j, carry, diagonal):
            acc, run = carry
            start = pl.multiple_of(j * blk, blk)
            kbd = _pair_blockdiag(k_ref[pl.ds(start, blk), lanes], lane_kv)
            vbd = _pair_blockdiag(v_ref[pl.ds(start, blk), lanes], lane_kv)
            z = lax.dot_general(qp, kbd, (((1,), (1,)), ((), ())), preferred_element_type=F32) * scale
            softplus = jnp.maximum(z, 0.0) + jnp.log1p(jnp.exp(-jnp.abs(z)))
            log_keep = -softplus
            if diagonal:
                log_keep = jnp.where(strictly_before, log_keep, 0.0)
            hi, lo = _split2(log_keep)
            later = jnp.dot(hi, m, preferred_element_type=F32) + jnp.dot(lo, m, preferred_element_type=F32)
            w = jnp.exp((z - softplus) + later + run)
            if diagonal:
                w = jnp.where(strictly_before, w, 0.0)
            acc = acc + jnp.dot(w.astype(BF16), vbd, preferred_element_type=F32)
            tot0 = jnp.sum(log_keep[:, :blk], axis=1, keepdims=True)
            tot1 = jnp.sum(log_keep[:, blk:], axis=1, keepdims=True)
            return acc, run + jnp.where(first, tot0, tot1)

        carry = (jnp.zeros((blk, pw), F32), jnp.zeros((blk, 2 * blk), F32))
        carry = block(i, carry, True)
        acc, _ = lax.fori_loop(0, i, lambda t, c: block(i - 1 - t, c, False), carry)
        o_ref[:, lanes] = acc.astype(BF16)


def _stickbreak(proj_b):
    b_, s_, _ = proj_b.shape
    gw = GROUP_WIDTH
    blk = SB_BLOCK
    idx = jnp.arange(2 * blk)
    same_head = (idx[:, None] // blk) == (idx[None, :] // blk)
    m = (same_head & (idx[:, None] > idx[None, :])).astype(BF16)
    return pl.pallas_call(
        _stickbreak_kernel,
        grid=(b_, s_ // blk),
        in_specs=[pl.BlockSpec((None, blk, gw), lambda b, i: (b, i, 6)),
                  pl.BlockSpec((None, s_, gw), lambda b, i: (b, 0, 7)),
                  pl.BlockSpec((None, s_, gw), lambda b, i: (b, 0, 8)),
                  _const_spec((2 * blk, 2 * blk))],
        out_specs=pl.BlockSpec((None, blk, gw), lambda b, i: (b, i, 0)),
        out_shape=jax.ShapeDtypeStruct((b_, s_, gw), BF16),
        compiler_params=_params("parallel", "arbitrary"),
        name="stickbreak",
    )(proj_b, proj_b, proj_b, m)


def _hgrn_kernel(layer, q_ref, f_ref, v_ref, g_ref, lbl_ref, ng_ref, tri_ref, eb_ref, o_ref,
                 state_ref, b_ref, w_ref, gs_ref, oi_ref):
    ts, gw = q_ref.shape
    c = HGRN_CHUNK

    @pl.when(pl.program_id(1) == 0)
    def _():
        state_ref[...] = jnp.zeros((gw, gw), F32)

    logits = lbl_ref[...]
    e = jnp.exp(logits - jnp.max(logits, axis=0, keepdims=True))
    lb_p = e / jnp.sum(e, axis=0, keepdims=True)
    lb = jnp.sum(lb_p[1:layer + 1, :], axis=0, keepdims=True) if layer > 0 else jnp.zeros((1, gw), F32)

    tri = tri_ref[...]
    eb = eb_ref[...]
    blockdiag = eb > 0
    srow = lax.broadcasted_iota(jnp.int32, (c, gw), 0)

    for ci in range(ts // c):
        base = ci * c
        rows = slice(base, base + c)
        f_pre = f_ref[rows, :]
        q = q_ref[rows, :]
        v = v_ref[rows, :]
        f_gate = lb + (1.0 - lb) * jax.nn.sigmoid(f_pre)
        log_f = jnp.log(jnp.maximum(f_gate, GATE_FLOOR))
        kk = (1.0 - lb) * jax.nn.sigmoid(-f_pre)

        p1 = log_f.astype(BF16)
        r1 = log_f - p1.astype(F32)
        p2 = r1.astype(BF16)
        p3 = (r1 - p2.astype(F32)).astype(BF16)
        b = (jnp.dot(tri, p1, preferred_element_type=F32) + jnp.dot(tri, p2, preferred_element_type=F32)
             + jnp.dot(tri, p3, preferred_element_type=F32))
        b_ref[...] = b

        state_t = state_ref[...]
        o = lax.dot_general((q * jnp.exp(b)).astype(BF16), state_t.astype(BF16), (((1,), (1,)), ((), ())),
                            preferred_element_type=F32)

        def fill(t, _):
            bt = b_ref[pl.ds(t, 1), :]
            qt = q_ref[pl.ds(base + t, 1), :]
            w = jnp.exp(jnp.minimum(bt - b, 0.0)) * (kk * qt)
            w = jnp.where(srow <= t, w, 0.0)
            w_ref[pl.ds(pl.multiple_of(t * c, c), c), :] = w.astype(BF16)
            return 0

        lax.fori_loop(0, c, fill, 0)
        gs_ref[...] = jnp.dot(w_ref[...], eb, preferred_element_type=F32)

        def reduce_rows(t, _):
            gt = gs_ref[pl.ds(pl.multiple_of(t * c, c), c), :] * v
            oi_ref[pl.ds(t, 1), :] = jnp.sum(gt, axis=0, keepdims=True)
            return 0

        lax.fori_loop(0, c, reduce_rows, 0)
        o = o + oi_ref[...]

        b_last = b[c - 1:c, :]
        kd = (kk * jnp.exp(b_last - b)).astype(BF16)
        kv_t = lax.dot_general(v.astype(BF16), kd, (((0,), (0,)), ((), ())), preferred_element_type=F32)
        state_ref[...] = state_t * jnp.exp(b_last) + jnp.where(blockdiag, kv_t, 0.0)

        ms = _head_mean(o * o, eb)
        o = o * lax.rsqrt(ms + NORM_EPS)
        o_ref[rows, :] = (o * ng_ref[...] * jax.nn.silu(g_ref[rows, :])).astype(BF16)


def _hgrn(proj, lb_logits, norm_g, eb, layer, ts):
    b_, s_, _ = proj.shape
    gw = GROUP_WIDTH
    c = HGRN_CHUNK
    depth = lb_logits.shape[0]
    tri = jnp.tril(jnp.ones((c, c), F32)).astype(BF16)
    col = lambda j: pl.BlockSpec((None, ts, gw), lambda b, s: (b, s, j))
    return pl.pallas_call(
        functools.partial(_hgrn_kernel, layer),
        grid=(b_, s_ // ts),
        in_specs=[col(9), col(10), col(11), col(12), _const_spec((depth, gw)), _const_spec((1, gw)),
                  _const_spec((c, c)), _const_spec((gw, gw))],
        out_specs=pl.BlockSpec((None, ts, gw), lambda b, s: (b, s, 0)),
        out_shape=jax.ShapeDtypeStruct((b_, s_, gw), BF16),
        scratch_shapes=[pltpu.VMEM((gw, gw), F32), pltpu.VMEM((c, gw), F32), pltpu.VMEM((c * c, gw), BF16),
                        pltpu.VMEM((c * c, gw), F32), pltpu.VMEM((c, gw), F32)],
        compiler_params=_params("parallel", "arbitrary"),
        name="hgrn2",
    )(proj, proj, proj, proj, lb_logits, norm_g.reshape(1, gw), tri, eb)


def kernel(x, ln_in_g, ln_in_b, w_in, conv_w, conv_b, rg_wa, rg_ba, rg_wx, rg_bx, rg_lambda, ret_norm_g,
           hgrn_lb_logits, hgrn_norm_g, w_out, ln1_g, ln1_b, w_up, w_down, ln2_g, ln2_b):
    b_, s_, d = x.shape
    depth = w_in.shape[0]
    t = b_ * s_
    alpha = (2 * depth) ** 0.25
    tm = min(512, t)
    scan_rows = min(256, s_)

    head = jnp.arange(GROUP_WIDTH) // HEAD_DIM
    eb = (head[:, None] == head[None, :]).astype(BF16)

    h, hb = _ln_in(x.reshape(t, d), ln_in_g, ln_in_b, tm)
    for l in range(depth):
        pf, pb = _proj(hb, w_in[l].astype(BF16), tm, 2)
        pf = pf.reshape(b_, s_, -1)
        pb = pb.reshape(b_, s_, -1)
        y_a = _rglru(pf, conv_w[l], conv_b[l], rg_wa[l], rg_ba[l], rg_wx[l], rg_bx[l], rg_lambda[l], scan_rows)
        y_b = _retention(pf, pb, ret_norm_g[l], eb)
        y_c = _stickbreak(pb)
        y_d = _hgrn(pf, hgrn_lb_logits, hgrn_norm_g[l], eb, l, scan_rows)
        ys = [y.reshape(t, GROUP_WIDTH) for y in (y_a, y_b, y_c, y_d)]
        h, hb = _outproj(ys, w_out[l].astype(BF16), h, ln1_g[l], ln1_b[l], alpha, tm)
        h, hb = _mlp(hb, h, w_up[l].astype(BF16), w_down[l].astype(BF16), ln2_g[l], ln2_b[l], alpha, tm, 1024)
    return h.reshape(b_, s_, d).astype(x.dtype)
```

```python
import functools
import math

import jax
import jax.numpy as jnp
from jax import lax
from jax.experimental import pallas as pl
from jax.experimental.pallas import tpu as pltpu

F32 = jnp.float32
BF16 = jnp.bfloat16

HEAD_DIM = 64
N_GROUP_HEADS = 4
GROUP_WIDTH = HEAD_DIM * N_GROUP_HEADS
N_IN_SLICES = 13
CONV_WIDTH = 4
RG_LRU_C = 8.0
RET_CHUNK = 128
SB_BLOCK = 128
HGRN_CHUNK = 16
ROPE_BASE = 10000.0
LN_EPS = 1e-5
NORM_EPS = 1e-6
GATE_FLOOR = 1e-30

VMEM_LIMIT_BYTES = 56 * 1024 * 1024


def _params(*semantics):
    return pltpu.CompilerParams(dimension_semantics=semantics, vmem_limit_bytes=VMEM_LIMIT_BYTES)


def _const_spec(shape):
    zeros = (0,) * len(shape)
    return pl.BlockSpec(shape, lambda *_: zeros)


def _layer_norm_rows(x, g, b):
    mu = jnp.mean(x, axis=-1, keepdims=True)
    xc = x - mu
    var = jnp.mean(xc * xc, axis=-1, keepdims=True)
    return xc * lax.rsqrt(var + LN_EPS) * g + b


def _split2(x):
    hi = x.astype(BF16)
    lo = (x - hi.astype(F32)).astype(BF16)
    return hi, lo


def _head_mean(x, eb):
    hi, lo = _split2(x)
    s = jnp.dot(hi, eb, preferred_element_type=F32) + jnp.dot(lo, eb, preferred_element_type=F32)
    return s * (1.0 / HEAD_DIM)


def _ln_in_kernel(x_ref, g_ref, b_ref, h_ref, hb_ref):
    h = _layer_norm_rows(x_ref[...], g_ref[...], b_ref[...])
    h_ref[...] = h
    hb_ref[...] = h.astype(BF16)


def _ln_in(x2, g, b, tm):
    t, d = x2.shape
    return pl.pallas_call(
        _ln_in_kernel,
        grid=(t // tm,),
        in_specs=[pl.BlockSpec((tm, d), lambda i: (i, 0)), _const_spec((1, d)), _const_spec((1, d))],
        out_specs=[pl.BlockSpec((tm, d), lambda i: (i, 0)), pl.BlockSpec((tm, d), lambda i: (i, 0))],
        out_shape=[jax.ShapeDtypeStruct((t, d), F32), jax.ShapeDtypeStruct((t, d), BF16)],
        compiler_params=_params("parallel"),
        name="ln_in",
    )(x2, g.reshape(1, d), b.reshape(1, d))


def _proj_kernel(h_ref, w_ref, pf_ref, pb_ref):
    p = jnp.dot(h_ref[...], w_ref[...], preferred_element_type=F32)
    pf_ref[...] = p
    pb_ref[...] = p.astype(BF16)


def _proj(hb, w, tm, n_col_blocks):
    t, d = hb.shape
    n = w.shape[1]
    tn = n // n_col_blocks
    return pl.pallas_call(
        _proj_kernel,
        grid=(n_col_blocks, t // tm),
        in_specs=[pl.BlockSpec((tm, d), lambda j, i: (i, 0)), pl.BlockSpec((d, tn), lambda j, i: (0, j))],
        out_specs=[pl.BlockSpec((tm, tn), lambda j, i: (i, j)), pl.BlockSpec((tm, tn), lambda j, i: (i, j))],
        out_shape=[jax.ShapeDtypeStruct((t, n), F32), jax.ShapeDtypeStruct((t, n), BF16)],
        compiler_params=_params("parallel", "parallel"),
        name="in_proj",
    )(hb, w)


def _outproj_kernel(alpha, ya_ref, yb_ref, yc_ref, yd_ref, w_ref, h_ref, g_ref, b_ref, o_ref, ob_ref):
    gw = GROUP_WIDTH
    mix = jnp.dot(ya_ref[...], w_ref[0 * gw:1 * gw, :], preferred_element_type=F32)
    mix += jnp.dot(yb_ref[...], w_ref[1 * gw:2 * gw, :], preferred_element_type=F32)
    mix += jnp.dot(yc_ref[...], w_ref[2 * gw:3 * gw, :], preferred_element_type=F32)
    mix += jnp.dot(yd_ref[...], w_ref[3 * gw:4 * gw, :], preferred_element_type=F32)
    h = _layer_norm_rows(alpha * h_ref[...] + mix, g_ref[...], b_ref[...])
    o_ref[...] = h
    ob_ref[...] = h.astype(BF16)


def _outproj(ys, w, h, g, b, alpha, tm):
    t, d = h.shape
    gw = GROUP_WIDTH
    y_spec = pl.BlockSpec((tm, gw), lambda i: (i, 0))
    row_spec = pl.BlockSpec((tm, d), lambda i: (i, 0))
    return pl.pallas_call(
        functools.partial(_outproj_kernel, alpha),
        grid=(t // tm,),
        in_specs=[y_spec, y_spec, y_spec, y_spec, _const_spec((d, d)), row_spec,
                  _const_spec((1, d)), _const_spec((1, d))],
        out_specs=[row_spec, row_spec],
        out_shape=[jax.ShapeDtypeStruct((t, d), F32), jax.ShapeDtypeStruct((t, d), BF16)],
        compiler_params=_params("parallel"),
        name="out_proj_ln",
    )(*ys, w, h, g.reshape(1, d), b.reshape(1, d))


def _mlp_kernel(alpha, ff_chunk, hb_ref, h_ref, wu_ref, wd_ref, g_ref, b_ref, o_ref, ob_ref):
    hb = hb_ref[...]
    d_ff = wu_ref.shape[1]
    acc = jnp.zeros(h_ref.shape, F32)
    for c in range(d_ff // ff_chunk):
        cols = slice(c * ff_chunk, (c + 1) * ff_chunk)
        u = jnp.dot(hb, wu_ref[:, cols], preferred_element_type=F32)
        u = jnp.square(jnp.maximum(u, 0.0)).astype(BF16)
        acc += jnp.dot(u, wd_ref[cols, :], preferred_element_type=F32)
    h = _layer_norm_rows(alpha * h_ref[...] + acc, g_ref[...], b_ref[...])
    o_ref[...] = h
    ob_ref[...] = h.astype(BF16)


def _mlp(hb, h, wu, wd, g, b, alpha, tm, ff_chunk):
    t, d = h.shape
    d_ff = wu.shape[1]
    row_spec = pl.BlockSpec((tm, d), lambda i: (i, 0))
    resident = lambda shape: pl.BlockSpec(shape, lambda i: (0, 0), pipeline_mode=pl.Buffered(1))
    return pl.pallas_call(
        functools.partial(_mlp_kernel, alpha, ff_chunk),
        grid=(t // tm,),
        in_specs=[row_spec, row_spec, resident((d, d_ff)), resident((d_ff, d)),
                  _const_spec((1, d)), _const_spec((1, d))],
        out_specs=[row_spec, row_spec],
        out_shape=[jax.ShapeDtypeStruct((t, d), F32), jax.ShapeDtypeStruct((t, d), BF16)],
        compiler_params=_params("parallel"),
        name="mlp_ln",
    )(hb, h, wu, wd, g.reshape(1, d), b.reshape(1, d))


def _rglru_kernel(xa_ref, ga_ref, cw_ref, cb_ref, wg_ref, bg_ref, lam_ref, o_ref, ext_ref, hprev_ref):
    ts, gw = xa_ref.shape
    si = pl.program_id(1)

    @pl.when(si == 0)
    def _():
        ext_ref[0:8, :] = jnp.zeros((8, gw), F32)
        hprev_ref[...] = jnp.zeros((8, gw), F32)

    @pl.when(si > 0)
    def _():
        ext_ref[0:8, :] = ext_ref[ts:ts + 8, :]

    x = xa_ref[...]
    ext_ref[8:ts + 8, :] = x
    cw = cw_ref[...]
    xc = x * cw[3:4, :] + cb_ref[...]
    for k in range(1, CONV_WIDTH):
        xc += ext_ref[8 - k:8 - k + ts, :] * cw[3 - k:4 - k, :]

    gates = jnp.dot(xc.astype(BF16), wg_ref[...], preferred_element_type=F32) + bg_ref[...]
    r = jax.nn.sigmoid(gates[:, :gw])
    i = jax.nn.sigmoid(gates[:, gw:])
    lam = lam_ref[...]
    log_sig_lam = -(jnp.maximum(-lam, 0.0) + jnp.log1p(jnp.exp(-jnp.abs(lam))))
    log_a = RG_LRU_C * r * log_sig_lam
    a = jnp.exp(log_a)
    th = jnp.tanh(log_a)
    one_minus_a2 = -2.0 * th / (1.0 - th)
    u = jnp.sqrt(jnp.maximum(one_minus_a2, 0.0)) * (i * xc)

    row = lax.broadcasted_iota(jnp.int32, (ts, gw), 0)
    k = 1
    while k < ts:
        live = row >= k
        a_sh = jnp.where(live, pltpu.roll(a, k, 0), 1.0)
        u_sh = jnp.where(live, pltpu.roll(u, k, 0), 0.0)
        u = a * u_sh + u
        a = a * a_sh
        k *= 2
    h = u + a * hprev_ref[7:8, :]
    hprev_ref[...] = h[ts - 8:ts, :]
    o_ref[...] = (jax.nn.gelu(ga_ref[...], approximate=True) * h).astype(BF16)


def _blockdiag_heads(w):
    h, di, dj = w.shape
    eye = jnp.eye(h, dtype=w.dtype)
    return (eye[:, None, :, None] * w[:, :, None, :]).reshape(h * di, h * dj)


def _rglru(proj, conv_w, conv_b, wa, ba, wx, bx, lam, ts):
    b_, s_, _ = proj.shape
    gw = GROUP_WIDTH
    wg = jnp.concatenate([_blockdiag_heads(wa), _blockdiag_heads(wx)], axis=1).astype(BF16)
    bg = jnp.concatenate([ba.reshape(1, gw), bx.reshape(1, gw)], axis=1)
    col = lambda j: pl.BlockSpec((None, ts, gw), lambda b, s: (b, s, j))
    return pl.pallas_call(
        _rglru_kernel,
        grid=(b_, s_ // ts),
        in_specs=[col(0), col(1), _const_spec((CONV_WIDTH, gw)), _const_spec((1, gw)),
                  _const_spec((gw, 2 * gw)), _const_spec((1, 2 * gw)), _const_spec((1, gw))],
        out_specs=pl.BlockSpec((None, ts, gw), lambda b, s: (b, s, 0)),
        out_shape=jax.ShapeDtypeStruct((b_, s_, gw), BF16),
        scratch_shapes=[pltpu.VMEM((ts + 8, gw), F32), pltpu.VMEM((8, gw), F32)],
        compiler_params=_params("parallel", "arbitrary"),
        name="rglru",
    )(proj, proj, conv_w, conv_b.reshape(1, gw), wg, bg, lam.reshape(1, gw))


def _retention_kernel(q_ref, k_ref, v_ref, g_ref, cos_ref, sin_ref, qdec_ref, kdec_ref, cdec_ref,
                      dmask_ref, ng_ref, eb_ref, o_ref, state_ref):
    c, gw = q_ref.shape

    @pl.when(pl.program_id(1) == 0)
    def _():
        state_ref[...] = jnp.zeros((gw, gw), F32)

    lane = lax.broadcasted_iota(jnp.int32, (c, gw), 1)
    first_half = (lane % HEAD_DIM) < (HEAD_DIM // 2)
    cos = cos_ref[...]
    sin = sin_ref[...]

    def rotary(t):
        partner = jnp.where(first_half, pltpu.roll(t, gw - HEAD_DIM // 2, 1), pltpu.roll(t, HEAD_DIM // 2, 1))
        return t * cos + partner * sin

    q = rotary(q_ref[...])
    k = rotary(k_ref[...]) * (HEAD_DIM ** -0.5)
    vb = v_ref[...]
    eb = eb_ref[...]
    kb = k.astype(BF16)

    o = jnp.dot((q * qdec_ref[...]).astype(BF16), state_ref[...].astype(BF16), preferred_element_type=F32)
    for h in range(N_GROUP_HEADS):
        in_head = (lane // HEAD_DIM) == h
        qh = jnp.where(in_head, q, 0.0).astype(BF16)
        scores = lax.dot_general(qh, kb, (((1,), (1,)), ((), ())), preferred_element_type=F32)
        p = (scores * dmask_ref[h]).astype(BF16)
        o += jnp.where(in_head, jnp.dot(p, vb, preferred_element_type=F32), 0.0)

    kv = lax.dot_general((k * kdec_ref[...]).astype(BF16), vb, (((0,), (0,)), ((), ())),
                         preferred_element_type=F32)
    state_ref[...] = state_ref[...] * cdec_ref[...] + jnp.where(eb > 0, kv, 0.0)

    mu = _head_mean(o, eb)
    oc = o - mu
    var = _head_mean(oc * oc, eb)
    o = oc * lax.rsqrt(var + NORM_EPS) * ng_ref[...]
    o_ref[...] = (jax.nn.silu(g_ref[...]) * o).astype(BF16)


def _retention_tables(s_):
    half = HEAD_DIM // 2
    inv_freq = ROPE_BASE ** (-jnp.arange(0, HEAD_DIM, 2, dtype=F32) / HEAD_DIM)
    ang = jnp.arange(s_, dtype=F32)[:, None] * inv_freq[None, :]
    cos, sin = jnp.cos(ang), jnp.sin(ang)
    cos_t = jnp.tile(jnp.concatenate([cos, cos], axis=-1), (1, N_GROUP_HEADS))
    sin_t = jnp.tile(jnp.concatenate([-sin, sin], axis=-1), (1, N_GROUP_HEADS))
    c_ = RET_CHUNK
    log_gamma = jnp.log1p(-jnp.exp2(-5.0 - jnp.arange(N_GROUP_HEADS, dtype=F32)))
    pos = jnp.arange(c_, dtype=F32)
    diff = pos[:, None] - pos[None, :]
    dmask = jnp.where(diff >= 0, jnp.exp(log_gamma[:, None, None] * jnp.maximum(diff, 0.0)), 0.0)
    lanes = lambda per_head: jnp.repeat(per_head, HEAD_DIM, axis=-1)
    kdec = lanes(jnp.exp(log_gamma[None, :] * (c_ - 1.0 - pos)[:, None]))
    qdec = lanes(jnp.exp(log_gamma[None, :] * (pos + 1.0)[:, None]))
    cdec = lanes(jnp.exp(log_gamma * c_)[None, :])
    del half
    return cos_t, sin_t, qdec, kdec, cdec, dmask


def _retention(proj, proj_b, norm_g, eb):
    b_, s_, _ = proj.shape
    gw = GROUP_WIDTH
    c_ = RET_CHUNK
    cos_t, sin_t, qdec, kdec, cdec, dmask = _retention_tables(s_)
    col = lambda j: pl.BlockSpec((None, c_, gw), lambda b, n: (b, n, j))
    pos_spec = pl.BlockSpec((c_, gw), lambda b, n: (n, 0))
    return pl.pallas_call(
        _retention_kernel,
        grid=(b_, s_ // c_),
        in_specs=[col(2), col(3), col(4), col(5), pos_spec, pos_spec,
                  _const_spec((c_, gw)), _const_spec((c_, gw)), _const_spec((1, gw)),
                  _const_spec((N_GROUP_HEADS, c_, c_)), _const_spec((1, gw)), _const_spec((gw, gw))],
        out_specs=pl.BlockSpec((None, c_, gw), lambda b, n: (b, n, 0)),
        out_shape=jax.ShapeDtypeStruct((b_, s_, gw), BF16),
        scratch_shapes=[pltpu.VMEM((gw, gw), F32)],
        compiler_params=_params("parallel", "arbitrary"),
        name="retention",
    )(proj, proj, proj_b, proj, cos_t, sin_t, qdec, kdec, cdec, dmask, norm_g.reshape(1, gw), eb)


def _pair_blockdiag(x, lane):
    zero = jnp.zeros_like(x)
    return jnp.concatenate([jnp.where(lane < HEAD_DIM, x, zero), jnp.where(lane >= HEAD_DIM, x, zero)], axis=0)


def _stickbreak_kernel(q_ref, k_ref, v_ref, m_ref, o_ref):
    tq = q_ref.shape[0]
    blk = SB_BLOCK
    sub = tq // blk
    pw = 2 * HEAD_DIM
    n_pairs = N_GROUP_HEADS // 2
    i = pl.program_id(1)
    scale = HEAD_DIM ** -0.5
    lane_kv = lax.broadcasted_iota(jnp.int32, (blk, pw), 1)
    q_pos = i * tq + lax.broadcasted_iota(jnp.int32, (tq, 2 * blk), 0)
    key_off = lax.broadcasted_iota(jnp.int32, (tq, 2 * blk), 1) % blk
    m = m_ref[...]
    qs = [q_ref[:, p * pw:(p + 1) * pw] * scale for p in range(n_pairs)]

    def key_block(j, p, masked):
        lanes = slice(p * pw, (p + 1) * pw)
        start = pl.multiple_of(j * blk, blk)
        kbd = _pair_blockdiag(k_ref[pl.ds(start, blk), lanes], lane_kv)
        vbd = _pair_blockdiag(v_ref[pl.ds(start, blk), lanes], lane_kv)
        z = lax.dot_general(qs[p], kbd, (((1,), (1,)), ((), ())), preferred_element_type=F32)
        softplus = jnp.maximum(z, 0.0) + jnp.log(1.0 + jnp.exp(-jnp.abs(z)))
        log_keep = -softplus
        before = None
        if masked:
            before = (j * blk + key_off) < q_pos
            log_keep = jnp.where(before, log_keep, 0.0)
        hi, lo = _split2(log_keep)
        later = jnp.dot(hi, m, preferred_element_type=F32) + jnp.dot(lo, m, preferred_element_type=F32)
        tot = (jnp.sum(log_keep[:, :blk], axis=1, keepdims=True),
               jnp.sum(log_keep[:, blk:], axis=1, keepdims=True))
        return (z - softplus) + later, tot, vbd, before

    def step(j_hi, carry, masked):
        out = []
        for p in range(n_pairs):
            acc, run0, run1 = carry[p]
            parts = [key_block(j_hi - u, p, masked) for u in range(sub)]
            for log_w, tot, vbd, before in parts:
                w = jnp.concatenate([jnp.exp(log_w[:, :blk] + run0), jnp.exp(log_w[:, blk:] + run1)], axis=1)
                if masked:
                    w = jnp.where(before, w, 0.0)
                acc = acc + jnp.dot(w.astype(BF16), vbd, preferred_element_type=F32)
                run0 = run0 + tot[0]
                run1 = run1 + tot[1]
            out.append((acc, run0, run1))
        return tuple(out)

    zero_col = jnp.zeros((tq, 1), F32)
    carry = tuple((jnp.zeros((tq, pw), F32), zero_col, zero_col) for _ in range(n_pairs))
    carry = step(sub * i + sub - 1, carry, True)
    carry = lax.fori_loop(0, i, lambda t, c: step(sub * (i - 1 - t) + sub - 1, c, False), carry)
    for p in range(n_pairs):
        o_ref[:, p * pw:(p + 1) * pw] = carry[p][0].astype(BF16)


def _stickbreak(proj_b, tq):
    b_, s_, _ = proj_b.shape
    gw = GROUP_WIDTH
    blk = SB_BLOCK
    idx = jnp.arange(2 * blk)
    same_head = (idx[:, None] // blk) == (idx[None, :] // blk)
    m = (same_head & (idx[:, None] > idx[None, :])).astype(BF16)
    return pl.pallas_call(
        _stickbreak_kernel,
        grid=(b_, s_ // tq),
        in_specs=[pl.BlockSpec((None, tq, gw), lambda b, i: (b, i, 6)),
                  pl.BlockSpec((None, s_, gw), lambda b, i: (b, 0, 7)),
                  pl.BlockSpec((None, s_, gw), lambda b, i: (b, 0, 8)),
                  _const_spec((2 * blk, 2 * blk))],
        out_specs=pl.BlockSpec((None, tq, gw), lambda b, i: (b, i, 0)),
        out_shape=jax.ShapeDtypeStruct((b_, s_, gw), BF16),
        compiler_params=_params("parallel", "arbitrary"),
        name="stickbreak",
    )(proj_b, proj_b, proj_b, m)


def _split3(x):
    p1 = x.astype(BF16)
    r1 = x - p1.astype(F32)
    p2 = r1.astype(BF16)
    p3 = (r1 - p2.astype(F32)).astype(BF16)
    return p1, p2, p3


def _dot3(a, parts):
    return sum(jnp.dot(a, p, preferred_element_type=F32) for p in parts)


def _hgrn_kernel(layer, q_ref, f_ref, v_ref, g_ref, lbl_ref, ng_ref, tri_ref, sel_ref, bias_ref, eb_ref, o_ref,
                 state_ref, b2_ref, kk_ref, w_ref, gs_ref, oi_ref):
    ts, gw = q_ref.shape
    c = HGRN_CHUNK
    n_chunks = ts // c
    pw = 2 * HEAD_DIM
    n_pairs = N_GROUP_HEADS // 2

    @pl.when(pl.program_id(1) == 0)
    def _():
        state_ref[...] = jnp.zeros(state_ref.shape, F32)

    logits = lbl_ref[...]
    e = jnp.exp(logits - jnp.max(logits, axis=0, keepdims=True))
    lb_p = e / jnp.sum(e, axis=0, keepdims=True)
    lb = jnp.sum(lb_p[1:layer + 1, :], axis=0, keepdims=True) if layer > 0 else jnp.zeros((1, gw), F32)

    f_pre = f_ref[...]
    q = q_ref[...]
    f_gate = lb + (1.0 - lb) * jax.nn.sigmoid(f_pre)
    log_f = jnp.log(jnp.maximum(f_gate, GATE_FLOOR))
    kk = (1.0 - lb) * jax.nn.sigmoid(-f_pre)
    parts = _split3(log_f)
    b = _dot3(tri_ref[...], parts)
    b_tot = _dot3(sel_ref[...], parts)
    qe = (q * jnp.exp(b)).astype(BF16)
    kd = (kk * jnp.exp(b_tot - b)).astype(BF16)
    decay = jnp.exp(b_tot)
    b2_ref[...] = b * math.log2(math.e)
    kk_ref[...] = kk

    def fill(n, _):
        r0 = pl.multiple_of(n * c, c)
        bb = b2_ref[pl.ds(r0, c), :]
        qb = q_ref[pl.ds(r0, c), :]
        for s in range(c):
            bs = b2_ref[pl.ds(r0 + s, 1), :]
            ks = kk_ref[pl.ds(r0 + s, 1), :]
            w = jnp.exp2(jnp.minimum(bb - bs, bias_ref[s])) * (qb * ks)
            w_ref[pl.ds(pl.multiple_of(n * (c * c) + s * c, c), c), :] = w.astype(BF16)
        return 0

    lax.fori_loop(0, n_chunks, fill, 0)
    eb = eb_ref[...]
    gs_ref[...] = jnp.dot(w_ref[...], eb, preferred_element_type=F32)

    def reduce_keys(n, _):
        r0 = pl.multiple_of(n * c, c)
        acc = jnp.zeros((c, gw), F32)
        for s in range(c):
            acc += gs_ref[pl.ds(pl.multiple_of(n * (c * c) + s * c, c), c), :] * v_ref[pl.ds(r0 + s, 1), :]
        oi_ref[pl.ds(r0, c), :] = acc
        return 0

    lax.fori_loop(0, n_chunks, reduce_keys, 0)

    vb = v_ref[...].astype(BF16)
    same_head = eb[0:pw, 0:pw] > 0
    rows = [slice(n * c, (n + 1) * c) for n in range(n_chunks)]
    lanes = [slice(p * pw, (p + 1) * pw) for p in range(n_pairs)]
    kv_t = [[lax.dot_general(vb[r, l], kd[r, l], (((0,), (0,)), ((), ())), preferred_element_type=F32)
             for r in rows] for l in lanes]
    states = []
    for p in range(n_pairs):
        state = state_ref[p]
        entering = []
        for n in range(n_chunks):
            entering.append(state.astype(BF16))
            state = state * decay[n * c:n * c + 1, lanes[p]] + jnp.where(same_head, kv_t[p][n], 0.0)
        state_ref[p] = state
        states.append(entering)
    o_state = [jnp.concatenate([lax.dot_general(qe[rows[n], lanes[p]], states[p][n], (((1,), (1,)), ((), ())),
                                                preferred_element_type=F32) for n in range(n_chunks)], axis=0)
               for p in range(n_pairs)]

    o = jnp.concatenate(o_state, axis=1) + oi_ref[...]
    ms = _head_mean(o * o, eb)
    o = o * lax.rsqrt(ms + NORM_EPS)
    o_ref[...] = (o * ng_ref[...] * jax.nn.silu(g_ref[...])).astype(BF16)


def _hgrn(proj, lb_logits, norm_g, eb, layer, ts):
    b_, s_, _ = proj.shape
    gw = GROUP_WIDTH
    c = HGRN_CHUNK
    depth = lb_logits.shape[0]
    row = jnp.arange(ts)
    same_chunk = (row[:, None] // c) == (row[None, :] // c)
    tri = (same_chunk & (row[:, None] >= row[None, :])).astype(BF16)
    sel = same_chunk.astype(BF16)
    pos = jnp.arange(c)
    bias = jnp.where(pos[None, :, None] >= pos[:, None, None], 0.0, -1e30)
    bias = jnp.broadcast_to(bias, (c, c, gw)).astype(F32)
    col = lambda j: pl.BlockSpec((None, ts, gw), lambda b, s: (b, s, j))
    return pl.pallas_call(
        functools.partial(_hgrn_kernel, layer),
        grid=(b_, s_ // ts),
        in_specs=[col(9), col(10), col(11), col(12), _const_spec((depth, gw)), _const_spec((1, gw)),
                  _const_spec((ts, ts)), _const_spec((ts, ts)), _const_spec((c, c, gw)), _const_spec((gw, gw))],
        out_specs=pl.BlockSpec((None, ts, gw), lambda b, s: (b, s, 0)),
        out_shape=jax.ShapeDtypeStruct((b_, s_, gw), BF16),
        scratch_shapes=[pltpu.VMEM((N_GROUP_HEADS // 2, 2 * HEAD_DIM, 2 * HEAD_DIM), F32),
                        pltpu.VMEM((ts, gw), F32), pltpu.VMEM((ts, gw), F32),
                        pltpu.VMEM((ts * c, gw), BF16), pltpu.VMEM((ts * c, gw), F32), pltpu.VMEM((ts, gw), F32)],
        compiler_params=_params("parallel", "arbitrary"),
        name="hgrn2",
    )(proj, proj, proj, proj, lb_logits, norm_g.reshape(1, gw), tri, sel, bias, eb)


def kernel(x, ln_in_g, ln_in_b, w_in, conv_w, conv_b, rg_wa, rg_ba, rg_wx, rg_bx, rg_lambda, ret_norm_g,
           hgrn_lb_logits, hgrn_norm_g, w_out, ln1_g, ln1_b, w_up, w_down, ln2_g, ln2_b):
    b_, s_, d = x.shape
    depth = w_in.shape[0]
    t = b_ * s_
    alpha = (2 * depth) ** 0.25
    tm = min(512, t)
    scan_rows = min(256, s_)

    head = jnp.arange(GROUP_WIDTH) // HEAD_DIM
    eb = (head[:, None] == head[None, :]).astype(BF16)

    h, hb = _ln_in(x.reshape(t, d), ln_in_g, ln_in_b, tm)
    for l in range(depth):
        pf, pb = _proj(hb, w_in[l].astype(BF16), tm, 2)
        pf = pf.reshape(b_, s_, -1)
        pb = pb.reshape(b_, s_, -1)
        y_a = _rglru(pf, conv_w[l], conv_b[l], rg_wa[l], rg_ba[l], rg_wx[l], rg_bx[l], rg_lambda[l], scan_rows)
        y_b = _retention(pf, pb, ret_norm_g[l], eb)
        y_c = _stickbreak(pb, min(256, s_))
        y_d = _hgrn(pf, hgrn_lb_logits, hgrn_norm_g[l], eb, l, scan_rows)
        ys = [y.reshape(t, GROUP_WIDTH) for y in (y_a, y_b, y_c, y_d)]
        h, hb = _outproj(ys, w_out[l].astype(BF16), h, ln1_g[l], ln1_b[l], alpha, tm)
        h, hb = _mlp(hb, h, w_up[l].astype(BF16), w_down[l].astype(BF16), ln2_g[l], ln2_b[l], alpha, tm, 1024)
    return h.reshape(b_, s_, d).astype(x.dtype)
```

```python
import functools
import math

import jax
import jax.numpy as jnp
from jax import lax
from jax.experimental import pallas as pl
from jax.experimental.pallas import tpu as pltpu

F32 = jnp.float32
BF16 = jnp.bfloat16

HEAD_DIM = 64
N_GROUP_HEADS = 4
GROUP_WIDTH = HEAD_DIM * N_GROUP_HEADS
F32_SLICES = (0, 1, 2, 3, 5, 9, 10, 11, 12)
BF16_SLICES = (4, 6, 7, 8)
A_X, A_G, R_Q, R_K, R_G, D_Q, D_F, D_V, D_G = range(9)
R_V, S_Q, S_K, S_V = range(4)
CONV_WIDTH = 4
RG_LRU_C = 8.0
RET_CHUNK = 128
SB_BLOCK = 128
HGRN_CHUNK = 16
ROPE_BASE = 10000.0
LN_EPS = 1e-5
NORM_EPS = 1e-6
GATE_FLOOR = 1e-30
EXP_F32_ZERO_BELOW = -104.0

VMEM_LIMIT_BYTES = 56 * 1024 * 1024


def _params(*semantics):
    return pltpu.CompilerParams(dimension_semantics=semantics, vmem_limit_bytes=VMEM_LIMIT_BYTES)


def _const_spec(shape):
    zeros = (0,) * len(shape)
    return pl.BlockSpec(shape, lambda *_: zeros)


def _layer_norm_rows(x, g, b):
    mu = jnp.mean(x, axis=-1, keepdims=True)
    xc = x - mu
    var = jnp.mean(xc * xc, axis=-1, keepdims=True)
    return xc * lax.rsqrt(var + LN_EPS) * g + b


def _split2(x):
    hi = x.astype(BF16)
    lo = (x - hi.astype(F32)).astype(BF16)
    return hi, lo


def _head_mean(x, eb):
    hi, lo = _split2(x)
    s = jnp.dot(hi, eb, preferred_element_type=F32) + jnp.dot(lo, eb, preferred_element_type=F32)
    return s * (1.0 / HEAD_DIM)


def _ln_in_kernel(x_ref, g_ref, b_ref, h_ref, hb_ref):
    h = _layer_norm_rows(x_ref[...], g_ref[...], b_ref[...])
    h_ref[...] = h
    hb_ref[...] = h.astype(BF16)


def _ln_in(x2, g, b, tm):
    t, d = x2.shape
    return pl.pallas_call(
        _ln_in_kernel,
        grid=(t // tm,),
        in_specs=[pl.BlockSpec((tm, d), lambda i: (i, 0)), _const_spec((1, d)), _const_spec((1, d))],
        out_specs=[pl.BlockSpec((tm, d), lambda i: (i, 0)), pl.BlockSpec((tm, d), lambda i: (i, 0))],
        out_shape=[jax.ShapeDtypeStruct((t, d), F32), jax.ShapeDtypeStruct((t, d), BF16)],
        compiler_params=_params("parallel"),
        name="ln_in",
    )(x2, g.reshape(1, d), b.reshape(1, d))


def _proj_kernel(h_ref, w_ref, p_ref):
    p_ref[...] = jnp.dot(h_ref[...], w_ref[...], preferred_element_type=F32).astype(p_ref.dtype)


def _proj(hb, w, tm, out_dtype):
    t, d = hb.shape
    n = w.shape[1]
    return pl.pallas_call(
        _proj_kernel,
        grid=(t // tm,),
        in_specs=[pl.BlockSpec((tm, d), lambda i: (i, 0)), _const_spec((d, n))],
        out_specs=pl.BlockSpec((tm, n), lambda i: (i, 0)),
        out_shape=jax.ShapeDtypeStruct((t, n), out_dtype),
        compiler_params=_params("parallel"),
        name="in_proj",
    )(hb, w)


def _block_tail_kernel(alpha, ff_chunk, ya_ref, yb_ref, yc_ref, yd_ref, wo_ref, h_ref, g1_ref, b1_ref,
                       wu_ref, wd_ref, g2_ref, b2_ref, o_ref, ob_ref):
    gw = GROUP_WIDTH
    mix = jnp.dot(ya_ref[...], wo_ref[0 * gw:1 * gw, :], preferred_element_type=F32)
    mix += jnp.dot(yb_ref[...], wo_ref[1 * gw:2 * gw, :], preferred_element_type=F32)
    mix += jnp.dot(yc_ref[...], wo_ref[2 * gw:3 * gw, :], preferred_element_type=F32)
    mix += jnp.dot(yd_ref[...], wo_ref[3 * gw:4 * gw, :], preferred_element_type=F32)
    h1 = _layer_norm_rows(alpha * h_ref[...] + mix, g1_ref[...], b1_ref[...])
    hb = h1.astype(BF16)
    d_ff = wu_ref.shape[1]
    acc = jnp.zeros(h_ref.shape, F32)
    for c in range(d_ff // ff_chunk):
        cols = slice(c * ff_chunk, (c + 1) * ff_chunk)
        u = jnp.dot(hb, wu_ref[:, cols], preferred_element_type=F32)
        u = jnp.square(jnp.maximum(u, 0.0)).astype(BF16)
        acc += jnp.dot(u, wd_ref[cols, :], preferred_element_type=F32)
    h2 = _layer_norm_rows(alpha * h1 + acc, g2_ref[...], b2_ref[...])
    o_ref[...] = h2
    ob_ref[...] = h2.astype(BF16)


def _block_tail(ys, wo, h, g1, b1, wu, wd, g2, b2, alpha, tm, ff_chunk):
    t, d = h.shape
    d_ff = wu.shape[1]
    gw = GROUP_WIDTH
    y_spec = pl.BlockSpec((tm, gw), lambda i: (i, 0))
    row_spec = pl.BlockSpec((tm, d), lambda i: (i, 0))
    resident = lambda shape: pl.BlockSpec(shape, lambda i: (0, 0), pipeline_mode=pl.Buffered(1))
    vec = _const_spec((1, d))
    return pl.pallas_call(
        functools.partial(_block_tail_kernel, alpha, ff_chunk),
        grid=(t // tm,),
        in_specs=[y_spec, y_spec, y_spec, y_spec, resident((d, d)), row_spec, vec, vec,
                  resident((d, d_ff)), resident((d_ff, d)), vec, vec],
        out_specs=[row_spec, row_spec],
        out_shape=[jax.ShapeDtypeStruct((t, d), F32), jax.ShapeDtypeStruct((t, d), BF16)],
        compiler_params=_params("parallel"),
        name="out_proj_mlp",
    )(*ys, wo, h, g1.reshape(1, d), b1.reshape(1, d), wu, wd, g2.reshape(1, d), b2.reshape(1, d))


def _rglru_kernel(xa_ref, ga_ref, cw_ref, cb_ref, wg_ref, bg_ref, lam_ref, o_ref, ext_ref, hprev_ref):
    ts, gw = xa_ref.shape
    si = pl.program_id(1)

    @pl.when(si == 0)
    def _():
        ext_ref[0:8, :] = jnp.zeros((8, gw), F32)
        hprev_ref[...] = jnp.zeros((8, gw), F32)

    @pl.when(si > 0)
    def _():
        ext_ref[0:8, :] = ext_ref[ts:ts + 8, :]

    x = xa_ref[...]
    ext_ref[8:ts + 8, :] = x
    cw = cw_ref[...]
    xc = x * cw[3:4, :] + cb_ref[...]
    for k in range(1, CONV_WIDTH):
        xc += ext_ref[8 - k:8 - k + ts, :] * cw[3 - k:4 - k, :]

    gates = jnp.dot(xc.astype(BF16), wg_ref[...], preferred_element_type=F32) + bg_ref[...]
    r = jax.nn.sigmoid(gates[:, :gw])
    i = jax.nn.sigmoid(gates[:, gw:])
    lam = lam_ref[...]
    log_sig_lam = -(jnp.maximum(-lam, 0.0) + jnp.log1p(jnp.exp(-jnp.abs(lam))))
    log_a = RG_LRU_C * r * log_sig_lam
    a = jnp.exp(log_a)
    th = jnp.tanh(log_a)
    one_minus_a2 = -2.0 * th / (1.0 - th)
    u = jnp.sqrt(jnp.maximum(one_minus_a2, 0.0)) * (i * xc)

    row = lax.broadcasted_iota(jnp.int32, (ts, gw), 0)
    k = 1
    while k < ts:
        live = row >= k
        a_sh = jnp.where(live, pltpu.roll(a, k, 0), 1.0)
        u_sh = jnp.where(live, pltpu.roll(u, k, 0), 0.0)
        u = a * u_sh + u
        a = a * a_sh
        k *= 2
    h = u + a * hprev_ref[7:8, :]
    hprev_ref[...] = h[ts - 8:ts, :]
    o_ref[...] = (jax.nn.gelu(ga_ref[...], approximate=True) * h).astype(BF16)


def _blockdiag_heads(w):
    h, di, dj = w.shape
    eye = jnp.eye(h, dtype=w.dtype)
    return (eye[:, None, :, None] * w[:, :, None, :]).reshape(h * di, h * dj)


def _rglru(proj, conv_w, conv_b, wa, ba, wx, bx, lam, ts):
    b_, s_, _ = proj.shape
    gw = GROUP_WIDTH
    wg = jnp.concatenate([_blockdiag_heads(wa), _blockdiag_heads(wx)], axis=1).astype(BF16)
    bg = jnp.concatenate([ba.reshape(1, gw), bx.reshape(1, gw)], axis=1)
    col = lambda j: pl.BlockSpec((None, ts, gw), lambda b, s: (b, s, j))
    return pl.pallas_call(
        _rglru_kernel,
        grid=(b_, s_ // ts),
        in_specs=[col(A_X), col(A_G), _const_spec((CONV_WIDTH, gw)), _const_spec((1, gw)),
                  _const_spec((gw, 2 * gw)), _const_spec((1, 2 * gw)), _const_spec((1, gw))],
        out_specs=pl.BlockSpec((None, ts, gw), lambda b, s: (b, s, 0)),
        out_shape=jax.ShapeDtypeStruct((b_, s_, gw), BF16),
        scratch_shapes=[pltpu.VMEM((ts + 8, gw), F32), pltpu.VMEM((8, gw), F32)],
        compiler_params=_params("parallel", "arbitrary"),
        name="rglru",
    )(proj, proj, conv_w, conv_b.reshape(1, gw), wg, bg, lam.reshape(1, gw))


def _retention_kernel(q_ref, k_ref, v_ref, g_ref, cos_ref, sin_ref, qdec_ref, kdec_ref, cdec_ref,
                      dmask_ref, ng_ref, eb_ref, o_ref, state_ref):
    c, gw = q_ref.shape

    @pl.when(pl.program_id(1) == 0)
    def _():
        state_ref[...] = jnp.zeros((gw, gw), F32)

    lane = lax.broadcasted_iota(jnp.int32, (c, gw), 1)
    first_half = (lane % HEAD_DIM) < (HEAD_DIM // 2)
    cos = cos_ref[...]
    sin = sin_ref[...]

    def rotary(t):
        partner = jnp.where(first_half, pltpu.roll(t, gw - HEAD_DIM // 2, 1), pltpu.roll(t, HEAD_DIM // 2, 1))
        return t * cos + partner * sin

    q = rotary(q_ref[...])
    k = rotary(k_ref[...]) * (HEAD_DIM ** -0.5)
    vb = v_ref[...]
    eb = eb_ref[...]
    kb = k.astype(BF16)

    o = jnp.dot((q * qdec_ref[...]).astype(BF16), state_ref[...].astype(BF16), preferred_element_type=F32)
    for h in range(N_GROUP_HEADS):
        in_head = (lane // HEAD_DIM) == h
        qh = jnp.where(in_head, q, 0.0).astype(BF16)
        scores = lax.dot_general(qh, kb, (((1,), (1,)), ((), ())), preferred_element_type=F32)
        p = (scores * dmask_ref[h]).astype(BF16)
        o += jnp.where(in_head, jnp.dot(p, vb, preferred_element_type=F32), 0.0)

    kv = lax.dot_general((k * kdec_ref[...]).astype(BF16), vb, (((0,), (0,)), ((), ())),
                         preferred_element_type=F32)
    state_ref[...] = state_ref[...] * cdec_ref[...] + jnp.where(eb > 0, kv, 0.0)

    mu = _head_mean(o, eb)
    oc = o - mu
    var = _head_mean(oc * oc, eb)
    o = oc * lax.rsqrt(var + NORM_EPS) * ng_ref[...]
    o_ref[...] = (jax.nn.silu(g_ref[...]) * o).astype(BF16)


def _retention_tables(s_):
    half = HEAD_DIM // 2
    inv_freq = ROPE_BASE ** (-jnp.arange(0, HEAD_DIM, 2, dtype=F32) / HEAD_DIM)
    ang = jnp.arange(s_, dtype=F32)[:, None] * inv_freq[None, :]
    cos, sin = jnp.cos(ang), jnp.sin(ang)
    cos_t = jnp.tile(jnp.concatenate([cos, cos], axis=-1), (1, N_GROUP_HEADS))
    sin_t = jnp.tile(jnp.concatenate([-sin, sin], axis=-1), (1, N_GROUP_HEADS))
    c_ = RET_CHUNK
    log_gamma = jnp.log1p(-jnp.exp2(-5.0 - jnp.arange(N_GROUP_HEADS, dtype=F32)))
    pos = jnp.arange(c_, dtype=F32)
    diff = pos[:, None] - pos[None, :]
    dmask = jnp.where(diff >= 0, jnp.exp(log_gamma[:, None, None] * jnp.maximum(diff, 0.0)), 0.0)
    lanes = lambda per_head: jnp.repeat(per_head, HEAD_DIM, axis=-1)
    kdec = lanes(jnp.exp(log_gamma[None, :] * (c_ - 1.0 - pos)[:, None]))
    qdec = lanes(jnp.exp(log_gamma[None, :] * (pos + 1.0)[:, None]))
    cdec = lanes(jnp.exp(log_gamma * c_)[None, :])
    del half
    return cos_t, sin_t, qdec, kdec, cdec, dmask


def _retention(proj, proj_b, norm_g, eb):
    b_, s_, _ = proj.shape
    gw = GROUP_WIDTH
    c_ = RET_CHUNK
    cos_t, sin_t, qdec, kdec, cdec, dmask = _retention_tables(s_)
    col = lambda j: pl.BlockSpec((None, c_, gw), lambda b, n: (b, n, j))
    pos_spec = pl.BlockSpec((c_, gw), lambda b, n: (n, 0))
    return pl.pallas_call(
        _retention_kernel,
        grid=(b_, s_ // c_),
        in_specs=[col(R_Q), col(R_K), col(R_V), col(R_G), pos_spec, pos_spec,
                  _const_spec((c_, gw)), _const_spec((c_, gw)), _const_spec((1, gw)),
                  _const_spec((N_GROUP_HEADS, c_, c_)), _const_spec((1, gw)), _const_spec((gw, gw))],
        out_specs=pl.BlockSpec((None, c_, gw), lambda b, n: (b, n, 0)),
        out_shape=jax.ShapeDtypeStruct((b_, s_, gw), BF16),
        scratch_shapes=[pltpu.VMEM((gw, gw), F32)],
        compiler_params=_params("parallel", "arbitrary"),
        name="retention",
    )(proj, proj, proj_b, proj, cos_t, sin_t, qdec, kdec, cdec, dmask, norm_g.reshape(1, gw), eb)


def _pair_blockdiag(x, lane):
    zero = jnp.zeros_like(x)
    return jnp.concatenate([jnp.where(lane < HEAD_DIM, x, zero), jnp.where(lane >= HEAD_DIM, x, zero)], axis=0)


def _stickbreak_kernel(q_ref, k_ref, v_ref, m_ref, o_ref):
    tq = q_ref.shape[0]
    blk = SB_BLOCK
    sub = tq // blk
    pw = 2 * HEAD_DIM
    n_pairs = N_GROUP_HEADS // 2
    i = pl.program_id(1)
    scale = HEAD_DIM ** -0.5
    lane_kv = lax.broadcasted_iota(jnp.int32, (blk, pw), 1)
    q_pos = i * tq + lax.broadcasted_iota(jnp.int32, (tq, 2 * blk), 0)
    key_off = lax.broadcasted_iota(jnp.int32, (tq, 2 * blk), 1) % blk
    m = m_ref[...]
    qs = [q_ref[:, p * pw:(p + 1) * pw] * scale for p in range(n_pairs)]

    def key_block(j, p, masked):
        lanes = slice(p * pw, (p + 1) * pw)
        start = pl.multiple_of(j * blk, blk)
        kbd = _pair_blockdiag(k_ref[pl.ds(start, blk), lanes], lane_kv)
        vbd = _pair_blockdiag(v_ref[pl.ds(start, blk), lanes], lane_kv)
        z = lax.dot_general(qs[p], kbd, (((1,), (1,)), ((), ())), preferred_element_type=F32)
        softplus = jnp.maximum(z, 0.0) + jnp.log(1.0 + jnp.exp(-jnp.abs(z)))
        log_keep = -softplus
        before = None
        if masked:
            before = (j * blk + key_off) < q_pos
            log_keep = jnp.where(before, log_keep, 0.0)
        hi, lo = _split2(log_keep)
        later = jnp.dot(hi, m, preferred_element_type=F32) + jnp.dot(lo, m, preferred_element_type=F32)
        tot = (jnp.sum(log_keep[:, :blk], axis=1, keepdims=True),
               jnp.sum(log_keep[:, blk:], axis=1, keepdims=True))
        return (z - softplus) + later, tot, vbd, before

    def step(j_hi, carry, masked):
        out = []
        for p in range(n_pairs):
            acc, run0, run1 = carry[p]
            parts = [key_block(j_hi - u, p, masked) for u in range(sub)]
            for log_w, tot, vbd, before in parts:
                w = jnp.concatenate([jnp.exp(log_w[:, :blk] + run0), jnp.exp(log_w[:, blk:] + run1)], axis=1)
                if masked:
                    w = jnp.where(before, w, 0.0)
                acc = acc + jnp.dot(w.astype(BF16), vbd, preferred_element_type=F32)
                run0 = run0 + tot[0]
                run1 = run1 + tot[1]
            out.append((acc, run0, run1))
        return tuple(out)

    zero_col = jnp.zeros((tq, 1), F32)
    carry = tuple((jnp.zeros((tq, pw), F32), zero_col, zero_col) for _ in range(n_pairs))
    carry = step(sub * i + sub - 1, carry, True)

    def any_weight_left(c):
        top = functools.reduce(jnp.maximum, [r for _, run0, run1 in c for r in (run0, run1)])
        return (jnp.max(top) > EXP_F32_ZERO_BELOW).astype(jnp.int32)

    def sweep(state):
        t, _, c = state
        c = step(sub * (i - 1 - t) + sub - 1, c, False)
        return t + 1, any_weight_left(c), c

    _, _, carry = lax.while_loop(lambda state: (state[0] < i) & (state[1] > 0), sweep,
                                 (jnp.int32(0), any_weight_left(carry), carry))
    for p in range(n_pairs):
        o_ref[:, p * pw:(p + 1) * pw] = carry[p][0].astype(BF16)


def _stickbreak(proj_b, tq):
    b_, s_, _ = proj_b.shape
    gw = GROUP_WIDTH
    blk = SB_BLOCK
    idx = jnp.arange(2 * blk)
    same_head = (idx[:, None] // blk) == (idx[None, :] // blk)
    m = (same_head & (idx[:, None] > idx[None, :])).astype(BF16)
    return pl.pallas_call(
        _stickbreak_kernel,
        grid=(b_, s_ // tq),
        in_specs=[pl.BlockSpec((None, tq, gw), lambda b, i: (b, i, S_Q)),
                  pl.BlockSpec((None, s_, gw), lambda b, i: (b, 0, S_K)),
                  pl.BlockSpec((None, s_, gw), lambda b, i: (b, 0, S_V)),
                  _const_spec((2 * blk, 2 * blk))],
        out_specs=pl.BlockSpec((None, tq, gw), lambda b, i: (b, i, 0)),
        out_shape=jax.ShapeDtypeStruct((b_, s_, gw), BF16),
        compiler_params=_params("parallel", "arbitrary"),
        name="stickbreak",
    )(proj_b, proj_b, proj_b, m)


def _split3(x):
    p1 = x.astype(BF16)
    r1 = x - p1.astype(F32)
    p2 = r1.astype(BF16)
    p3 = (r1 - p2.astype(F32)).astype(BF16)
    return p1, p2, p3


def _dot3(a, parts):
    return sum(jnp.dot(a, p, preferred_element_type=F32) for p in parts)


def _hgrn_kernel(layer, q_ref, f_ref, v_ref, g_ref, lbl_ref, ng_ref, tri_ref, sel_ref, bias_ref, eb_ref, o_ref,
                 state_ref, b2_ref, kk_ref, w_ref, gs_ref, oi_ref):
    ts, gw = q_ref.shape
    c = HGRN_CHUNK
    n_chunks = ts // c
    pw = 2 * HEAD_DIM
    n_pairs = N_GROUP_HEADS // 2

    @pl.when(pl.program_id(1) == 0)
    def _():
        state_ref[...] = jnp.zeros(state_ref.shape, F32)

    logits = lbl_ref[...]
    e = jnp.exp(logits - jnp.max(logits, axis=0, keepdims=True))
    lb_p = e / jnp.sum(e, axis=0, keepdims=True)
    lb = jnp.sum(lb_p[1:layer + 1, :], axis=0, keepdims=True) if layer > 0 else jnp.zeros((1, gw), F32)

    f_pre = f_ref[...]
    q = q_ref[...]
    f_gate = lb + (1.0 - lb) * jax.nn.sigmoid(f_pre)
    log_f = jnp.log(jnp.maximum(f_gate, GATE_FLOOR))
    kk = (1.0 - lb) * jax.nn.sigmoid(-f_pre)
    parts = _split3(log_f)
    b = _dot3(tri_ref[...], parts)
    b_tot = _dot3(sel_ref[...], parts)
    qe = (q * jnp.exp(b)).astype(BF16)
    kd = (kk * jnp.exp(b_tot - b)).astype(BF16)
    decay = jnp.exp(b_tot)
    b2_ref[...] = b * math.log2(math.e)
    kk_ref[...] = kk

    def fill(n, _):
        r0 = pl.multiple_of(n * c, c)
        bb = b2_ref[pl.ds(r0, c), :]
        qb = q_ref[pl.ds(r0, c), :]
        for s in range(c):
            bs = b2_ref[pl.ds(r0 + s, 1), :]
            ks = kk_ref[pl.ds(r0 + s, 1), :]
            w = jnp.exp2(jnp.minimum(bb - bs, bias_ref[s])) * (qb * ks)
            w_ref[pl.ds(pl.multiple_of(n * (c * c) + s * c, c), c), :] = w.astype(BF16)
        return 0

    lax.fori_loop(0, n_chunks, fill, 0)
    eb = eb_ref[...]
    gs_ref[...] = jnp.dot(w_ref[...], eb, preferred_element_type=F32)

    def reduce_keys(n, _):
        r0 = pl.multiple_of(n * c, c)
        acc = jnp.zeros((c, gw), F32)
        for s in range(c):
            acc += gs_ref[pl.ds(pl.multiple_of(n * (c * c) + s * c, c), c), :] * v_ref[pl.ds(r0 + s, 1), :]
        oi_ref[pl.ds(r0, c), :] = acc
        return 0

    lax.fori_loop(0, n_chunks, reduce_keys, 0)

    vb = v_ref[...].astype(BF16)
    same_head = eb[0:pw, 0:pw] > 0
    rows = [slice(n * c, (n + 1) * c) for n in range(n_chunks)]
    lanes = [slice(p * pw, (p + 1) * pw) for p in range(n_pairs)]
    kv_t = [[lax.dot_general(vb[r, l], kd[r, l], (((0,), (0,)), ((), ())), preferred_element_type=F32)
             for r in rows] for l in lanes]
    states = []
    for p in range(n_pairs):
        state = state_ref[p]
        entering = []
        for n in range(n_chunks):
            entering.append(state.astype(BF16))
            state = state * decay[n * c:n * c + 1, lanes[p]] + jnp.where(same_head, kv_t[p][n], 0.0)
        state_ref[p] = state
        states.append(entering)
    o_state = [jnp.concatenate([lax.dot_general(qe[rows[n], lanes[p]], states[p][n], (((1,), (1,)), ((), ())),
                                                preferred_element_type=F32) for n in range(n_chunks)], axis=0)
               for p in range(n_pairs)]

    o = jnp.concatenate(o_state, axis=1) + oi_ref[...]
    ms = _head_mean(o * o, eb)
    o = o * lax.rsqrt(ms + NORM_EPS)
    o_ref[...] = (o * ng_ref[...] * jax.nn.silu(g_ref[...])).astype(BF16)


def _hgrn(proj, lb_logits, norm_g, eb, layer, ts):
    b_, s_, _ = proj.shape
    gw = GROUP_WIDTH
    c = HGRN_CHUNK
    depth = lb_logits.shape[0]
    row = jnp.arange(ts)
    same_chunk = (row[:, None] // c) == (row[None, :] // c)
    tri = (same_chunk & (row[:, None] >= row[None, :])).astype(BF16)
    sel = same_chunk.astype(BF16)
    pos = jnp.arange(c)
    bias = jnp.where(pos[None, :, None] >= pos[:, None, None], 0.0, -1e30)
    bias = jnp.broadcast_to(bias, (c, c, gw)).astype(F32)
    col = lambda j: pl.BlockSpec((None, ts, gw), lambda b, s: (b, s, j))
    return pl.pallas_call(
        functools.partial(_hgrn_kernel, layer),
        grid=(b_, s_ // ts),
        in_specs=[col(D_Q), col(D_F), col(D_V), col(D_G), _const_spec((depth, gw)), _const_spec((1, gw)),
                  _const_spec((ts, ts)), _const_spec((ts, ts)), _const_spec((c, c, gw)), _const_spec((gw, gw))],
        out_specs=pl.BlockSpec((None, ts, gw), lambda b, s: (b, s, 0)),
        out_shape=jax.ShapeDtypeStruct((b_, s_, gw), BF16),
        scratch_shapes=[pltpu.VMEM((N_GROUP_HEADS // 2, 2 * HEAD_DIM, 2 * HEAD_DIM), F32),
                        pltpu.VMEM((ts, gw), F32), pltpu.VMEM((ts, gw), F32),
                        pltpu.VMEM((ts * c, gw), BF16), pltpu.VMEM((ts * c, gw), F32), pltpu.VMEM((ts, gw), F32)],
        compiler_params=_params("parallel", "arbitrary"),
        name="hgrn2",
    )(proj, proj, proj, proj, lb_logits, norm_g.reshape(1, gw), tri, sel, bias, eb)


def kernel(x, ln_in_g, ln_in_b, w_in, conv_w, conv_b, rg_wa, rg_ba, rg_wx, rg_bx, rg_lambda, ret_norm_g,
           hgrn_lb_logits, hgrn_norm_g, w_out, ln1_g, ln1_b, w_up, w_down, ln2_g, ln2_b):
    b_, s_, d = x.shape
    depth = w_in.shape[0]
    t = b_ * s_
    alpha = (2 * depth) ** 0.25
    tm = min(512, t)
    scan_rows = min(256, s_)

    head = jnp.arange(GROUP_WIDTH) // HEAD_DIM
    eb = (head[:, None] == head[None, :]).astype(BF16)

    h, hb = _ln_in(x.reshape(t, d), ln_in_g, ln_in_b, tm)
    gw = GROUP_WIDTH
    columns = lambda w, slices: jnp.concatenate([w[:, s * gw:(s + 1) * gw] for s in slices], axis=1).astype(BF16)
    for l in range(depth):
        pf = _proj(hb, columns(w_in[l], F32_SLICES), tm, F32).reshape(b_, s_, -1)
        pb = _proj(hb, columns(w_in[l], BF16_SLICES), tm, BF16).reshape(b_, s_, -1)
        y_a = _rglru(pf, conv_w[l], conv_b[l], rg_wa[l], rg_ba[l], rg_wx[l], rg_bx[l], rg_lambda[l], scan_rows)
        y_b = _retention(pf, pb, ret_norm_g[l], eb)
        y_c = _stickbreak(pb, min(256, s_))
        y_d = _hgrn(pf, hgrn_lb_logits, hgrn_norm_g[l], eb, l, scan_rows)
        ys = [y.reshape(t, gw) for y in (y_a, y_b, y_c, y_d)]
        h, hb = _block_tail(ys, w_out[l].astype(BF16), h, ln1_g[l], ln1_b[l], w_up[l].astype(BF16),
                            w_down[l].astype(BF16), ln2_g[l], ln2_b[l], alpha, tm, 1024)
    return h.reshape(b_, s_, d).astype(x.dtype)
```

```python
import functools
import math

import jax
import jax.numpy as jnp
from jax import lax
from jax.experimental import pallas as pl
from jax.experimental.pallas import tpu as pltpu

F32 = jnp.float32
BF16 = jnp.bfloat16

HEAD_DIM = 64
N_GROUP_HEADS = 4
GROUP_WIDTH = HEAD_DIM * N_GROUP_HEADS
F32_SLICES = (0, 1, 2, 3, 5, 9, 10, 11, 12)
BF16_SLICES = (4, 6, 7, 8)
A_X, A_G, R_Q, R_K, R_G, D_Q, D_F, D_V, D_G = range(9)
R_V, S_Q, S_K, S_V = range(4)
CONV_WIDTH = 4
RG_LRU_C = 8.0
RET_CHUNK = 128
SB_BLOCK = 128
HGRN_CHUNK = 16
ROPE_BASE = 10000.0
LN_EPS = 1e-5
NORM_EPS = 1e-6
GATE_FLOOR = 1e-30
EXP_F32_ZERO_BELOW = -104.0

VMEM_LIMIT_BYTES = 56 * 1024 * 1024
SUBLANES = 8


def _params(*semantics):
    return pltpu.CompilerParams(dimension_semantics=semantics, vmem_limit_bytes=VMEM_LIMIT_BYTES)


def _const_spec(shape):
    zeros = (0,) * len(shape)
    return pl.BlockSpec(shape, lambda *_: zeros)


def _layer_norm_rows(x, g, b):
    mu = jnp.mean(x, axis=-1, keepdims=True)
    xc = x - mu
    var = jnp.mean(xc * xc, axis=-1, keepdims=True)
    return xc * lax.rsqrt(var + LN_EPS) * g + b


def _split2(x):
    hi = x.astype(BF16)
    lo = (x - hi.astype(F32)).astype(BF16)
    return hi, lo


def _head_mean(x, eb):
    hi, lo = _split2(x)
    s = jnp.dot(hi, eb, preferred_element_type=F32) + jnp.dot(lo, eb, preferred_element_type=F32)
    return s * (1.0 / HEAD_DIM)


def _ln_in_kernel(x_ref, g_ref, b_ref, h_ref, hb_ref):
    h = _layer_norm_rows(x_ref[...], g_ref[...], b_ref[...])
    h_ref[...] = h
    hb_ref[...] = h.astype(BF16)


def _ln_in(x2, g, b, tm):
    t, d = x2.shape
    return pl.pallas_call(
        _ln_in_kernel,
        grid=(t // tm,),
        in_specs=[pl.BlockSpec((tm, d), lambda i: (i, 0)), _const_spec((1, d)), _const_spec((1, d))],
        out_specs=[pl.BlockSpec((tm, d), lambda i: (i, 0)), pl.BlockSpec((tm, d), lambda i: (i, 0))],
        out_shape=[jax.ShapeDtypeStruct((t, d), F32), jax.ShapeDtypeStruct((t, d), BF16)],
        compiler_params=_params("parallel"),
        name="ln_in",
    )(x2, g.reshape(1, d), b.reshape(1, d))


def _proj_kernel(h_ref, wf_ref, wb_ref, pf_ref, pb_ref):
    h = h_ref[...]
    pf_ref[...] = jnp.dot(h, wf_ref[...], preferred_element_type=F32)
    pb_ref[...] = jnp.dot(h, wb_ref[...], preferred_element_type=F32).astype(BF16)


def _proj(hb, wf, wb, tm):
    t, d = hb.shape
    nf, nb = wf.shape[1], wb.shape[1]
    resident = lambda shape: pl.BlockSpec(shape, lambda i: (0, 0), pipeline_mode=pl.Buffered(1))
    return pl.pallas_call(
        _proj_kernel,
        grid=(t // tm,),
        in_specs=[pl.BlockSpec((tm, d), lambda i: (i, 0)), resident((d, nf)), resident((d, nb))],
        out_specs=[pl.BlockSpec((tm, nf), lambda i: (i, 0)), pl.BlockSpec((tm, nb), lambda i: (i, 0))],
        out_shape=[jax.ShapeDtypeStruct((t, nf), F32), jax.ShapeDtypeStruct((t, nb), BF16)],
        compiler_params=_params("parallel"),
        name="in_proj",
    )(hb, wf, wb)


def _block_tail_kernel(alpha, ff_chunk, ya_ref, yb_ref, yc_ref, yd_ref, wo_ref, h_ref, g1_ref, b1_ref,
                       wu_ref, wd_ref, g2_ref, b2_ref, o_ref, ob_ref):
    gw = GROUP_WIDTH
    mix = jnp.dot(ya_ref[...], wo_ref[0 * gw:1 * gw, :], preferred_element_type=F32)
    mix += jnp.dot(yb_ref[...], wo_ref[1 * gw:2 * gw, :], preferred_element_type=F32)
    mix += jnp.dot(yc_ref[...], wo_ref[2 * gw:3 * gw, :], preferred_element_type=F32)
    mix += jnp.dot(yd_ref[...], wo_ref[3 * gw:4 * gw, :], preferred_element_type=F32)
    h1 = _layer_norm_rows(alpha * h_ref[...] + mix, g1_ref[...], b1_ref[...])
    hb = h1.astype(BF16)
    d_ff = wu_ref.shape[1]
    acc = jnp.zeros(h_ref.shape, F32)
    for c in range(d_ff // ff_chunk):
        cols = slice(c * ff_chunk, (c + 1) * ff_chunk)
        u = jnp.dot(hb, wu_ref[:, cols], preferred_element_type=F32)
        u = jnp.square(jnp.maximum(u, 0.0)).astype(BF16)
        acc += jnp.dot(u, wd_ref[cols, :], preferred_element_type=F32)
    h2 = _layer_norm_rows(alpha * h1 + acc, g2_ref[...], b2_ref[...])
    o_ref[...] = h2
    ob_ref[...] = h2.astype(BF16)


def _block_tail(ys, wo, h, g1, b1, wu, wd, g2, b2, alpha, tm, ff_chunk):
    t, d = h.shape
    d_ff = wu.shape[1]
    gw = GROUP_WIDTH
    y_spec = pl.BlockSpec((tm, gw), lambda i: (i, 0))
    row_spec = pl.BlockSpec((tm, d), lambda i: (i, 0))
    resident = lambda shape: pl.BlockSpec(shape, lambda i: (0, 0), pipeline_mode=pl.Buffered(1))
    vec = _const_spec((1, d))
    return pl.pallas_call(
        functools.partial(_block_tail_kernel, alpha, ff_chunk),
        grid=(t // tm,),
        in_specs=[y_spec, y_spec, y_spec, y_spec, resident((d, d)), row_spec, vec, vec,
                  resident((d, d_ff)), resident((d_ff, d)), vec, vec],
        out_specs=[row_spec, row_spec],
        out_shape=[jax.ShapeDtypeStruct((t, d), F32), jax.ShapeDtypeStruct((t, d), BF16)],
        compiler_params=_params("parallel"),
        name="out_proj_mlp",
    )(*ys, wo, h, g1.reshape(1, d), b1.reshape(1, d), wu, wd, g2.reshape(1, d), b2.reshape(1, d))


def _rglru_kernel(xa_ref, ga_ref, cw_ref, cb_ref, wg_ref, bg_ref, lam_ref, o_ref, ext_ref, hprev_ref):
    ts, gw = xa_ref.shape
    si = pl.program_id(1)

    pad = SUBLANES

    @pl.when(si == 0)
    def _():
        ext_ref[0:pad, :] = jnp.zeros((pad, gw), F32)
        hprev_ref[...] = jnp.zeros((SUBLANES, gw), F32)

    @pl.when(si > 0)
    def _():
        ext_ref[0:pad, :] = ext_ref[ts:ts + pad, :]

    x = xa_ref[...]
    ext_ref[pad:ts + pad, :] = x
    cw = cw_ref[...]
    last = CONV_WIDTH - 1
    xc = x * cw[last:last + 1, :] + cb_ref[...]
    for k in range(1, CONV_WIDTH):
        xc += ext_ref[pad - k:pad - k + ts, :] * cw[last - k:last - k + 1, :]

    gates = jnp.dot(xc.astype(BF16), wg_ref[...], preferred_element_type=F32) + bg_ref[...]
    r = jax.nn.sigmoid(gates[:, :gw])
    i = jax.nn.sigmoid(gates[:, gw:])
    lam = lam_ref[...]
    log_sig_lam = -(jnp.maximum(-lam, 0.0) + jnp.log1p(jnp.exp(-jnp.abs(lam))))
    log_a = RG_LRU_C * r * log_sig_lam
    a = jnp.exp(log_a)
    th = jnp.tanh(log_a)
    one_minus_a2 = -2.0 * th / (1.0 - th)
    u = jnp.sqrt(jnp.maximum(one_minus_a2, 0.0)) * (i * xc)

    sub = lax.broadcasted_iota(jnp.int32, (SUBLANES, gw), 0)
    h_in = hprev_ref[SUBLANES - 1:SUBLANES, :]
    tiles = []
    for t0 in range(0, ts, SUBLANES):
        a_t = a[t0:t0 + SUBLANES, :]
        u_t = u[t0:t0 + SUBLANES, :]
        k = 1
        while k < SUBLANES:
            live = sub >= k
            a_sh = jnp.where(live, pltpu.roll(a_t, k, 0), 1.0)
            u_sh = jnp.where(live, pltpu.roll(u_t, k, 0), 0.0)
            u_t = a_t * u_sh + u_t
            a_t = a_t * a_sh
            k *= 2
        h_t = u_t + a_t * h_in
        h_in = h_t[SUBLANES - 1:SUBLANES, :]
        tiles.append(h_t)
    h = jnp.concatenate(tiles, axis=0)
    hprev_ref[...] = tiles[-1]
    o_ref[...] = (jax.nn.gelu(ga_ref[...], approximate=True) * h).astype(BF16)


def _blockdiag_heads(w):
    h, di, dj = w.shape
    eye = jnp.eye(h, dtype=w.dtype)
    return (eye[:, None, :, None] * w[:, :, None, :]).reshape(h * di, h * dj)


def _rglru(proj, conv_w, conv_b, wa, ba, wx, bx, lam, ts):
    b_, s_, _ = proj.shape
    gw = GROUP_WIDTH
    wg = jnp.concatenate([_blockdiag_heads(wa), _blockdiag_heads(wx)], axis=1).astype(BF16)
    bg = jnp.concatenate([ba.reshape(1, gw), bx.reshape(1, gw)], axis=1)
    col = lambda j: pl.BlockSpec((None, ts, gw), lambda b, s: (b, s, j))
    return pl.pallas_call(
        _rglru_kernel,
        grid=(b_, s_ // ts),
        in_specs=[col(A_X), col(A_G), _const_spec((CONV_WIDTH, gw)), _const_spec((1, gw)),
                  _const_spec((gw, 2 * gw)), _const_spec((1, 2 * gw)), _const_spec((1, gw))],
        out_specs=pl.BlockSpec((None, ts, gw), lambda b, s: (b, s, 0)),
        out_shape=jax.ShapeDtypeStruct((b_, s_, gw), BF16),
        scratch_shapes=[pltpu.VMEM((ts + SUBLANES, gw), F32), pltpu.VMEM((SUBLANES, gw), F32)],
        compiler_params=_params("parallel", "arbitrary"),
        name="rglru",
    )(proj, proj, conv_w, conv_b.reshape(1, gw), wg, bg, lam.reshape(1, gw))


def _retention_kernel(q_ref, k_ref, v_ref, g_ref, cos_ref, sin_ref, qdec_ref, kdec_ref, cdec_ref,
                      dmask_ref, ng_ref, eb_ref, o_ref, state_ref):
    ts, gw = q_ref.shape
    c = RET_CHUNK
    pw = 2 * HEAD_DIM
    n_pairs = N_GROUP_HEADS // 2

    @pl.when(pl.program_id(1) == 0)
    def _():
        state_ref[...] = jnp.zeros((gw, gw), F32)

    lane = lax.broadcasted_iota(jnp.int32, (ts, gw), 1)
    first_half = (lane % HEAD_DIM) < (HEAD_DIM // 2)
    cos = cos_ref[...]
    sin = sin_ref[...]

    def rotary(t):
        partner = jnp.where(first_half, pltpu.roll(t, gw - HEAD_DIM // 2, 1), pltpu.roll(t, HEAD_DIM // 2, 1))
        return t * cos + partner * sin

    q = rotary(q_ref[...])
    k = rotary(k_ref[...]) * (HEAD_DIM ** -0.5)
    qb = q.astype(BF16)
    kb = k.astype(BF16)
    q_dec = (q * qdec_ref[...]).astype(BF16)
    k_dec = (k * kdec_ref[...]).astype(BF16)
    vb = v_ref[...]
    eb = eb_ref[...]
    same_head = eb > 0
    lane_kv = lax.broadcasted_iota(jnp.int32, (c, pw), 1)
    chunks = [slice(n * c, (n + 1) * c) for n in range(ts // c)]

    kv = [lax.dot_general(k_dec[r, :], vb[r, :], (((0,), (0,)), ((), ())), preferred_element_type=F32)
          for r in chunks]
    state = state_ref[...]
    entering = []
    for kv_n in kv:
        entering.append(state.astype(BF16))
        state = state * cdec_ref[...] + jnp.where(same_head, kv_n, 0.0)
    state_ref[...] = state

    outs = []
    for r, state_n in zip(chunks, entering):
        inter = jnp.dot(q_dec[r, :], state_n, preferred_element_type=F32)
        intra = []
        for p in range(n_pairs):
            lanes = slice(p * pw, (p + 1) * pw)
            kbd = _pair_blockdiag(kb[r, lanes], lane_kv)
            vbd = _pair_blockdiag(vb[r, lanes], lane_kv)
            scores = lax.dot_general(qb[r, lanes], kbd, (((1,), (1,)), ((), ())), preferred_element_type=F32)
            intra.append(jnp.dot((scores * dmask_ref[p]).astype(BF16), vbd, preferred_element_type=F32))
        outs.append(inter + jnp.concatenate(intra, axis=1))
    o = jnp.concatenate(outs, axis=0)

    mu = _head_mean(o, eb)
    oc = o - mu
    var = _head_mean(oc * oc, eb)
    o = oc * lax.rsqrt(var + NORM_EPS) * ng_ref[...]
    o_ref[...] = (jax.nn.silu(g_ref[...]) * o).astype(BF16)


def _retention_tables(s_, rows):
    inv_freq = ROPE_BASE ** (-jnp.arange(0, HEAD_DIM, 2, dtype=F32) / HEAD_DIM)
    ang = jnp.arange(s_, dtype=F32)[:, None] * inv_freq[None, :]
    cos, sin = jnp.cos(ang), jnp.sin(ang)
    cos_t = jnp.tile(jnp.concatenate([cos, cos], axis=-1), (1, N_GROUP_HEADS))
    sin_t = jnp.tile(jnp.concatenate([-sin, sin], axis=-1), (1, N_GROUP_HEADS))
    c_ = RET_CHUNK
    log_gamma = jnp.log1p(-jnp.exp2(-5.0 - jnp.arange(N_GROUP_HEADS, dtype=F32)))
    pos = jnp.arange(c_, dtype=F32)
    diff = pos[:, None] - pos[None, :]
    dmask = jnp.where(diff >= 0, jnp.exp(log_gamma[:, None, None] * jnp.maximum(diff, 0.0)), 0.0)
    lanes = lambda per_head: jnp.repeat(per_head, HEAD_DIM, axis=-1)
    kdec = lanes(jnp.exp(log_gamma[None, :] * (c_ - 1.0 - pos)[:, None]))
    qdec = lanes(jnp.exp(log_gamma[None, :] * (pos + 1.0)[:, None]))
    cdec = lanes(jnp.exp(log_gamma * c_)[None, :])
    dmask = dmask.reshape(N_GROUP_HEADS // 2, 2, c_, c_).transpose(0, 2, 1, 3).reshape(N_GROUP_HEADS // 2, c_, 2 * c_)
    return cos_t, sin_t, jnp.tile(qdec, (rows // c_, 1)), jnp.tile(kdec, (rows // c_, 1)), cdec, dmask


def _retention(proj, proj_b, norm_g, eb, ts):
    b_, s_, _ = proj.shape
    gw = GROUP_WIDTH
    c_ = RET_CHUNK
    cos_t, sin_t, qdec, kdec, cdec, dmask = _retention_tables(s_, ts)
    col = lambda j: pl.BlockSpec((None, ts, gw), lambda b, n: (b, n, j))
    pos_spec = pl.BlockSpec((ts, gw), lambda b, n: (n, 0))
    return pl.pallas_call(
        _retention_kernel,
        grid=(b_, s_ // ts),
        in_specs=[col(R_Q), col(R_K), col(R_V), col(R_G), pos_spec, pos_spec,
                  _const_spec((ts, gw)), _const_spec((ts, gw)), _const_spec((1, gw)),
                  _const_spec((N_GROUP_HEADS // 2, c_, 2 * c_)), _const_spec((1, gw)), _const_spec((gw, gw))],
        out_specs=pl.BlockSpec((None, ts, gw), lambda b, n: (b, n, 0)),
        out_shape=jax.ShapeDtypeStruct((b_, s_, gw), BF16),
        scratch_shapes=[pltpu.VMEM((gw, gw), F32)],
        compiler_params=_params("parallel", "arbitrary"),
        name="retention",
    )(proj, proj, proj_b, proj, cos_t, sin_t, qdec, kdec, cdec, dmask, norm_g.reshape(1, gw), eb)


def _pair_blockdiag(x, lane):
    zero = jnp.zeros_like(x)
    return jnp.concatenate([jnp.where(lane < HEAD_DIM, x, zero), jnp.where(lane >= HEAD_DIM, x, zero)], axis=0)


def _stickbreak_kernel(q_ref, k_ref, v_ref, m_ref, o_ref):
    tq = q_ref.shape[0]
    blk = SB_BLOCK
    sub = tq // blk
    pw = 2 * HEAD_DIM
    n_pairs = N_GROUP_HEADS // 2
    i = pl.program_id(1)
    scale = HEAD_DIM ** -0.5
    lane_kv = lax.broadcasted_iota(jnp.int32, (blk, pw), 1)
    q_pos = i * tq + lax.broadcasted_iota(jnp.int32, (tq, 2 * blk), 0)
    key_off = lax.broadcasted_iota(jnp.int32, (tq, 2 * blk), 1) % blk
    m = m_ref[...]
    qs = [q_ref[:, p * pw:(p + 1) * pw] * scale for p in range(n_pairs)]

    def key_block(j, p, masked):
        lanes = slice(p * pw, (p + 1) * pw)
        start = pl.multiple_of(j * blk, blk)
        kbd = _pair_blockdiag(k_ref[pl.ds(start, blk), lanes], lane_kv)
        vbd = _pair_blockdiag(v_ref[pl.ds(start, blk), lanes], lane_kv)
        z = lax.dot_general(qs[p], kbd, (((1,), (1,)), ((), ())), preferred_element_type=F32)
        softplus = jnp.maximum(z, 0.0) + jnp.log(1.0 + jnp.exp(-jnp.abs(z)))
        log_keep = -softplus
        before = None
        if masked:
            before = (j * blk + key_off) < q_pos
            log_keep = jnp.where(before, log_keep, 0.0)
        hi, lo = _split2(log_keep)
        later = jnp.dot(hi, m, preferred_element_type=F32) + jnp.dot(lo, m, preferred_element_type=F32)
        tot = (jnp.sum(log_keep[:, :blk], axis=1, keepdims=True),
               jnp.sum(log_keep[:, blk:], axis=1, keepdims=True))
        return (z - softplus) + later, tot, vbd, before

    def step(j_hi, carry, masked):
        out = []
        for p in range(n_pairs):
            acc, run0, run1 = carry[p]
            parts = [key_block(j_hi - u, p, masked) for u in range(sub)]
            for log_w, tot, vbd, before in parts:
                w = jnp.concatenate([jnp.exp(log_w[:, :blk] + run0), jnp.exp(log_w[:, blk:] + run1)], axis=1)
                if masked:
                    w = jnp.where(before, w, 0.0)
                acc = acc + jnp.dot(w.astype(BF16), vbd, preferred_element_type=F32)
                run0 = run0 + tot[0]
                run1 = run1 + tot[1]
            out.append((acc, run0, run1))
        return tuple(out)

    zero_col = jnp.zeros((tq, 1), F32)
    carry = tuple((jnp.zeros((tq, pw), F32), zero_col, zero_col) for _ in range(n_pairs))
    carry = step(sub * i + sub - 1, carry, True)

    def any_weight_left(c):
        top = functools.reduce(jnp.maximum, [r for _, run0, run1 in c for r in (run0, run1)])
        return (jnp.max(top) > EXP_F32_ZERO_BELOW).astype(jnp.int32)

    def sweep(state):
        t, _, c = state
        c = step(sub * (i - 1 - t) + sub - 1, c, False)
        return t + 1, any_weight_left(c), c

    _, _, carry = lax.while_loop(lambda state: (state[0] < i) & (state[1] > 0), sweep,
                                 (jnp.int32(0), any_weight_left(carry), carry))
    for p in range(n_pairs):
        o_ref[:, p * pw:(p + 1) * pw] = carry[p][0].astype(BF16)


def _stickbreak(proj_b, tq):
    b_, s_, _ = proj_b.shape
    gw = GROUP_WIDTH
    blk = SB_BLOCK
    idx = jnp.arange(2 * blk)
    same_head = (idx[:, None] // blk) == (idx[None, :] // blk)
    m = (same_head & (idx[:, None] > idx[None, :])).astype(BF16)
    return pl.pallas_call(
        _stickbreak_kernel,
        grid=(b_, s_ // tq),
        in_specs=[pl.BlockSpec((None, tq, gw), lambda b, i: (b, i, S_Q)),
                  pl.BlockSpec((None, s_, gw), lambda b, i: (b, 0, S_K)),
                  pl.BlockSpec((None, s_, gw), lambda b, i: (b, 0, S_V)),
                  _const_spec((2 * blk, 2 * blk))],
        out_specs=pl.BlockSpec((None, tq, gw), lambda b, i: (b, i, 0)),
        out_shape=jax.ShapeDtypeStruct((b_, s_, gw), BF16),
        compiler_params=_params("parallel", "arbitrary"),
        name="stickbreak",
    )(proj_b, proj_b, proj_b, m)


def _split3(x):
    p1 = x.astype(BF16)
    r1 = x - p1.astype(F32)
    p2 = r1.astype(BF16)
    p3 = (r1 - p2.astype(F32)).astype(BF16)
    return p1, p2, p3


def _dot3(a, parts):
    return sum(jnp.dot(a, p, preferred_element_type=F32) for p in parts)


def _hgrn_kernel(layer, q_ref, f_ref, v_ref, g_ref, lbl_ref, ng_ref, tri_ref, sel_ref, bias_ref, eb_ref, o_ref,
                 state_ref, b2_ref, kk_ref):
    ts, gw = q_ref.shape
    c = HGRN_CHUNK
    n_chunks = ts // c
    pw = 2 * HEAD_DIM
    n_pairs = N_GROUP_HEADS // 2

    @pl.when(pl.program_id(1) == 0)
    def _():
        state_ref[...] = jnp.zeros(state_ref.shape, F32)

    logits = lbl_ref[...]
    e = jnp.exp(logits - jnp.max(logits, axis=0, keepdims=True))
    lb_p = e / jnp.sum(e, axis=0, keepdims=True)
    lb = jnp.sum(lb_p[1:layer + 1, :], axis=0, keepdims=True) if layer > 0 else jnp.zeros((1, gw), F32)

    f_pre = f_ref[...]
    q = q_ref[...]
    f_gate = lb + (1.0 - lb) * jax.nn.sigmoid(f_pre)
    log_f = jnp.log(jnp.maximum(f_gate, GATE_FLOOR))
    kk = (1.0 - lb) * jax.nn.sigmoid(-f_pre)
    parts = _split3(log_f)
    b = _dot3(tri_ref[...], parts)
    b_tot = _dot3(sel_ref[...], parts)
    qe = (q * jnp.exp(b)).astype(BF16)
    kd = (kk * jnp.exp(b_tot - b)).astype(BF16)
    decay = jnp.exp(b_tot)
    b2_ref[...] = b * math.log2(math.e)
    kk_ref[...] = kk

    eb = eb_ref[...]
    half = c // 2
    o_intra = []
    for n in range(n_chunks):
        r0 = n * c
        blocks = []
        for s in range(c):
            lo = 0 if s < half else half
            bs = b2_ref[r0 + s:r0 + s + 1, :]
            ks = kk_ref[r0 + s:r0 + s + 1, :]
            decay_ts = jnp.exp2(jnp.minimum(b2_ref[r0 + lo:r0 + c, :] - bs, bias_ref[s, lo:c, :]))
            blocks.append(decay_ts * (q_ref[r0 + lo:r0 + c, :] * ks))
        g = jnp.dot(jnp.concatenate(blocks, axis=0).astype(BF16), eb, preferred_element_type=F32)
        top = jnp.zeros((half, gw), F32)
        bottom = jnp.zeros((half, gw), F32)
        for s in range(c):
            vs = v_ref[r0 + s:r0 + s + 1, :]
            if s < half:
                top += g[s * c:s * c + half, :] * vs
                bottom += g[s * c + half:(s + 1) * c, :] * vs
            else:
                start = half * c + (s - half) * half
                bottom += g[start:start + half, :] * vs
        o_intra.append(top)
        o_intra.append(bottom)
    o_intra = jnp.concatenate(o_intra, axis=0)

    vb = v_ref[...].astype(BF16)
    same_head = eb[0:pw, 0:pw] > 0
    rows = [slice(n * c, (n + 1) * c) for n in range(n_chunks)]
    lanes = [slice(p * pw, (p + 1) * pw) for p in range(n_pairs)]
    kv_t = [[lax.dot_general(vb[r, l], kd[r, l], (((0,), (0,)), ((), ())), preferred_element_type=F32)
             for r in rows] for l in lanes]
    states = []
    for p in range(n_pairs):
        state = state_ref[p]
        entering = []
        for n in range(n_chunks):
            entering.append(state.astype(BF16))
            state = state * decay[n * c:n * c + 1, lanes[p]] + jnp.where(same_head, kv_t[p][n], 0.0)
        state_ref[p] = state
        states.append(entering)
    o_state = [jnp.concatenate([lax.dot_general(qe[rows[n], lanes[p]], states[p][n], (((1,), (1,)), ((), ())),
                                                preferred_element_type=F32) for n in range(n_chunks)], axis=0)
               for p in range(n_pairs)]

    o = jnp.concatenate(o_state, axis=1) + o_intra
    ms = _head_mean(o * o, eb)
    o = o * lax.rsqrt(ms + NORM_EPS)
    o_ref[...] = (o * ng_ref[...] * jax.nn.silu(g_ref[...])).astype(BF16)


def _hgrn(proj, lb_logits, norm_g, eb, layer, ts):
    b_, s_, _ = proj.shape
    gw = GROUP_WIDTH
    c = HGRN_CHUNK
    depth = lb_logits.shape[0]
    row = jnp.arange(ts)
    same_chunk = (row[:, None] // c) == (row[None, :] // c)
    tri = (same_chunk & (row[:, None] >= row[None, :])).astype(BF16)
    sel = same_chunk.astype(BF16)
    pos = jnp.arange(c)
    bias = jnp.where(pos[None, :, None] >= pos[:, None, None], 0.0, -1e30)
    bias = jnp.broadcast_to(bias, (c, c, gw)).astype(F32)
    col = lambda j: pl.BlockSpec((None, ts, gw), lambda b, s: (b, s, j))
    return pl.pallas_call(
        functools.partial(_hgrn_kernel, layer),
        grid=(b_, s_ // ts),
        in_specs=[col(D_Q), col(D_F), col(D_V), col(D_G), _const_spec((depth, gw)), _const_spec((1, gw)),
                  _const_spec((ts, ts)), _const_spec((ts, ts)), _const_spec((c, c, gw)), _const_spec((gw, gw))],
        out_specs=pl.BlockSpec((None, ts, gw), lambda b, s: (b, s, 0)),
        out_shape=jax.ShapeDtypeStruct((b_, s_, gw), BF16),
        scratch_shapes=[pltpu.VMEM((N_GROUP_HEADS // 2, 2 * HEAD_DIM, 2 * HEAD_DIM), F32),
                        pltpu.VMEM((ts, gw), F32), pltpu.VMEM((ts, gw), F32)],
        compiler_params=_params("parallel", "arbitrary"),
        name="hgrn2",
    )(proj, proj, proj, proj, lb_logits, norm_g.reshape(1, gw), tri, sel, bias, eb)


def kernel(x, ln_in_g, ln_in_b, w_in, conv_w, conv_b, rg_wa, rg_ba, rg_wx, rg_bx, rg_lambda, ret_norm_g,
           hgrn_lb_logits, hgrn_norm_g, w_out, ln1_g, ln1_b, w_up, w_down, ln2_g, ln2_b):
    b_, s_, d = x.shape
    depth = w_in.shape[0]
    t = b_ * s_
    alpha = (2 * depth) ** 0.25
    tm = min(512, t)
    scan_rows = min(256, s_)
    rglru_rows = min(512, s_)

    head = jnp.arange(GROUP_WIDTH) // HEAD_DIM
    eb = (head[:, None] == head[None, :]).astype(BF16)

    h, hb = _ln_in(x.reshape(t, d), ln_in_g, ln_in_b, tm)
    gw = GROUP_WIDTH
    columns = lambda w, slices: jnp.concatenate([w[:, s * gw:(s + 1) * gw] for s in slices], axis=1).astype(BF16)
    for l in range(depth):
        pf, pb = _proj(hb, columns(w_in[l], F32_SLICES), columns(w_in[l], BF16_SLICES), tm)
        pf = pf.reshape(b_, s_, -1)
        pb = pb.reshape(b_, s_, -1)
        y_a = _rglru(pf, conv_w[l], conv_b[l], rg_wa[l], rg_ba[l], rg_wx[l], rg_bx[l], rg_lambda[l], rglru_rows)
        y_b = _retention(pf, pb, ret_norm_g[l], eb, min(512, s_))
        y_c = _stickbreak(pb, min(256, s_))
        y_d = _hgrn(pf, hgrn_lb_logits, hgrn_norm_g[l], eb, l, scan_rows)
        ys = [y.reshape(t, gw) for y in (y_a, y_b, y_c, y_d)]
        h, hb = _block_tail(ys, w_out[l].astype(BF16), h, ln1_g[l], ln1_b[l], w_up[l].astype(BF16),
                            w_down[l].astype(BF16), ln2_g[l], ln2_b[l], alpha, tm, 1024)
    return h.reshape(b_, s_, d).astype(x.dtype)
```

```python
import functools
import math

import jax
import jax.numpy as jnp
from jax import lax
from jax.experimental import pallas as pl
from jax.experimental.pallas import tpu as pltpu

F32 = jnp.float32
BF16 = jnp.bfloat16

HEAD_DIM = 64
N_GROUP_HEADS = 4
GROUP_WIDTH = HEAD_DIM * N_GROUP_HEADS
F32_SLICES = (0, 1, 2, 3, 5, 9, 10, 11, 12)
BF16_SLICES = (4, 6, 7, 8)
A_X, A_G, R_Q, R_K, R_G, D_Q, D_F, D_V, D_G = range(9)
R_V, S_Q, S_K, S_V = range(4)
CONV_WIDTH = 4
RG_LRU_C = 8.0
RET_CHUNK = 128
SB_BLOCK = 128
SB_FIRST_DEPTH = 3
HGRN_CHUNK = 16
ROPE_BASE = 10000.0
LN_EPS = 1e-5
NORM_EPS = 1e-6
GATE_FLOOR = 1e-30
EXP2_F32_ZERO_BELOW = -150.1
EXP2_CLAMP = 126.0
LOG2_E = math.log2(math.e)

VMEM_LIMIT_BYTES = 56 * 1024 * 1024
SUBLANES = 8


def _params(*semantics):
    return pltpu.CompilerParams(dimension_semantics=semantics, vmem_limit_bytes=VMEM_LIMIT_BYTES)


def _const_spec(shape):
    zeros = (0,) * len(shape)
    return pl.BlockSpec(shape, lambda *_: zeros)


def _layer_norm_rows(x, g, b):
    mu = jnp.mean(x, axis=-1, keepdims=True)
    xc = x - mu
    var = jnp.mean(xc * xc, axis=-1, keepdims=True)
    return xc * lax.rsqrt(var + LN_EPS) * g + b


def _split2(x):
    hi = x.astype(BF16)
    lo = (x - hi.astype(F32)).astype(BF16)
    return hi, lo


def _head_mean(x, eb):
    hi, lo = _split2(x)
    s = jnp.dot(hi, eb, preferred_element_type=F32) + jnp.dot(lo, eb, preferred_element_type=F32)
    return s * (1.0 / HEAD_DIM)


def _proj_kernel(h_ref, wf_ref, wb_ref, pf_ref, pb_ref):
    h = h_ref[...]
    pf_ref[...] = jnp.dot(h, wf_ref[...], preferred_element_type=F32)
    pb_ref[...] = jnp.dot(h, wb_ref[...], preferred_element_type=F32).astype(BF16)


def _ln_proj_kernel(x_ref, g_ref, b_ref, wf_ref, wb_ref, h_ref, pf_ref, pb_ref):
    h = _layer_norm_rows(x_ref[...], g_ref[...], b_ref[...])
    h_ref[...] = h
    hb = h.astype(BF16)
    pf_ref[...] = jnp.dot(hb, wf_ref[...], preferred_element_type=F32)
    pb_ref[...] = jnp.dot(hb, wb_ref[...], preferred_element_type=F32).astype(BF16)


def _proj(h, wf, wb, tm, ln=None):
    t, d = h.shape
    nf, nb = wf.shape[1], wb.shape[1]
    resident = lambda shape: pl.BlockSpec(shape, lambda i: (0, 0), pipeline_mode=pl.Buffered(1))
    row = lambda n: pl.BlockSpec((tm, n), lambda i: (i, 0))
    proj_shapes = [jax.ShapeDtypeStruct((t, nf), F32), jax.ShapeDtypeStruct((t, nb), BF16)]
    if ln is None:
        return pl.pallas_call(
            _proj_kernel,
            grid=(t // tm,),
            in_specs=[row(d), resident((d, nf)), resident((d, nb))],
            out_specs=[row(nf), row(nb)],
            out_shape=proj_shapes,
            compiler_params=_params("parallel"),
            name="in_proj",
        )(h, wf, wb)
    g, b = ln
    return pl.pallas_call(
        _ln_proj_kernel,
        grid=(t // tm,),
        in_specs=[row(d), _const_spec((1, d)), _const_spec((1, d)), resident((d, nf)), resident((d, nb))],
        out_specs=[row(d), row(nf), row(nb)],
        out_shape=[jax.ShapeDtypeStruct((t, d), F32)] + proj_shapes,
        compiler_params=_params("parallel"),
        name="ln_in_proj",
    )(h, g.reshape(1, d), b.reshape(1, d), wf, wb)


def _block_tail_kernel(alpha, ff_chunk, ya_ref, yb_ref, yc_ref, yd_ref, wo_ref, h_ref, g1_ref, b1_ref,
                       wu_ref, wd_ref, g2_ref, b2_ref, o_ref, ob_ref):
    gw = GROUP_WIDTH
    mix = jnp.dot(ya_ref[...], wo_ref[0 * gw:1 * gw, :], preferred_element_type=F32)
    mix += jnp.dot(yb_ref[...], wo_ref[1 * gw:2 * gw, :], preferred_element_type=F32)
    mix += jnp.dot(yc_ref[...], wo_ref[2 * gw:3 * gw, :], preferred_element_type=F32)
    mix += jnp.dot(yd_ref[...], wo_ref[3 * gw:4 * gw, :], preferred_element_type=F32)
    h1 = _layer_norm_rows(alpha * h_ref[...] + mix, g1_ref[...], b1_ref[...])
    hb = h1.astype(BF16)
    d_ff = wu_ref.shape[1]
    acc = jnp.zeros(h_ref.shape, F32)
    for c in range(d_ff // ff_chunk):
        cols = slice(c * ff_chunk, (c + 1) * ff_chunk)
        u = jnp.dot(hb, wu_ref[:, cols], preferred_element_type=F32)
        u = jnp.square(jnp.maximum(u, 0.0)).astype(BF16)
        acc += jnp.dot(u, wd_ref[cols, :], preferred_element_type=F32)
    h2 = _layer_norm_rows(alpha * h1 + acc, g2_ref[...], b2_ref[...])
    o_ref[...] = h2
    ob_ref[...] = h2.astype(BF16)


def _block_tail(ys, wo, h, g1, b1, wu, wd, g2, b2, alpha, tm, ff_chunk):
    t, d = h.shape
    d_ff = wu.shape[1]
    gw = GROUP_WIDTH
    y_spec = pl.BlockSpec((tm, gw), lambda i: (i, 0))
    row_spec = pl.BlockSpec((tm, d), lambda i: (i, 0))
    resident = lambda shape: pl.BlockSpec(shape, lambda i: (0, 0), pipeline_mode=pl.Buffered(1))
    vec = _const_spec((1, d))
    return pl.pallas_call(
        functools.partial(_block_tail_kernel, alpha, ff_chunk),
        grid=(t // tm,),
        in_specs=[y_spec, y_spec, y_spec, y_spec, resident((d, d)), row_spec, vec, vec,
                  resident((d, d_ff)), resident((d_ff, d)), vec, vec],
        out_specs=[row_spec, row_spec],
        out_shape=[jax.ShapeDtypeStruct((t, d), F32), jax.ShapeDtypeStruct((t, d), BF16)],
        compiler_params=_params("parallel"),
        name="out_proj_mlp",
    )(*ys, wo, h, g1.reshape(1, d), b1.reshape(1, d), wu, wd, g2.reshape(1, d), b2.reshape(1, d))


def _rglru_kernel(xa_ref, ga_ref, cw_ref, cb_ref, wg_ref, bg_ref, lam_ref, o_ref, ext_ref, hprev_ref):
    ts, gw = xa_ref.shape
    si = pl.program_id(1)

    pad = SUBLANES

    @pl.when(si == 0)
    def _():
        ext_ref[0:pad, :] = jnp.zeros((pad, gw), F32)
        hprev_ref[...] = jnp.zeros((SUBLANES, gw), F32)

    @pl.when(si > 0)
    def _():
        ext_ref[0:pad, :] = ext_ref[ts:ts + pad, :]

    x = xa_ref[...]
    ext_ref[pad:ts + pad, :] = x
    cw = cw_ref[...]
    last = CONV_WIDTH - 1
    xc = x * cw[last:last + 1, :] + cb_ref[...]
    for k in range(1, CONV_WIDTH):
        xc += ext_ref[pad - k:pad - k + ts, :] * cw[last - k:last - k + 1, :]

    gates = jnp.dot(xc.astype(BF16), wg_ref[...], preferred_element_type=F32) + bg_ref[...]
    r = jax.nn.sigmoid(gates[:, :gw])
    i = jax.nn.sigmoid(gates[:, gw:])
    lam = lam_ref[...]
    log_sig_lam = -(jnp.maximum(-lam, 0.0) + jnp.log1p(jnp.exp(-jnp.abs(lam))))
    log_a = RG_LRU_C * r * log_sig_lam
    a = jnp.exp(log_a)
    th = jnp.tanh(log_a)
    one_minus_a2 = -2.0 * th / (1.0 - th)
    u = jnp.sqrt(jnp.maximum(one_minus_a2, 0.0)) * (i * xc)

    sub = lax.broadcasted_iota(jnp.int32, (SUBLANES, gw), 0)
    h_in = hprev_ref[SUBLANES - 1:SUBLANES, :]
    tiles = []
    for t0 in range(0, ts, SUBLANES):
        a_t = a[t0:t0 + SUBLANES, :]
        u_t = u[t0:t0 + SUBLANES, :]
        k = 1
        while k < SUBLANES:
            live = sub >= k
            a_sh = jnp.where(live, pltpu.roll(a_t, k, 0), 1.0)
            u_sh = jnp.where(live, pltpu.roll(u_t, k, 0), 0.0)
            u_t = a_t * u_sh + u_t
            a_t = a_t * a_sh
            k *= 2
        h_t = u_t + a_t * h_in
        h_in = h_t[SUBLANES - 1:SUBLANES, :]
        tiles.append(h_t)
    h = jnp.concatenate(tiles, axis=0)
    hprev_ref[...] = tiles[-1]
    o_ref[...] = (jax.nn.gelu(ga_ref[...], approximate=True) * h).astype(BF16)


def _blockdiag_heads(w):
    h, di, dj = w.shape
    eye = jnp.eye(h, dtype=w.dtype)
    return (eye[:, None, :, None] * w[:, :, None, :]).reshape(h * di, h * dj)


def _rglru(proj, conv_w, conv_b, wa, ba, wx, bx, lam, ts):
    b_, s_, _ = proj.shape
    gw = GROUP_WIDTH
    wg = jnp.concatenate([_blockdiag_heads(wa), _blockdiag_heads(wx)], axis=1).astype(BF16)
    bg = jnp.concatenate([ba.reshape(1, gw), bx.reshape(1, gw)], axis=1)
    col = lambda j: pl.BlockSpec((None, ts, gw), lambda b, s: (b, s, j))
    return pl.pallas_call(
        _rglru_kernel,
        grid=(b_, s_ // ts),
        in_specs=[col(A_X), col(A_G), _const_spec((CONV_WIDTH, gw)), _const_spec((1, gw)),
                  _const_spec((gw, 2 * gw)), _const_spec((1, 2 * gw)), _const_spec((1, gw))],
        out_specs=pl.BlockSpec((None, ts, gw), lambda b, s: (b, s, 0)),
        out_shape=jax.ShapeDtypeStruct((b_, s_, gw), BF16),
        scratch_shapes=[pltpu.VMEM((ts + SUBLANES, gw), F32), pltpu.VMEM((SUBLANES, gw), F32)],
        compiler_params=_params("parallel", "arbitrary"),
        name="rglru",
    )(proj, proj, conv_w, conv_b.reshape(1, gw), wg, bg, lam.reshape(1, gw))


def _retention_kernel(q_ref, k_ref, v_ref, g_ref, cos_ref, sin_ref, qdec_ref, kdec_ref, cdec_ref,
                      dmask_ref, ng_ref, eb_ref, o_ref, state_ref):
    ts, gw = q_ref.shape
    c = RET_CHUNK
    pw = 2 * HEAD_DIM
    n_pairs = N_GROUP_HEADS // 2

    @pl.when(pl.program_id(1) == 0)
    def _():
        state_ref[...] = jnp.zeros((gw, gw), F32)

    lane = lax.broadcasted_iota(jnp.int32, (ts, gw), 1)
    first_half = (lane % HEAD_DIM) < (HEAD_DIM // 2)
    cos = cos_ref[...]
    sin = sin_ref[...]

    def rotary(t):
        partner = jnp.where(first_half, pltpu.roll(t, gw - HEAD_DIM // 2, 1), pltpu.roll(t, HEAD_DIM // 2, 1))
        return t * cos + partner * sin

    q = rotary(q_ref[...])
    k = rotary(k_ref[...]) * (HEAD_DIM ** -0.5)
    qb = q.astype(BF16)
    kb = k.astype(BF16)
    q_dec = (q * qdec_ref[...]).astype(BF16)
    k_dec = (k * kdec_ref[...]).astype(BF16)
    vb = v_ref[...]
    eb = eb_ref[...]
    same_head = eb > 0
    lane_kv = lax.broadcasted_iota(jnp.int32, (c, pw), 1)
    chunks = [slice(n * c, (n + 1) * c) for n in range(ts // c)]

    kv = [lax.dot_general(k_dec[r, :], vb[r, :], (((0,), (0,)), ((), ())), preferred_element_type=F32)
          for r in chunks]
    state = state_ref[...]
    entering = []
    for kv_n in kv:
        entering.append(state.astype(BF16))
        state = state * cdec_ref[...] + jnp.where(same_head, kv_n, 0.0)
    state_ref[...] = state

    outs = []
    for r, state_n in zip(chunks, entering):
        inter = jnp.dot(q_dec[r, :], state_n, preferred_element_type=F32)
        intra = []
        for p in range(n_pairs):
            lanes = slice(p * pw, (p + 1) * pw)
            kbd = _pair_blockdiag(kb[r, lanes], lane_kv)
            vbd = _pair_blockdiag(vb[r, lanes], lane_kv)
            scores = lax.dot_general(qb[r, lanes], kbd, (((1,), (1,)), ((), ())), preferred_element_type=F32)
            intra.append(jnp.dot((scores * dmask_ref[p]).astype(BF16), vbd, preferred_element_type=F32))
        outs.append(inter + jnp.concatenate(intra, axis=1))
    o = jnp.concatenate(outs, axis=0)

    mu = _head_mean(o, eb)
    oc = o - mu
    var = _head_mean(oc * oc, eb)
    o = oc * lax.rsqrt(var + NORM_EPS) * ng_ref[...]
    o_ref[...] = (jax.nn.silu(g_ref[...]) * o).astype(BF16)


def _retention_tables(s_, rows):
    inv_freq = ROPE_BASE ** (-jnp.arange(0, HEAD_DIM, 2, dtype=F32) / HEAD_DIM)
    ang = jnp.arange(s_, dtype=F32)[:, None] * inv_freq[None, :]
    cos, sin = jnp.cos(ang), jnp.sin(ang)
    cos_t = jnp.tile(jnp.concatenate([cos, cos], axis=-1), (1, N_GROUP_HEADS))
    sin_t = jnp.tile(jnp.concatenate([-sin, sin], axis=-1), (1, N_GROUP_HEADS))
    c_ = RET_CHUNK
    log_gamma = jnp.log1p(-jnp.exp2(-5.0 - jnp.arange(N_GROUP_HEADS, dtype=F32)))
    pos = jnp.arange(c_, dtype=F32)
    diff = pos[:, None] - pos[None, :]
    dmask = jnp.where(diff >= 0, jnp.exp(log_gamma[:, None, None] * jnp.maximum(diff, 0.0)), 0.0)
    lanes = lambda per_head: jnp.repeat(per_head, HEAD_DIM, axis=-1)
    kdec = lanes(jnp.exp(log_gamma[None, :] * (c_ - 1.0 - pos)[:, None]))
    qdec = lanes(jnp.exp(log_gamma[None, :] * (pos + 1.0)[:, None]))
    cdec = lanes(jnp.exp(log_gamma * c_)[None, :])
    dmask = dmask.reshape(N_GROUP_HEADS // 2, 2, c_, c_).transpose(0, 2, 1, 3).reshape(N_GROUP_HEADS // 2, c_, 2 * c_)
    return cos_t, sin_t, jnp.tile(qdec, (rows // c_, 1)), jnp.tile(kdec, (rows // c_, 1)), cdec, dmask


def _retention(proj, proj_b, norm_g, eb, ts):
    b_, s_, _ = proj.shape
    gw = GROUP_WIDTH
    c_ = RET_CHUNK
    cos_t, sin_t, qdec, kdec, cdec, dmask = _retention_tables(s_, ts)
    col = lambda j: pl.BlockSpec((None, ts, gw), lambda b, n: (b, n, j))
    pos_spec = pl.BlockSpec((ts, gw), lambda b, n: (n, 0))
    return pl.pallas_call(
        _retention_kernel,
        grid=(b_, s_ // ts),
        in_specs=[col(R_Q), col(R_K), col(R_V), col(R_G), pos_spec, pos_spec,
                  _const_spec((ts, gw)), _const_spec((ts, gw)), _const_spec((1, gw)),
                  _const_spec((N_GROUP_HEADS // 2, c_, 2 * c_)), _const_spec((1, gw)), _const_spec((gw, gw))],
        out_specs=pl.BlockSpec((None, ts, gw), lambda b, n: (b, n, 0)),
        out_shape=jax.ShapeDtypeStruct((b_, s_, gw), BF16),
        scratch_shapes=[pltpu.VMEM((gw, gw), F32)],
        compiler_params=_params("parallel", "arbitrary"),
        name="retention",
    )(proj, proj, proj_b, proj, cos_t, sin_t, qdec, kdec, cdec, dmask, norm_g.reshape(1, gw), eb)


def _pair_blockdiag(x, lane):
    zero = jnp.zeros_like(x)
    return jnp.concatenate([jnp.where(lane < HEAD_DIM, x, zero), jnp.where(lane >= HEAD_DIM, x, zero)], axis=0)


def _stickbreak_kernel(q_ref, k_ref, v_ref, m_ref, o_ref, kbd_ref, vbd_ref):
    tq = q_ref.shape[0]
    blk = SB_BLOCK
    n_sub = tq // blk
    pw = 2 * HEAD_DIM
    n_pairs = N_GROUP_HEADS // 2
    n_blocks = k_ref.shape[0] // blk
    i = pl.program_id(1)
    scale = HEAD_DIM ** -0.5

    @pl.when(i == 0)
    def _():
        lane_kv = lax.broadcasted_iota(jnp.int32, (blk, pw), 1)

        def build(j, _):
            rows = pl.ds(pl.multiple_of(j * blk, blk), blk)
            for p in range(n_pairs):
                dst = pl.ds(pl.multiple_of((j * n_pairs + p) * (2 * blk), 2 * blk), 2 * blk)
                kbd_ref[dst, :] = _pair_blockdiag(k_ref[rows, p * pw:(p + 1) * pw], lane_kv)
                vbd_ref[dst, :] = _pair_blockdiag(v_ref[rows, p * pw:(p + 1) * pw], lane_kv)
            return 0

        lax.fori_loop(0, n_blocks, build, 0)

    row = lax.broadcasted_iota(jnp.int32, (blk, 2 * blk), 0)
    key_off = lax.broadcasted_iota(jnp.int32, (blk, 2 * blk), 1) % blk
    strictly_before = key_off < row
    m = m_ref[...]
    qs = [[q_ref[u * blk:(u + 1) * blk, p * pw:(p + 1) * pw] * scale for p in range(n_pairs)]
          for u in range(n_sub)]

    def step(first_block, carry, depth, diagonal_first, never_negative):
        chains = [(u, p, d) for u in range(n_sub) for p in range(n_pairs) for d in range(depth)]
        z2, vbd, log2_w, tot = {}, {}, {}, {}
        for u, p, d in chains:
            j = jnp.maximum(first_block + u - d, 0)
            base = pl.multiple_of((j * n_pairs + p) * (2 * blk), 2 * blk)
            vbd[u, p, d] = vbd_ref[pl.ds(base, 2 * blk), :]
            z2[u, p, d] = lax.dot_general(qs[u][p], kbd_ref[pl.ds(base, 2 * blk), :], (((1,), (1,)), ((), ())),
                                          preferred_element_type=F32) * LOG2_E
        for c in chains:
            softplus2 = jnp.maximum(z2[c], jnp.log2(1.0 + jnp.exp2(jnp.minimum(z2[c], EXP2_CLAMP))))
            if diagonal_first and c[2] == 0:
                softplus2 = jnp.where(strictly_before, softplus2, 0.0)
            hi, lo = _split2(softplus2)
            log2_w[c] = z2[c] + jnp.dot(jnp.concatenate([hi, lo], axis=1), m, preferred_element_type=F32)
            tot[c] = (jnp.sum(softplus2[:, :blk], axis=1, keepdims=True),
                      jnp.sum(softplus2[:, blk:], axis=1, keepdims=True))
        out = {}
        for u in range(n_sub):
            for p in range(n_pairs):
                acc, run0, run1 = carry[u][p]
                for d in range(depth):
                    c = (u, p, d)
                    use0, use1 = run0, run1
                    if not never_negative(u, d):
                        finished = first_block + u - d < 0
                        use0 = jnp.where(finished, -jnp.inf, run0)
                        use1 = jnp.where(finished, -jnp.inf, run1)
                    w = jnp.concatenate([jnp.exp2(log2_w[c][:, :blk] + use0), jnp.exp2(log2_w[c][:, blk:] + use1)],
                                        axis=1)
                    if diagonal_first and d == 0:
                        w = jnp.where(strictly_before, w, 0.0)
                    acc = acc + jnp.dot(w.astype(BF16), vbd[c], preferred_element_type=F32)
                    run0 = run0 - tot[c][0]
                    run1 = run1 - tot[c][1]
                out[u, p] = (acc, run0, run1)
        return tuple(tuple(out[u, p] for p in range(n_pairs)) for u in range(n_sub))

    zero_col = jnp.zeros((blk, 1), F32)
    carry = tuple(tuple((jnp.zeros((blk, pw), F32), zero_col, zero_col) for _ in range(n_pairs))
                  for _ in range(n_sub))
    carry = step(n_sub * i, carry, SB_FIRST_DEPTH, True, lambda u, d: u >= d)

    def any_weight_left(c):
        top = functools.reduce(jnp.maximum, [r for sub in c for _, run0, run1 in sub for r in (run0, run1)])
        return (jnp.max(top) > EXP2_F32_ZERO_BELOW).astype(jnp.int32)

    def sweep(state):
        t, _, c = state
        c = step(n_sub * i - SB_FIRST_DEPTH - t, c, 1, False, lambda u, d: u == n_sub - 1)
        return t + 1, any_weight_left(c), c

    n_trips = n_sub * i + n_sub - SB_FIRST_DEPTH
    _, _, carry = lax.while_loop(lambda state: (state[0] < n_trips) & (state[1] > 0), sweep,
                                 (jnp.int32(0), any_weight_left(carry), carry))
    for u in range(n_sub):
        for p in range(n_pairs):
            o_ref[u * blk:(u + 1) * blk, p * pw:(p + 1) * pw] = carry[u][p][0].astype(BF16)


def _stickbreak(proj_b, tq):
    b_, s_, _ = proj_b.shape
    gw = GROUP_WIDTH
    blk = SB_BLOCK
    idx = jnp.arange(2 * blk)
    same_head = (idx[:, None] // blk) == (idx[None, :] // blk)
    m = -(same_head & (idx[:, None] >= idx[None, :])).astype(BF16)
    m = jnp.concatenate([m, m], axis=0)
    return pl.pallas_call(
        _stickbreak_kernel,
        grid=(b_, s_ // tq),
        in_specs=[pl.BlockSpec((None, tq, gw), lambda b, i: (b, i, S_Q)),
                  pl.BlockSpec((None, s_, gw), lambda b, i: (b, 0, S_K)),
                  pl.BlockSpec((None, s_, gw), lambda b, i: (b, 0, S_V)),
                  _const_spec((4 * blk, 2 * blk))],
        out_specs=pl.BlockSpec((None, tq, gw), lambda b, i: (b, i, 0)),
        out_shape=jax.ShapeDtypeStruct((b_, s_, gw), BF16),
        scratch_shapes=[pltpu.VMEM((2 * s_ * (N_GROUP_HEADS // 2), 2 * HEAD_DIM), BF16),
                        pltpu.VMEM((2 * s_ * (N_GROUP_HEADS // 2), 2 * HEAD_DIM), BF16)],
        compiler_params=_params("parallel", "arbitrary"),
        name="stickbreak",
    )(proj_b, proj_b, proj_b, m)


def _split3(x):
    p1 = x.astype(BF16)
    r1 = x - p1.astype(F32)
    p2 = r1.astype(BF16)
    p3 = (r1 - p2.astype(F32)).astype(BF16)
    return p1, p2, p3


def _dot3(a, parts):
    return sum(jnp.dot(a, p, preferred_element_type=F32) for p in parts)


def _hgrn_kernel(layer, q_ref, f_ref, v_ref, g_ref, lbl_ref, ng_ref, tri_ref, sel_ref, bias_ref, eb_ref, o_ref,
                 state_ref, b2_ref, kk_ref):
    ts, gw = q_ref.shape
    c = HGRN_CHUNK
    n_chunks = ts // c
    pw = 2 * HEAD_DIM
    n_pairs = N_GROUP_HEADS // 2

    @pl.when(pl.program_id(1) == 0)
    def _():
        state_ref[...] = jnp.zeros(state_ref.shape, F32)

    logits = lbl_ref[...]
    e = jnp.exp(logits - jnp.max(logits, axis=0, keepdims=True))
    lb_p = e / jnp.sum(e, axis=0, keepdims=True)
    lb = jnp.sum(lb_p[1:layer + 1, :], axis=0, keepdims=True) if layer > 0 else jnp.zeros((1, gw), F32)

    f_pre = f_ref[...]
    q = q_ref[...]
    f_gate = lb + (1.0 - lb) * jax.nn.sigmoid(f_pre)
    log_f = jnp.log(jnp.maximum(f_gate, GATE_FLOOR))
    kk = (1.0 - lb) * jax.nn.sigmoid(-f_pre)
    parts = _split3(log_f)
    b = _dot3(tri_ref[...], parts)
    b_tot = _dot3(sel_ref[...], parts)
    qe = (q * jnp.exp(b)).astype(BF16)
    kd = (kk * jnp.exp(b_tot - b)).astype(BF16)
    decay = jnp.exp(b_tot)
    b2_ref[...] = b * math.log2(math.e)
    kk_ref[...] = (b - jnp.log(kk)) * math.log2(math.e)

    eb = eb_ref[...]
    half = c // 2
    o_intra = []
    for n in range(n_chunks):
        r0 = n * c
        blocks = []
        for s in range(c):
            lo = 0 if s < half else half
            key_s = kk_ref[r0 + s:r0 + s + 1, :]
            exponent = (b2_ref[r0 + lo:r0 + c, :] - key_s) + bias_ref[s, lo:c, :]
            blocks.append(jnp.exp2(exponent) * q_ref[r0 + lo:r0 + c, :])
        g = jnp.dot(jnp.concatenate(blocks, axis=0).astype(BF16), eb, preferred_element_type=F32)
        top = jnp.zeros((half, gw), F32)
        bottom = jnp.zeros((half, gw), F32)
        for s in range(c):
            vs = v_ref[r0 + s:r0 + s + 1, :]
            if s < half:
                top += g[s * c:s * c + half, :] * vs
                bottom += g[s * c + half:(s + 1) * c, :] * vs
            else:
                start = half * c + (s - half) * half
                bottom += g[start:start + half, :] * vs
        o_intra.append(top)
        o_intra.append(bottom)
    o_intra = jnp.concatenate(o_intra, axis=0)

    vb = v_ref[...].astype(BF16)
    same_head = eb[0:pw, 0:pw] > 0
    rows = [slice(n * c, (n + 1) * c) for n in range(n_chunks)]
    lanes = [slice(p * pw, (p + 1) * pw) for p in range(n_pairs)]
    kv_t = [[lax.dot_general(vb[r, l], kd[r, l], (((0,), (0,)), ((), ())), preferred_element_type=F32)
             for r in rows] for l in lanes]
    states = []
    for p in range(n_pairs):
        state = state_ref[p]
        entering = []
        for n in range(n_chunks):
            entering.append(state.astype(BF16))
            state = state * decay[n * c:n * c + 1, lanes[p]] + jnp.where(same_head, kv_t[p][n], 0.0)
        state_ref[p] = state
        states.append(entering)
    o_state = [jnp.concatenate([lax.dot_general(qe[rows[n], lanes[p]], states[p][n], (((1,), (1,)), ((), ())),
                                                preferred_element_type=F32) for n in range(n_chunks)], axis=0)
               for p in range(n_pairs)]

    o = jnp.concatenate(o_state, axis=1) + o_intra
    ms = _head_mean(o * o, eb)
    o = o * lax.rsqrt(ms + NORM_EPS)
    o_ref[...] = (o * ng_ref[...] * jax.nn.silu(g_ref[...])).astype(BF16)


def _hgrn(proj, lb_logits, norm_g, eb, layer, ts):
    b_, s_, _ = proj.shape
    gw = GROUP_WIDTH
    c = HGRN_CHUNK
    depth = lb_logits.shape[0]
    row = jnp.arange(ts)
    same_chunk = (row[:, None] // c) == (row[None, :] // c)
    tri = (same_chunk & (row[:, None] >= row[None, :])).astype(BF16)
    sel = same_chunk.astype(BF16)
    pos = jnp.arange(c)
    bias = jnp.where(pos[None, :, None] >= pos[:, None, None], 0.0, -1e30)
    bias = jnp.broadcast_to(bias, (c, c, gw)).astype(F32)
    col = lambda j: pl.BlockSpec((None, ts, gw), lambda b, s: (b, s, j))
    return pl.pallas_call(
        functools.partial(_hgrn_kernel, layer),
        grid=(b_, s_ // ts),
        in_specs=[col(D_Q), col(D_F), col(D_V), col(D_G), _const_spec((depth, gw)), _const_spec((1, gw)),
                  _const_spec((ts, ts)), _const_spec((ts, ts)), _const_spec((c, c, gw)), _const_spec((gw, gw))],
        out_specs=pl.BlockSpec((None, ts, gw), lambda b, s: (b, s, 0)),
        out_shape=jax.ShapeDtypeStruct((b_, s_, gw), BF16),
        scratch_shapes=[pltpu.VMEM((N_GROUP_HEADS // 2, 2 * HEAD_DIM, 2 * HEAD_DIM), F32),
                        pltpu.VMEM((ts, gw), F32), pltpu.VMEM((ts, gw), F32)],
        compiler_params=_params("parallel", "arbitrary"),
        name="hgrn2",
    )(proj, proj, proj, proj, lb_logits, norm_g.reshape(1, gw), tri, sel, bias, eb)


def kernel(x, ln_in_g, ln_in_b, w_in, conv_w, conv_b, rg_wa, rg_ba, rg_wx, rg_bx, rg_lambda, ret_norm_g,
           hgrn_lb_logits, hgrn_norm_g, w_out, ln1_g, ln1_b, w_up, w_down, ln2_g, ln2_b):
    b_, s_, d = x.shape
    depth = w_in.shape[0]
    t = b_ * s_
    alpha = (2 * depth) ** 0.25
    tm = min(512, t)
    scan_rows = min(256, s_)
    rglru_rows = min(512, s_)

    head = jnp.arange(GROUP_WIDTH) // HEAD_DIM
    eb = (head[:, None] == head[None, :]).astype(BF16)

    h = hb = None
    gw = GROUP_WIDTH
    columns = lambda w, slices: jnp.concatenate([w[:, s * gw:(s + 1) * gw] for s in slices], axis=1).astype(BF16)
    for l in range(depth):
        wf, wb = columns(w_in[l], F32_SLICES), columns(w_in[l], BF16_SLICES)
        if l == 0:
            h, pf, pb = _proj(x.reshape(t, d), wf, wb, tm, ln=(ln_in_g, ln_in_b))
        else:
            pf, pb = _proj(hb, wf, wb, tm)
        pf = pf.reshape(b_, s_, -1)
        pb = pb.reshape(b_, s_, -1)
        y_a = _rglru(pf, conv_w[l], conv_b[l], rg_wa[l], rg_ba[l], rg_wx[l], rg_bx[l], rg_lambda[l], rglru_rows)
        y_b = _retention(pf, pb, ret_norm_g[l], eb, min(512, s_))
        y_c = _stickbreak(pb, min(256, s_))
        y_d = _hgrn(pf, hgrn_lb_logits, hgrn_norm_g[l], eb, l, scan_rows)
        ys = [y.reshape(t, gw) for y in (y_a, y_b, y_c, y_d)]
        h, hb = _block_tail(ys, w_out[l].astype(BF16), h, ln1_g[l], ln1_b[l], w_up[l].astype(BF16),
                            w_down[l].astype(BF16), ln2_g[l], ln2_b[l], alpha, tm, 1024)
    return h.reshape(b_, s_, d).astype(x.dtype)
```

```python
import functools
import math

import jax
import jax.numpy as jnp
import numpy as np
from jax import lax
from jax.experimental import pallas as pl
from jax.experimental.pallas import tpu as pltpu

F32 = jnp.float32
BF16 = jnp.bfloat16

HEAD_DIM = 64
N_GROUP_HEADS = 4
GROUP_WIDTH = HEAD_DIM * N_GROUP_HEADS
F32_SLICES = (0, 1, 2, 3, 5, 9, 10, 11, 12)
BF16_SLICES = (4, 6, 7, 8)
A_X, A_G, R_Q, R_K, R_G, D_Q, D_F, D_V, D_G = range(9)
R_V, S_Q, S_K, S_V = range(4)
CONV_WIDTH = 4
RG_LRU_C = 8.0
RET_CHUNK = 128
SB_BLOCK = 128
SB_FIRST_DEPTH = 3
HGRN_CHUNK = 16
ROPE_BASE = 10000.0
LN_EPS = 1e-5
NORM_EPS = 1e-6
GATE_FLOOR = 1e-30
EXP2_F32_ZERO_BELOW = -150.1
EXP2_CLAMP = 126.0
LOG2_E = math.log2(math.e)

VMEM_LIMIT_BYTES = 56 * 1024 * 1024
SUBLANES = 8
DENSE_ROW_PARTS = 2


def _params(*semantics):
    return pltpu.CompilerParams(dimension_semantics=semantics, vmem_limit_bytes=VMEM_LIMIT_BYTES)


def _const_spec(shape):
    zeros = (0,) * len(shape)
    return pl.BlockSpec(shape, lambda *_: zeros)


def _layer_norm_rows(x, g, b):
    mu = jnp.mean(x, axis=-1, keepdims=True)
    xc = x - mu
    var = jnp.mean(xc * xc, axis=-1, keepdims=True)
    return xc * lax.rsqrt(var + LN_EPS) * g + b


def _split2(x):
    hi = x.astype(BF16)
    lo = (x - hi.astype(F32)).astype(BF16)
    return hi, lo


def _head_mean(x, eb):
    hi, lo = _split2(x)
    s = jnp.dot(hi, eb, preferred_element_type=F32) + jnp.dot(lo, eb, preferred_element_type=F32)
    return s * (1.0 / HEAD_DIM)


def _proj_kernel(h_ref, wf_ref, wb_ref, pf_ref, pb_ref):
    h = h_ref[...]
    pf_ref[...] = jnp.dot(h, wf_ref[...], preferred_element_type=F32)
    pb_ref[...] = jnp.dot(h, wb_ref[...], preferred_element_type=F32).astype(BF16)


def _ln_proj_kernel(x_ref, g_ref, b_ref, wf_ref, wb_ref, h_ref, pf_ref, pb_ref):
    tm = x_ref.shape[0]
    for r in range(DENSE_ROW_PARTS):
        rows = slice(r * (tm // DENSE_ROW_PARTS), (r + 1) * (tm // DENSE_ROW_PARTS))
        h = _layer_norm_rows(x_ref[rows, :], g_ref[...], b_ref[...])
        h_ref[rows, :] = h
        hb = h.astype(BF16)
        pf_ref[rows, :] = jnp.dot(hb, wf_ref[...], preferred_element_type=F32)
        pb_ref[rows, :] = jnp.dot(hb, wb_ref[...], preferred_element_type=F32).astype(BF16)


def _proj(h, wf, wb, tm, ln=None):
    t, d = h.shape
    nf, nb = wf.shape[1], wb.shape[1]
    resident = lambda shape: pl.BlockSpec(shape, lambda i: (0, 0), pipeline_mode=pl.Buffered(1))
    row = lambda n: pl.BlockSpec((tm, n), lambda i: (i, 0))
    proj_shapes = [jax.ShapeDtypeStruct((t, nf), F32), jax.ShapeDtypeStruct((t, nb), BF16)]
    if ln is None:
        return pl.pallas_call(
            _proj_kernel,
            grid=(t // tm,),
            in_specs=[row(d), resident((d, nf)), resident((d, nb))],
            out_specs=[row(nf), row(nb)],
            out_shape=proj_shapes,
            compiler_params=_params("parallel"),
            name="in_proj",
        )(h, wf, wb)
    g, b = ln
    return pl.pallas_call(
        _ln_proj_kernel,
        grid=(t // tm,),
        in_specs=[row(d), _const_spec((1, d)), _const_spec((1, d)), resident((d, nf)), resident((d, nb))],
        out_specs=[row(d), row(nf), row(nb)],
        out_shape=[jax.ShapeDtypeStruct((t, d), F32)] + proj_shapes,
        compiler_params=_params("parallel"),
        name="ln_in_proj",
    )(h, g.reshape(1, d), b.reshape(1, d), wf, wb)


def _block_tail_kernel(alpha, ff_chunk, ya_ref, yb_ref, yc_ref, yd_ref, wo_ref, h_ref, g1_ref, b1_ref,
                       wu_ref, wd_ref, g2_ref, b2_ref, o_ref, ob_ref):
    gw = GROUP_WIDTH
    tm = h_ref.shape[0]
    parts = [slice(r * (tm // DENSE_ROW_PARTS), (r + 1) * (tm // DENSE_ROW_PARTS)) for r in range(DENSE_ROW_PARTS)]
    y_refs = (ya_ref, yb_ref, yc_ref, yd_ref)
    mix = [sum(jnp.dot(y_ref[rows, :], wo_ref[s * gw:(s + 1) * gw, :], preferred_element_type=F32)
               for s, y_ref in enumerate(y_refs)) for rows in parts]
    h1 = [_layer_norm_rows(alpha * h_ref[rows, :] + mix_r, g1_ref[...], b1_ref[...]) for rows, mix_r in zip(parts, mix)]
    hb = [h.astype(BF16) for h in h1]
    d_ff = wu_ref.shape[1]
    acc = [jnp.zeros(h.shape, F32) for h in h1]
    for c in range(d_ff // ff_chunk):
        cols = slice(c * ff_chunk, (c + 1) * ff_chunk)
        for r in range(DENSE_ROW_PARTS):
            u = jnp.dot(hb[r], wu_ref[:, cols], preferred_element_type=F32)
            u = jnp.square(jnp.maximum(u, 0.0)).astype(BF16)
            acc[r] += jnp.dot(u, wd_ref[cols, :], preferred_element_type=F32)
    for rows, h1_r, acc_r in zip(parts, h1, acc):
        h2 = _layer_norm_rows(alpha * h1_r + acc_r, g2_ref[...], b2_ref[...])
        o_ref[rows, :] = h2
        ob_ref[rows, :] = h2.astype(BF16)


def _block_tail(ys, wo, h, g1, b1, wu, wd, g2, b2, alpha, tm, ff_chunk):
    t, d = h.shape
    d_ff = wu.shape[1]
    gw = GROUP_WIDTH
    y_spec = pl.BlockSpec((tm, gw), lambda i: (i, 0))
    row_spec = pl.BlockSpec((tm, d), lambda i: (i, 0))
    resident = lambda shape: pl.BlockSpec(shape, lambda i: (0, 0), pipeline_mode=pl.Buffered(1))
    vec = _const_spec((1, d))
    return pl.pallas_call(
        functools.partial(_block_tail_kernel, alpha, ff_chunk),
        grid=(t // tm,),
        in_specs=[y_spec, y_spec, y_spec, y_spec, resident((d, d)), row_spec, vec, vec,
                  resident((d, d_ff)), resident((d_ff, d)), vec, vec],
        out_specs=[row_spec, row_spec],
        out_shape=[jax.ShapeDtypeStruct((t, d), F32), jax.ShapeDtypeStruct((t, d), BF16)],
        compiler_params=_params("parallel"),
        name="out_proj_mlp",
    )(*ys, wo, h, g1.reshape(1, d), b1.reshape(1, d), wu, wd, g2.reshape(1, d), b2.reshape(1, d))


def _rglru_kernel(xa_ref, ga_ref, cw_ref, cb_ref, wg_ref, bg_ref, lam_ref, o_ref, ext_ref, hprev_ref):
    ts, gw = xa_ref.shape
    si = pl.program_id(1)

    pad = SUBLANES

    @pl.when(si == 0)
    def _():
        ext_ref[0:pad, :] = jnp.zeros((pad, gw), F32)
        hprev_ref[...] = jnp.zeros((SUBLANES, gw), F32)

    @pl.when(si > 0)
    def _():
        ext_ref[0:pad, :] = ext_ref[ts:ts + pad, :]

    x = xa_ref[...]
    ext_ref[pad:ts + pad, :] = x
    cw = cw_ref[...]
    last = CONV_WIDTH - 1
    xc = x * cw[last:last + 1, :] + cb_ref[...]
    for k in range(1, CONV_WIDTH):
        xc += ext_ref[pad - k:pad - k + ts, :] * cw[last - k:last - k + 1, :]

    gates = jnp.dot(xc.astype(BF16), wg_ref[...], preferred_element_type=F32) + bg_ref[...]
    r = jax.nn.sigmoid(gates[:, :gw])
    i = jax.nn.sigmoid(gates[:, gw:])
    lam = lam_ref[...]
    log_sig_lam = -(jnp.maximum(-lam, 0.0) + jnp.log1p(jnp.exp(-jnp.abs(lam))))
    log_a = RG_LRU_C * r * log_sig_lam
    a = jnp.exp(log_a)
    th = jnp.tanh(log_a)
    one_minus_a2 = -2.0 * th / (1.0 - th)
    u = jnp.sqrt(jnp.maximum(one_minus_a2, 0.0)) * (i * xc)

    sub = lax.broadcasted_iota(jnp.int32, (SUBLANES, gw), 0)
    h_in = hprev_ref[SUBLANES - 1:SUBLANES, :]
    tiles = []
    for t0 in range(0, ts, SUBLANES):
        a_t = a[t0:t0 + SUBLANES, :]
        u_t = u[t0:t0 + SUBLANES, :]
        k = 1
        while k < SUBLANES:
            live = sub >= k
            a_sh = jnp.where(live, pltpu.roll(a_t, k, 0), 1.0)
            u_sh = jnp.where(live, pltpu.roll(u_t, k, 0), 0.0)
            u_t = a_t * u_sh + u_t
            a_t = a_t * a_sh
            k *= 2
        h_t = u_t + a_t * h_in
        h_in = h_t[SUBLANES - 1:SUBLANES, :]
        tiles.append(h_t)
    h = jnp.concatenate(tiles, axis=0)
    hprev_ref[...] = tiles[-1]
    o_ref[...] = (jax.nn.gelu(ga_ref[...], approximate=True) * h).astype(BF16)


def _blockdiag_heads(w):
    h, di, dj = w.shape
    eye = jnp.eye(h, dtype=w.dtype)
    return (eye[:, None, :, None] * w[:, :, None, :]).reshape(h * di, h * dj)


def _rglru(proj, conv_w, conv_b, wa, ba, wx, bx, lam, ts):
    b_, s_, _ = proj.shape
    gw = GROUP_WIDTH
    wg = jnp.concatenate([_blockdiag_heads(wa), _blockdiag_heads(wx)], axis=1).astype(BF16)
    bg = jnp.concatenate([ba.reshape(1, gw), bx.reshape(1, gw)], axis=1)
    col = lambda j: pl.BlockSpec((None, ts, gw), lambda b, s: (b, s, j))
    return pl.pallas_call(
        _rglru_kernel,
        grid=(b_, s_ // ts),
        in_specs=[col(A_X), col(A_G), _const_spec((CONV_WIDTH, gw)), _const_spec((1, gw)),
                  _const_spec((gw, 2 * gw)), _const_spec((1, 2 * gw)), _const_spec((1, gw))],
        out_specs=pl.BlockSpec((None, ts, gw), lambda b, s: (b, s, 0)),
        out_shape=jax.ShapeDtypeStruct((b_, s_, gw), BF16),
        scratch_shapes=[pltpu.VMEM((ts + SUBLANES, gw), F32), pltpu.VMEM((SUBLANES, gw), F32)],
        compiler_params=_params("parallel", "arbitrary"),
        name="rglru",
    )(proj, proj, conv_w, conv_b.reshape(1, gw), wg, bg, lam.reshape(1, gw))


def _retention_kernel(q_ref, k_ref, v_ref, g_ref, cos_ref, sin_ref, qdec_ref, kdec_ref, cdec_ref,
                      dmask_ref, ng_ref, eb_ref, o_ref, state_ref):
    ts, gw = q_ref.shape
    c = RET_CHUNK
    pw = 2 * HEAD_DIM
    n_pairs = N_GROUP_HEADS // 2

    @pl.when(pl.program_id(1) == 0)
    def _():
        state_ref[...] = jnp.zeros((gw, gw), F32)

    lane = lax.broadcasted_iota(jnp.int32, (ts, gw), 1)
    first_half = (lane % HEAD_DIM) < (HEAD_DIM // 2)
    cos = cos_ref[...]
    sin = sin_ref[...]

    def rotary(t):
        partner = jnp.where(first_half, pltpu.roll(t, gw - HEAD_DIM // 2, 1), pltpu.roll(t, HEAD_DIM // 2, 1))
        return t * cos + partner * sin

    q = rotary(q_ref[...])
    k = rotary(k_ref[...]) * (HEAD_DIM ** -0.5)
    qb = q.astype(BF16)
    kb = k.astype(BF16)
    q_dec = (q * qdec_ref[...]).astype(BF16)
    k_dec = (k * kdec_ref[...]).astype(BF16)
    vb = v_ref[...]
    eb = eb_ref[...]
    same_head = eb > 0
    lane_kv = lax.broadcasted_iota(jnp.int32, (c, pw), 1)
    chunks = [slice(n * c, (n + 1) * c) for n in range(ts // c)]

    kv = [lax.dot_general(k_dec[r, :], vb[r, :], (((0,), (0,)), ((), ())), preferred_element_type=F32)
          for r in chunks]
    state = state_ref[...]
    entering = []
    for kv_n in kv:
        entering.append(state.astype(BF16))
        state = state * cdec_ref[...] + jnp.where(same_head, kv_n, 0.0)
    state_ref[...] = state

    outs = []
    for r, state_n in zip(chunks, entering):
        inter = jnp.dot(q_dec[r, :], state_n, preferred_element_type=F32)
        intra = []
        for p in range(n_pairs):
            lanes = slice(p * pw, (p + 1) * pw)
            kbd = _pair_blockdiag(kb[r, lanes], lane_kv)
            vbd = _pair_blockdiag(vb[r, lanes], lane_kv)
            scores = lax.dot_general(qb[r, lanes], kbd, (((1,), (1,)), ((), ())), preferred_element_type=F32)
            intra.append(jnp.dot((scores * dmask_ref[p]).astype(BF16), vbd, preferred_element_type=F32))
        outs.append(inter + jnp.concatenate(intra, axis=1))
    o = jnp.concatenate(outs, axis=0)

    mu = _head_mean(o, eb)
    oc = o - mu
    var = _head_mean(oc * oc, eb)
    o = oc * lax.rsqrt(var + NORM_EPS) * ng_ref[...]
    o_ref[...] = (jax.nn.silu(g_ref[...]) * o).astype(BF16)


def _retention_tables(s_, rows):
    inv_freq = ROPE_BASE ** (-np.arange(0, HEAD_DIM, 2, dtype=np.float64) / HEAD_DIM)
    ang = np.arange(s_, dtype=np.float64)[:, None] * inv_freq[None, :]
    cos, sin = np.cos(ang), np.sin(ang)
    cos_t = np.tile(np.concatenate([cos, cos], axis=-1), (1, N_GROUP_HEADS))
    sin_t = np.tile(np.concatenate([-sin, sin], axis=-1), (1, N_GROUP_HEADS))
    c_ = RET_CHUNK
    log_gamma = np.log1p(-np.exp2(-5.0 - np.arange(N_GROUP_HEADS, dtype=np.float64)))
    pos = np.arange(c_, dtype=np.float64)
    diff = pos[:, None] - pos[None, :]
    dmask = np.where(diff >= 0, np.exp(log_gamma[:, None, None] * np.maximum(diff, 0.0)), 0.0)
    lanes = lambda per_head: np.repeat(per_head, HEAD_DIM, axis=-1)
    kdec = lanes(np.exp(log_gamma[None, :] * (c_ - 1.0 - pos)[:, None]))
    qdec = lanes(np.exp(log_gamma[None, :] * (pos + 1.0)[:, None]))
    cdec = lanes(np.exp(log_gamma * c_)[None, :])
    dmask = dmask.reshape(N_GROUP_HEADS // 2, 2, c_, c_).transpose(0, 2, 1, 3).reshape(N_GROUP_HEADS // 2, c_, 2 * c_)
    tables = (cos_t, sin_t, np.tile(qdec, (rows // c_, 1)), np.tile(kdec, (rows // c_, 1)), cdec, dmask)
    return tuple(jnp.asarray(t, F32) for t in tables)


def _retention(proj, proj_b, norm_g, eb, ts):
    b_, s_, _ = proj.shape
    gw = GROUP_WIDTH
    c_ = RET_CHUNK
    cos_t, sin_t, qdec, kdec, cdec, dmask = _retention_tables(s_, ts)
    col = lambda j: pl.BlockSpec((None, ts, gw), lambda b, n: (b, n, j))
    pos_spec = pl.BlockSpec((ts, gw), lambda b, n: (n, 0))
    return pl.pallas_call(
        _retention_kernel,
        grid=(b_, s_ // ts),
        in_specs=[col(R_Q), col(R_K), col(R_V), col(R_G), pos_spec, pos_spec,
                  _const_spec((ts, gw)), _const_spec((ts, gw)), _const_spec((1, gw)),
                  _const_spec((N_GROUP_HEADS // 2, c_, 2 * c_)), _const_spec((1, gw)), _const_spec((gw, gw))],
        out_specs=pl.BlockSpec((None, ts, gw), lambda b, n: (b, n, 0)),
        out_shape=jax.ShapeDtypeStruct((b_, s_, gw), BF16),
        scratch_shapes=[pltpu.VMEM((gw, gw), F32)],
        compiler_params=_params("parallel", "arbitrary"),
        name="retention",
    )(proj, proj, proj_b, proj, cos_t, sin_t, qdec, kdec, cdec, dmask, norm_g.reshape(1, gw), eb)


def _pair_blockdiag(x, lane):
    zero = jnp.zeros_like(x)
    return jnp.concatenate([jnp.where(lane < HEAD_DIM, x, zero), jnp.where(lane >= HEAD_DIM, x, zero)], axis=0)


def _stickbreak_kernel(q_ref, k_ref, v_ref, m_ref, o_ref, kbd_ref, vbd_ref):
    tq = q_ref.shape[0]
    blk = SB_BLOCK
    n_sub = tq // blk
    pw = 2 * HEAD_DIM
    n_pairs = N_GROUP_HEADS // 2
    n_blocks = k_ref.shape[0] // blk
    i = pl.program_id(1)
    scale = HEAD_DIM ** -0.5

    @pl.when(i == 0)
    def _():
        lane_kv = lax.broadcasted_iota(jnp.int32, (blk, pw), 1)

        def build(j, _):
            rows = pl.ds(pl.multiple_of(j * blk, blk), blk)
            for p in range(n_pairs):
                dst = pl.ds(pl.multiple_of((j * n_pairs + p) * (2 * blk), 2 * blk), 2 * blk)
                kbd_ref[dst, :] = _pair_blockdiag(k_ref[rows, p * pw:(p + 1) * pw], lane_kv)
                vbd_ref[dst, :] = _pair_blockdiag(v_ref[rows, p * pw:(p + 1) * pw], lane_kv)
            return 0

        lax.fori_loop(0, n_blocks, build, 0)

    row = lax.broadcasted_iota(jnp.int32, (blk, 2 * blk), 0)
    key_off = lax.broadcasted_iota(jnp.int32, (blk, 2 * blk), 1) % blk
    strictly_before = key_off < row
    m = m_ref[...]
    qs = [[q_ref[u * blk:(u + 1) * blk, p * pw:(p + 1) * pw] * scale for p in range(n_pairs)]
          for u in range(n_sub)]

    def step(first_block, carry, depth, diagonal_first, never_negative):
        chains = [(u, p, d) for u in range(n_sub) for p in range(n_pairs) for d in range(depth)]
        z2, vbd, log2_w, tot = {}, {}, {}, {}
        for u, p, d in chains:
            j = jnp.maximum(first_block + u - d, 0)
            base = pl.multiple_of((j * n_pairs + p) * (2 * blk), 2 * blk)
            vbd[u, p, d] = vbd_ref[pl.ds(base, 2 * blk), :]
            z2[u, p, d] = lax.dot_general(qs[u][p], kbd_ref[pl.ds(base, 2 * blk), :], (((1,), (1,)), ((), ())),
                                          preferred_element_type=F32) * LOG2_E
        for c in chains:
            softplus2 = jnp.maximum(z2[c], jnp.log2(1.0 + jnp.exp2(jnp.minimum(z2[c], EXP2_CLAMP))))
            if diagonal_first and c[2] == 0:
                softplus2 = jnp.where(strictly_before, softplus2, 0.0)
            hi, lo = _split2(softplus2)
            log2_w[c] = z2[c] + jnp.dot(jnp.concatenate([hi, lo], axis=1), m, preferred_element_type=F32)
            tot[c] = (jnp.sum(softplus2[:, :blk], axis=1, keepdims=True),
                      jnp.sum(softplus2[:, blk:], axis=1, keepdims=True))
        out = {}
        for u in range(n_sub):
            for p in range(n_pairs):
                acc, run0, run1 = carry[u][p]
                for d in range(depth):
                    c = (u, p, d)
                    use0, use1 = run0, run1
                    if not never_negative(u, d):
                        finished = first_block + u - d < 0
                        use0 = jnp.where(finished, -jnp.inf, run0)
                        use1 = jnp.where(finished, -jnp.inf, run1)
                    w = jnp.concatenate([jnp.exp2(log2_w[c][:, :blk] + use0), jnp.exp2(log2_w[c][:, blk:] + use1)],
                                        axis=1)
                    if diagonal_first and d == 0:
                        w = jnp.where(strictly_before, w, 0.0)
                    acc = acc + jnp.dot(w.astype(BF16), vbd[c], preferred_element_type=F32)
                    run0 = run0 - tot[c][0]
                    run1 = run1 - tot[c][1]
                out[u, p] = (acc, run0, run1)
        return tuple(tuple(out[u, p] for p in range(n_pairs)) for u in range(n_sub))

    zero_col = jnp.zeros((blk, 1), F32)
    carry = tuple(tuple((jnp.zeros((blk, pw), F32), zero_col, zero_col) for _ in range(n_pairs))
                  for _ in range(n_sub))
    carry = step(n_sub * i, carry, SB_FIRST_DEPTH, True, lambda u, d: u >= d)

    def any_weight_left(c):
        top = functools.reduce(jnp.maximum, [r for sub in c for _, run0, run1 in sub for r in (run0, run1)])
        return (jnp.max(top) > EXP2_F32_ZERO_BELOW).astype(jnp.int32)

    def sweep(state):
        t, _, c = state
        c = step(n_sub * i - SB_FIRST_DEPTH - t, c, 1, False, lambda u, d: u == n_sub - 1)
        return t + 1, any_weight_left(c), c

    n_trips = n_sub * i + n_sub - SB_FIRST_DEPTH
    _, _, carry = lax.while_loop(lambda state: (state[0] < n_trips) & (state[1] > 0), sweep,
                                 (jnp.int32(0), any_weight_left(carry), carry))
    for u in range(n_sub):
        for p in range(n_pairs):
            o_ref[u * blk:(u + 1) * blk, p * pw:(p + 1) * pw] = carry[u][p][0].astype(BF16)


def _stickbreak(proj_b, tq):
    b_, s_, _ = proj_b.shape
    gw = GROUP_WIDTH
    blk = SB_BLOCK
    idx = np.arange(2 * blk)
    same_head = (idx[:, None] // blk) == (idx[None, :] // blk)
    m = -(same_head & (idx[:, None] >= idx[None, :])).astype(np.float32)
    m = jnp.asarray(np.concatenate([m, m], axis=0), BF16)
    return pl.pallas_call(
        _stickbreak_kernel,
        grid=(b_, s_ // tq),
        in_specs=[pl.BlockSpec((None, tq, gw), lambda b, i: (b, i, S_Q)),
                  pl.BlockSpec((None, s_, gw), lambda b, i: (b, 0, S_K)),
                  pl.BlockSpec((None, s_, gw), lambda b, i: (b, 0, S_V)),
                  _const_spec((4 * blk, 2 * blk))],
        out_specs=pl.BlockSpec((None, tq, gw), lambda b, i: (b, i, 0)),
        out_shape=jax.ShapeDtypeStruct((b_, s_, gw), BF16),
        scratch_shapes=[pltpu.VMEM((2 * s_ * (N_GROUP_HEADS // 2), 2 * HEAD_DIM), BF16),
                        pltpu.VMEM((2 * s_ * (N_GROUP_HEADS // 2), 2 * HEAD_DIM), BF16)],
        compiler_params=_params("parallel", "arbitrary"),
        name="stickbreak",
    )(proj_b, proj_b, proj_b, m)


def _split3(x):
    p1 = x.astype(BF16)
    r1 = x - p1.astype(F32)
    p2 = r1.astype(BF16)
    p3 = (r1 - p2.astype(F32)).astype(BF16)
    return p1, p2, p3


def _dot3(a, parts):
    return sum(jnp.dot(a, p, preferred_element_type=F32) for p in parts)


def _hgrn_kernel(layer, q_ref, f_ref, v_ref, g_ref, lbl_ref, ng_ref, tri_ref, sel_ref, bias_ref, eb_ref, o_ref,
                 state_ref, b2_ref, kk_ref):
    ts, gw = q_ref.shape
    c = HGRN_CHUNK
    n_chunks = ts // c
    pw = 2 * HEAD_DIM
    n_pairs = N_GROUP_HEADS // 2

    @pl.when(pl.program_id(1) == 0)
    def _():
        state_ref[...] = jnp.zeros(state_ref.shape, F32)

    logits = lbl_ref[...]
    e = jnp.exp(logits - jnp.max(logits, axis=0, keepdims=True))
    lb_p = e / jnp.sum(e, axis=0, keepdims=True)
    lb = jnp.sum(lb_p[1:layer + 1, :], axis=0, keepdims=True) if layer > 0 else jnp.zeros((1, gw), F32)

    f_pre = f_ref[...]
    q = q_ref[...]
    f_gate = lb + (1.0 - lb) * jax.nn.sigmoid(f_pre)
    log_f = jnp.log(jnp.maximum(f_gate, GATE_FLOOR))
    kk = (1.0 - lb) * jax.nn.sigmoid(-f_pre)
    parts = _split3(log_f)
    b = _dot3(tri_ref[...], parts)
    b_tot = _dot3(sel_ref[...], parts)
    qe = (q * jnp.exp(b)).astype(BF16)
    kd = (kk * jnp.exp(b_tot - b)).astype(BF16)
    decay = jnp.exp(b_tot)
    b2_ref[...] = b * math.log2(math.e)
    kk_ref[...] = (b - jnp.log(kk)) * math.log2(math.e)

    eb = eb_ref[...]
    vb = v_ref[...].astype(BF16)
    same_head = eb[0:pw, 0:pw] > 0
    rows = [slice(n * c, (n + 1) * c) for n in range(n_chunks)]
    lanes = [slice(p * pw, (p + 1) * pw) for p in range(n_pairs)]
    kv_t = [[lax.dot_general(vb[r, l], kd[r, l], (((0,), (0,)), ((), ())), preferred_element_type=F32)
             for r in rows] for l in lanes]
    states = []
    for p in range(n_pairs):
        state = state_ref[p]
        entering = []
        for n in range(n_chunks):
            entering.append(state.astype(BF16))
            state = state * decay[n * c:n * c + 1, lanes[p]] + jnp.where(same_head, kv_t[p][n], 0.0)
        state_ref[p] = state
        states.append(entering)
    o_state = [jnp.concatenate([lax.dot_general(qe[rows[n], lanes[p]], states[p][n], (((1,), (1,)), ((), ())),
                                                preferred_element_type=F32) for n in range(n_chunks)], axis=0)
               for p in range(n_pairs)]

    half = c // 2
    o_intra = []
    for n in range(n_chunks):
        r0 = n * c
        blocks = []
        for s in range(c):
            lo = 0 if s < half else half
            key_s = kk_ref[r0 + s:r0 + s + 1, :]
            exponent = (b2_ref[r0 + lo:r0 + c, :] - key_s) + bias_ref[s, lo:c, :]
            blocks.append(jnp.exp2(exponent) * q_ref[r0 + lo:r0 + c, :])
        g = jnp.dot(jnp.concatenate(blocks, axis=0).astype(BF16), eb, preferred_element_type=F32)
        top = jnp.zeros((half, gw), F32)
        bottom = jnp.zeros((half, gw), F32)
        for s in range(c):
            vs = v_ref[r0 + s:r0 + s + 1, :]
            if s < half:
                top += g[s * c:s * c + half, :] * vs
                bottom += g[s * c + half:(s + 1) * c, :] * vs
            else:
                start = half * c + (s - half) * half
                bottom += g[start:start + half, :] * vs
        o_intra.append(top)
        o_intra.append(bottom)
    o_intra = jnp.concatenate(o_intra, axis=0)

    o = jnp.concatenate(o_state, axis=1) + o_intra
    ms = _head_mean(o * o, eb)
    o = o * lax.rsqrt(ms + NORM_EPS)
    o_ref[...] = (o * ng_ref[...] * jax.nn.silu(g_ref[...])).astype(BF16)


def _hgrn(proj, lb_logits, norm_g, eb, layer, ts):
    b_, s_, _ = proj.shape
    gw = GROUP_WIDTH
    c = HGRN_CHUNK
    depth = lb_logits.shape[0]
    row = np.arange(ts)
    same_chunk = (row[:, None] // c) == (row[None, :] // c)
    tri = jnp.asarray(same_chunk & (row[:, None] >= row[None, :]), BF16)
    sel = jnp.asarray(same_chunk, BF16)
    pos = np.arange(c)
    bias = np.where(pos[None, :, None] >= pos[:, None, None], 0.0, -1e30)
    bias = jnp.asarray(np.broadcast_to(bias, (c, c, gw)), F32)
    col = lambda j: pl.BlockSpec((None, ts, gw), lambda b, s: (b, s, j))
    return pl.pallas_call(
        functools.partial(_hgrn_kernel, layer),
        grid=(b_, s_ // ts),
        in_specs=[col(D_Q), col(D_F), col(D_V), col(D_G), _const_spec((depth, gw)), _const_spec((1, gw)),
                  _const_spec((ts, ts)), _const_spec((ts, ts)), _const_spec((c, c, gw)), _const_spec((gw, gw))],
        out_specs=pl.BlockSpec((None, ts, gw), lambda b, s: (b, s, 0)),
        out_shape=jax.ShapeDtypeStruct((b_, s_, gw), BF16),
        scratch_shapes=[pltpu.VMEM((N_GROUP_HEADS // 2, 2 * HEAD_DIM, 2 * HEAD_DIM), F32),
                        pltpu.VMEM((ts, gw), F32), pltpu.VMEM((ts, gw), F32)],
        compiler_params=_params("parallel", "arbitrary"),
        name="hgrn2",
    )(proj, proj, proj, proj, lb_logits, norm_g.reshape(1, gw), tri, sel, bias, eb)


def kernel(x, ln_in_g, ln_in_b, w_in, conv_w, conv_b, rg_wa, rg_ba, rg_wx, rg_bx, rg_lambda, ret_norm_g,
           hgrn_lb_logits, hgrn_norm_g, w_out, ln1_g, ln1_b, w_up, w_down, ln2_g, ln2_b):
    b_, s_, d = x.shape
    depth = w_in.shape[0]
    t = b_ * s_
    alpha = (2 * depth) ** 0.25
    tm = min(512, t)
    scan_rows = min(256, s_)
    rglru_rows = min(1024, s_)

    head = np.arange(GROUP_WIDTH) // HEAD_DIM
    eb = jnp.asarray(head[:, None] == head[None, :], BF16)

    h = hb = None
    gw = GROUP_WIDTH
    columns = lambda w, slices: jnp.concatenate([w[:, s * gw:(s + 1) * gw] for s in slices], axis=1).astype(BF16)
    for l in range(depth):
        wf, wb = columns(w_in[l], F32_SLICES), columns(w_in[l], BF16_SLICES)
        if l == 0:
            h, pf, pb = _proj(x.reshape(t, d), wf, wb, tm, ln=(ln_in_g, ln_in_b))
        else:
            pf, pb = _proj(hb, wf, wb, tm)
        pf = pf.reshape(b_, s_, -1)
        pb = pb.reshape(b_, s_, -1)
        y_a = _rglru(pf, conv_w[l], conv_b[l], rg_wa[l], rg_ba[l], rg_wx[l], rg_bx[l], rg_lambda[l], rglru_rows)
        y_b = _retention(pf, pb, ret_norm_g[l], eb, min(1024, s_))
        y_c = _stickbreak(pb, min(256, s_))
        y_d = _hgrn(pf, hgrn_lb_logits, hgrn_norm_g[l], eb, l, scan_rows)
        ys = [y.reshape(t, gw) for y in (y_a, y_b, y_c, y_d)]
        h, hb = _block_tail(ys, w_out[l].astype(BF16), h, ln1_g[l], ln1_b[l], w_up[l].astype(BF16),
                            w_down[l].astype(BF16), ln2_g[l], ln2_b[l], alpha, tm, 1024)
    return h.reshape(b_, s_, d).astype(x.dtype)
```

```python
import functools
import math

import jax
import jax.numpy as jnp
import numpy as np
from jax import lax
from jax.experimental import pallas as pl
from jax.experimental.pallas import tpu as pltpu

F32 = jnp.float32
BF16 = jnp.bfloat16

HEAD_DIM = 64
N_GROUP_HEADS = 4
GROUP_WIDTH = HEAD_DIM * N_GROUP_HEADS
F32_SLICES = (0, 1, 2, 3, 5, 9, 10, 11, 12)
BF16_SLICES = (4, 6, 7, 8)
A_X, A_G, R_Q, R_K, R_G, D_Q, D_F, D_V, D_G = range(9)
R_V, S_Q, S_K, S_V = range(4)
CONV_WIDTH = 4
RG_LRU_C = 8.0
RET_CHUNK = 128
SB_BLOCK = 128
SB_FIRST_DEPTH = 3
HGRN_CHUNK = 16
HGRN_ROWS = 256
HGRN_STAGES = 4
ROPE_BASE = 10000.0
LN_EPS = 1e-5
NORM_EPS = 1e-6
GATE_FLOOR = 1e-30
EXP2_F32_ZERO_BELOW = -150.1
EXP2_CLAMP = 126.0
LOG2_E = math.log2(math.e)

VMEM_LIMIT_BYTES = 56 * 1024 * 1024
SUBLANES = 8
DENSE_ROW_PARTS = 2


def _params(*semantics):
    return pltpu.CompilerParams(dimension_semantics=semantics, vmem_limit_bytes=VMEM_LIMIT_BYTES)


def _const_spec(shape):
    zeros = (0,) * len(shape)
    return pl.BlockSpec(shape, lambda *_: zeros)


def _layer_norm_rows(x, g, b):
    mu = jnp.mean(x, axis=-1, keepdims=True)
    xc = x - mu
    var = jnp.mean(xc * xc, axis=-1, keepdims=True)
    return xc * lax.rsqrt(var + LN_EPS) * g + b


def _split2(x):
    hi = x.astype(BF16)
    lo = (x - hi.astype(F32)).astype(BF16)
    return hi, lo


def _head_mean(x, eb):
    hi, lo = _split2(x)
    s = jnp.dot(hi, eb, preferred_element_type=F32) + jnp.dot(lo, eb, preferred_element_type=F32)
    return s * (1.0 / HEAD_DIM)


def _proj_kernel(h_ref, wf_ref, wb_ref, pf_ref, pb_ref):
    h = h_ref[...]
    pf_ref[...] = jnp.dot(h, wf_ref[...], preferred_element_type=F32)
    pb_ref[...] = jnp.dot(h, wb_ref[...], preferred_element_type=F32).astype(BF16)


def _ln_proj_kernel(x_ref, g_ref, b_ref, wf_ref, wb_ref, h_ref, pf_ref, pb_ref):
    tm = x_ref.shape[0]
    for r in range(DENSE_ROW_PARTS):
        rows = slice(r * (tm // DENSE_ROW_PARTS), (r + 1) * (tm // DENSE_ROW_PARTS))
        h = _layer_norm_rows(x_ref[rows, :], g_ref[...], b_ref[...])
        h_ref[rows, :] = h
        hb = h.astype(BF16)
        pf_ref[rows, :] = jnp.dot(hb, wf_ref[...], preferred_element_type=F32)
        pb_ref[rows, :] = jnp.dot(hb, wb_ref[...], preferred_element_type=F32).astype(BF16)


def _proj(h, wf, wb, tm, ln=None):
    t, d = h.shape
    nf, nb = wf.shape[1], wb.shape[1]
    resident = lambda shape: pl.BlockSpec(shape, lambda i: (0, 0), pipeline_mode=pl.Buffered(1))
    row = lambda n: pl.BlockSpec((tm, n), lambda i: (i, 0))
    proj_shapes = [jax.ShapeDtypeStruct((t, nf), F32), jax.ShapeDtypeStruct((t, nb), BF16)]
    if ln is None:
        return pl.pallas_call(
            _proj_kernel,
            grid=(t // tm,),
            in_specs=[row(d), resident((d, nf)), resident((d, nb))],
            out_specs=[row(nf), row(nb)],
            out_shape=proj_shapes,
            compiler_params=_params("parallel"),
            name="in_proj",
        )(h, wf, wb)
    g, b = ln
    return pl.pallas_call(
        _ln_proj_kernel,
        grid=(t // tm,),
        in_specs=[row(d), _const_spec((1, d)), _const_spec((1, d)), resident((d, nf)), resident((d, nb))],
        out_specs=[row(d), row(nf), row(nb)],
        out_shape=[jax.ShapeDtypeStruct((t, d), F32)] + proj_shapes,
        compiler_params=_params("parallel"),
        name="ln_in_proj",
    )(h, g.reshape(1, d), b.reshape(1, d), wf, wb)


def _block_tail_kernel(alpha, ff_chunk, layer, tiles_per_seq, n_tiles,
                       xa_ref, ga_ref, cw_ref, cb_ref, wg_ref, bg_ref, lam_ref,
                       dq_ref, df_ref, dv_ref, dg_ref, lbl_ref, ng_ref, tri_ref, sel_ref, bias_ref, eb_ref,
                       yb_ref, yc_ref, wo_ref, h_ref, g1_ref, b1_ref, wu_ref, wd_ref, g2_ref, b2_ref,
                       o_ref, ob_ref, ext_ref, hprev_ref, state_ref, b2s_ref, keys_ref, ya_ref, yd_ref):
    i = pl.program_id(0)
    gw = GROUP_WIDTH
    tm = h_ref.shape[0]

    @pl.when(i == 0)
    def _():
        ya_ref[...] = jnp.zeros(ya_ref.shape, BF16)
        yd_ref[...] = jnp.zeros(yd_ref.shape, BF16)

    done = (i + 1) % 2
    ya_done = ya_ref[done]
    yd_done = yd_ref[done]
    first = jnp.minimum(i, n_tiles - 1) % tiles_per_seq == 0

    parts = [slice(r * (tm // DENSE_ROW_PARTS), (r + 1) * (tm // DENSE_ROW_PARTS)) for r in range(DENSE_ROW_PARTS)]
    mix = []
    for rows in parts:
        ys = (ya_done[rows, :], yb_ref[rows, :], yc_ref[rows, :], yd_done[rows, :])
        mix.append(sum(jnp.dot(y, wo_ref[s * gw:(s + 1) * gw, :], preferred_element_type=F32)
                       for s, y in enumerate(ys)))

    mixer_d = [_hgrn_stages(layer, sub * HGRN_ROWS, sub, first if sub == 0 else None, dq_ref, df_ref, dv_ref, dg_ref,
                            lbl_ref, ng_ref, tri_ref, sel_ref, bias_ref, eb_ref, state_ref, b2s_ref, keys_ref,
                            yd_ref.at[i % 2]) for sub in range(tm // HGRN_ROWS)]
    pending = [stages for stages in mixer_d for _ in range(HGRN_STAGES)]
    next(pending.pop(0))

    h1 = [_layer_norm_rows(alpha * h_ref[rows, :] + mix_r, g1_ref[...], b1_ref[...]) for rows, mix_r in zip(parts, mix)]
    hb = [h.astype(BF16) for h in h1]
    d_ff = wu_ref.shape[1]
    acc = [jnp.zeros(h.shape, F32) for h in h1]
    units = [(c, r) for c in range(d_ff // ff_chunk) for r in range(DENSE_ROW_PARTS)]
    mixer_a = None
    for n, (c, r) in enumerate(units):
        cols = slice(c * ff_chunk, (c + 1) * ff_chunk)
        u = jnp.dot(hb[r], wu_ref[:, cols], preferred_element_type=F32)
        u = jnp.square(jnp.maximum(u, 0.0)).astype(BF16)
        acc[r] += jnp.dot(u, wd_ref[cols, :], preferred_element_type=F32)
        if pending:
            next(pending.pop(0))
            if len(pending) == HGRN_STAGES:
                next(pending.pop(0))
        if not pending and mixer_a is None:
            mixer_a = _rglru_conv_gates(xa_ref[...], cw_ref, cb_ref, wg_ref, bg_ref, ext_ref, first)
    assert not pending and mixer_a is not None, "more mixer stages than dense matmul units"
    for rows, h1_r, acc_r in zip(parts, h1, acc):
        h2 = _layer_norm_rows(alpha * h1_r + acc_r, g2_ref[...], b2_ref[...])
        o_ref[rows, :] = h2
        ob_ref[rows, :] = h2.astype(BF16)

    ya_ref[i % 2] = _rglru_scan(*mixer_a, ga_ref[...], lam_ref, hprev_ref, first)


def _blockdiag_heads(w):
    h, di, dj = w.shape
    eye = jnp.eye(h, dtype=w.dtype)
    return (eye[:, None, :, None] * w[:, :, None, :]).reshape(h * di, h * dj)


def _block_tail(pf2, rg, hg, ys, wo, h, g1, b1, wu, wd, g2, b2, alpha, tm, ff_chunk, rows_per_seq, layer):
    t, d = h.shape
    d_ff = wu.shape[1]
    gw = GROUP_WIDTH
    n_tiles = t // tm
    conv_w, conv_b, wa, ba, wx, bx, lam = rg
    lb_logits, norm_g, eb = hg
    depth = lb_logits.shape[0]
    wg = jnp.concatenate([_blockdiag_heads(wa), _blockdiag_heads(wx)], axis=1).astype(BF16)
    bg = jnp.concatenate([ba.reshape(1, gw), bx.reshape(1, gw)], axis=1)
    c = HGRN_CHUNK
    row = np.arange(HGRN_ROWS)
    same_chunk = (row[:, None] // c) == (row[None, :] // c)
    tri = jnp.asarray(same_chunk & (row[:, None] >= row[None, :]), BF16)
    sel = jnp.asarray(same_chunk, BF16)
    pos = np.arange(c)
    bias = np.where(pos[None, :, None] >= pos[:, None, None], 0.0, -1e30)
    bias = jnp.asarray(np.broadcast_to(bias, (c, c, gw)), F32)
    ahead = lambda col: pl.BlockSpec((tm, gw), lambda i: (jnp.minimum(i, n_tiles - 1), col))
    behind = lambda n: pl.BlockSpec((tm, n), lambda i: (jnp.maximum(i - 1, 0), 0))
    resident = lambda shape: pl.BlockSpec(shape, lambda i: (0, 0), pipeline_mode=pl.Buffered(1))
    vec = _const_spec((1, d))
    n_sub = tm // HGRN_ROWS
    return pl.pallas_call(
        functools.partial(_block_tail_kernel, alpha, ff_chunk, layer, rows_per_seq // tm, n_tiles),
        grid=(n_tiles + 1,),
        in_specs=[ahead(A_X), ahead(A_G), _const_spec((CONV_WIDTH, gw)), _const_spec((1, gw)),
                  _const_spec((gw, 2 * gw)), _const_spec((1, 2 * gw)), _const_spec((1, gw)),
                  ahead(D_Q), ahead(D_F), ahead(D_V), ahead(D_G), _const_spec((depth, gw)), _const_spec((1, gw)),
                  _const_spec((HGRN_ROWS, HGRN_ROWS)), _const_spec((HGRN_ROWS, HGRN_ROWS)),
                  _const_spec((c, c, gw)), _const_spec((gw, gw)),
                  behind(gw), behind(gw), resident((d, d)), behind(d), vec, vec,
                  resident((d, d_ff)), resident((d_ff, d)), vec, vec],
        out_specs=[behind(d), behind(d)],
        out_shape=[jax.ShapeDtypeStruct((t, d), F32), jax.ShapeDtypeStruct((t, d), BF16)],
        scratch_shapes=[pltpu.VMEM((tm + SUBLANES, gw), F32), pltpu.VMEM((SUBLANES, gw), F32),
                        pltpu.VMEM((N_GROUP_HEADS // 2, 2 * HEAD_DIM, 2 * HEAD_DIM), F32),
                        pltpu.VMEM((n_sub, HGRN_ROWS, gw), F32), pltpu.VMEM((n_sub, HGRN_ROWS, gw), F32),
                        pltpu.VMEM((2, tm, gw), BF16), pltpu.VMEM((2, tm, gw), BF16)],
        compiler_params=_params("arbitrary"),
        name="mixers_ad_out_proj_mlp",
    )(pf2, pf2, conv_w, conv_b.reshape(1, gw), wg, bg, lam.reshape(1, gw),
      pf2, pf2, pf2, pf2, lb_logits, norm_g.reshape(1, gw), tri, sel, bias, eb,
      *ys, wo, h, g1.reshape(1, d), b1.reshape(1, d), wu, wd, g2.reshape(1, d), b2.reshape(1, d))


def _rglru_conv_gates(x, cw_ref, cb_ref, wg_ref, bg_ref, ext_ref, first_of_sequence):
    ts, gw = x.shape
    pad = SUBLANES
    ext_ref[0:pad, :] = jnp.where(first_of_sequence, 0.0, ext_ref[ts:ts + pad, :])
    ext_ref[pad:ts + pad, :] = x
    cw = cw_ref[...]
    last = CONV_WIDTH - 1
    xc = x * cw[last:last + 1, :] + cb_ref[...]
    for k in range(1, CONV_WIDTH):
        xc += ext_ref[pad - k:pad - k + ts, :] * cw[last - k:last - k + 1, :]

    return xc, jnp.dot(xc.astype(BF16), wg_ref[...], preferred_element_type=F32) + bg_ref[...]


def _rglru_scan(xc, gates, gate_in, lam_ref, hprev_ref, first_of_sequence):
    ts, gw = xc.shape
    r = jax.nn.sigmoid(gates[:, :gw])
    i = jax.nn.sigmoid(gates[:, gw:])
    lam = lam_ref[...]
    log_sig_lam = -(jnp.maximum(-lam, 0.0) + jnp.log1p(jnp.exp(-jnp.abs(lam))))
    log_a = RG_LRU_C * r * log_sig_lam
    a = jnp.exp(log_a)
    th = jnp.tanh(log_a)
    one_minus_a2 = -2.0 * th / (1.0 - th)
    u = jnp.sqrt(jnp.maximum(one_minus_a2, 0.0)) * (i * xc)

    sub = lax.broadcasted_iota(jnp.int32, (SUBLANES, gw), 0)
    h_in = jnp.where(first_of_sequence, 0.0, hprev_ref[SUBLANES - 1:SUBLANES, :])
    tiles = []
    for t0 in range(0, ts, SUBLANES):
        a_t = a[t0:t0 + SUBLANES, :]
        u_t = u[t0:t0 + SUBLANES, :]
        k = 1
        while k < SUBLANES:
            live = sub >= k
            a_sh = jnp.where(live, pltpu.roll(a_t, k, 0), 1.0)
            u_sh = jnp.where(live, pltpu.roll(u_t, k, 0), 0.0)
            u_t = a_t * u_sh + u_t
            a_t = a_t * a_sh
            k *= 2
        h_t = u_t + a_t * h_in
        h_in = h_t[SUBLANES - 1:SUBLANES, :]
        tiles.append(h_t)
    h = jnp.concatenate(tiles, axis=0)
    hprev_ref[...] = tiles[-1]
    return (jax.nn.gelu(gate_in, approximate=True) * h).astype(BF16)


def _retention_kernel(q_ref, k_ref, v_ref, g_ref, cos_ref, sin_ref, qdec_ref, kdec_ref, cdec_ref,
                      dmask_ref, ng_ref, eb_ref, o_ref, state_ref):
    ts, gw = q_ref.shape
    c = RET_CHUNK
    pw = 2 * HEAD_DIM
    n_pairs = N_GROUP_HEADS // 2

    @pl.when(pl.program_id(1) == 0)
    def _():
        state_ref[...] = jnp.zeros((gw, gw), F32)

    lane = lax.broadcasted_iota(jnp.int32, (ts, gw), 1)
    first_half = (lane % HEAD_DIM) < (HEAD_DIM // 2)
    cos = cos_ref[...]
    sin = sin_ref[...]

    def rotary(t):
        partner = jnp.where(first_half, pltpu.roll(t, gw - HEAD_DIM // 2, 1), pltpu.roll(t, HEAD_DIM // 2, 1))
        return t * cos + partner * sin

    q = rotary(q_ref[...])
    k = rotary(k_ref[...]) * (HEAD_DIM ** -0.5)
    qb = q.astype(BF16)
    kb = k.astype(BF16)
    q_dec = (q * qdec_ref[...]).astype(BF16)
    k_dec = (k * kdec_ref[...]).astype(BF16)
    vb = v_ref[...]
    eb = eb_ref[...]
    same_head = eb > 0
    lane_kv = lax.broadcasted_iota(jnp.int32, (c, pw), 1)
    chunks = [slice(n * c, (n + 1) * c) for n in range(ts // c)]

    kv = [lax.dot_general(k_dec[r, :], vb[r, :], (((0,), (0,)), ((), ())), preferred_element_type=F32)
          for r in chunks]
    state = state_ref[...]
    entering = []
    for kv_n in kv:
        entering.append(state.astype(BF16))
        state = state * cdec_ref[...] + jnp.where(same_head, kv_n, 0.0)
    state_ref[...] = state

    outs = []
    for r, state_n in zip(chunks, entering):
        inter = jnp.dot(q_dec[r, :], state_n, preferred_element_type=F32)
        intra = []
        for p in range(n_pairs):
            lanes = slice(p * pw, (p + 1) * pw)
            kbd = _pair_blockdiag(kb[r, lanes], lane_kv)
            vbd = _pair_blockdiag(vb[r, lanes], lane_kv)
            scores = lax.dot_general(qb[r, lanes], kbd, (((1,), (1,)), ((), ())), preferred_element_type=F32)
            intra.append(jnp.dot((scores * dmask_ref[p]).astype(BF16), vbd, preferred_element_type=F32))
        outs.append(inter + jnp.concatenate(intra, axis=1))
    o = jnp.concatenate(outs, axis=0)

    mu = _head_mean(o, eb)
    oc = o - mu
    var = _head_mean(oc * oc, eb)
    o = oc * lax.rsqrt(var + NORM_EPS) * ng_ref[...]
    o_ref[...] = (jax.nn.silu(g_ref[...]) * o).astype(BF16)


def _retention_tables(s_, rows):
    inv_freq = ROPE_BASE ** (-np.arange(0, HEAD_DIM, 2, dtype=np.float64) / HEAD_DIM)
    ang = np.arange(s_, dtype=np.float64)[:, None] * inv_freq[None, :]
    cos, sin = np.cos(ang), np.sin(ang)
    cos_t = np.tile(np.concatenate([cos, cos], axis=-1), (1, N_GROUP_HEADS))
    sin_t = np.tile(np.concatenate([-sin, sin], axis=-1), (1, N_GROUP_HEADS))
    c_ = RET_CHUNK
    log_gamma = np.log1p(-np.exp2(-5.0 - np.arange(N_GROUP_HEADS, dtype=np.float64)))
    pos = np.arange(c_, dtype=np.float64)
    diff = pos[:, None] - pos[None, :]
    dmask = np.where(diff >= 0, np.exp(log_gamma[:, None, None] * np.maximum(diff, 0.0)), 0.0)
    lanes = lambda per_head: np.repeat(per_head, HEAD_DIM, axis=-1)
    kdec = lanes(np.exp(log_gamma[None, :] * (c_ - 1.0 - pos)[:, None]))
    qdec = lanes(np.exp(log_gamma[None, :] * (pos + 1.0)[:, None]))
    cdec = lanes(np.exp(log_gamma * c_)[None, :])
    dmask = dmask.reshape(N_GROUP_HEADS // 2, 2, c_, c_).transpose(0, 2, 1, 3).reshape(N_GROUP_HEADS // 2, c_, 2 * c_)
    tables = (cos_t, sin_t, np.tile(qdec, (rows // c_, 1)), np.tile(kdec, (rows // c_, 1)), cdec, dmask)
    return tuple(jnp.asarray(t, F32) for t in tables)


def _retention(proj, proj_b, norm_g, eb, ts):
    b_, s_, _ = proj.shape
    gw = GROUP_WIDTH
    c_ = RET_CHUNK
    cos_t, sin_t, qdec, kdec, cdec, dmask = _retention_tables(s_, ts)
    col = lambda j: pl.BlockSpec((None, ts, gw), lambda b, n: (b, n, j))
    pos_spec = pl.BlockSpec((ts, gw), lambda b, n: (n, 0))
    return pl.pallas_call(
        _retention_kernel,
        grid=(b_, s_ // ts),
        in_specs=[col(R_Q), col(R_K), col(R_V), col(R_G), pos_spec, pos_spec,
                  _const_spec((ts, gw)), _const_spec((ts, gw)), _const_spec((1, gw)),
                  _const_spec((N_GROUP_HEADS // 2, c_, 2 * c_)), _const_spec((1, gw)), _const_spec((gw, gw))],
        out_specs=pl.BlockSpec((None, ts, gw), lambda b, n: (b, n, 0)),
        out_shape=jax.ShapeDtypeStruct((b_, s_, gw), BF16),
        scratch_shapes=[pltpu.VMEM((gw, gw), F32)],
        compiler_params=_params("parallel", "arbitrary"),
        name="retention",
    )(proj, proj, proj_b, proj, cos_t, sin_t, qdec, kdec, cdec, dmask, norm_g.reshape(1, gw), eb)


def _pair_blockdiag(x, lane):
    zero = jnp.zeros_like(x)
    return jnp.concatenate([jnp.where(lane < HEAD_DIM, x, zero), jnp.where(lane >= HEAD_DIM, x, zero)], axis=0)


def _stickbreak_kernel(q_ref, k_ref, v_ref, m_ref, o_ref, kbd_ref, vbd_ref):
    tq = q_ref.shape[0]
    blk = SB_BLOCK
    n_sub = tq // blk
    pw = 2 * HEAD_DIM
    n_pairs = N_GROUP_HEADS // 2
    n_blocks = k_ref.shape[0] // blk
    i = pl.program_id(1)
    scale = HEAD_DIM ** -0.5

    @pl.when(i == 0)
    def _():
        lane_kv = lax.broadcasted_iota(jnp.int32, (blk, pw), 1)

        def build(j, _):
            rows = pl.ds(pl.multiple_of(j * blk, blk), blk)
            for p in range(n_pairs):
                dst = pl.ds(pl.multiple_of((j * n_pairs + p) * (2 * blk), 2 * blk), 2 * blk)
                kbd_ref[dst, :] = _pair_blockdiag(k_ref[rows, p * pw:(p + 1) * pw], lane_kv)
                vbd_ref[dst, :] = _pair_blockdiag(v_ref[rows, p * pw:(p + 1) * pw], lane_kv)
            return 0

        lax.fori_loop(0, n_blocks, build, 0)

    row = lax.broadcasted_iota(jnp.int32, (blk, 2 * blk), 0)
    key_off = lax.broadcasted_iota(jnp.int32, (blk, 2 * blk), 1) % blk
    strictly_before = key_off < row
    m = m_ref[...]
    qs = [[q_ref[u * blk:(u + 1) * blk, p * pw:(p + 1) * pw] * scale for p in range(n_pairs)]
          for u in range(n_sub)]

    def step(first_block, carry, depth, diagonal_first, never_negative):
        chains = [(u, p, d) for u in range(n_sub) for p in range(n_pairs) for d in range(depth)]
        z2, vbd, log2_w, tot = {}, {}, {}, {}
        for u, p, d in chains:
            j = jnp.maximum(first_block + u - d, 0)
            base = pl.multiple_of((j * n_pairs + p) * (2 * blk), 2 * blk)
            vbd[u, p, d] = vbd_ref[pl.ds(base, 2 * blk), :]
            z2[u, p, d] = lax.dot_general(qs[u][p], kbd_ref[pl.ds(base, 2 * blk), :], (((1,), (1,)), ((), ())),
                                          preferred_element_type=F32) * LOG2_E
        for c in chains:
            softplus2 = jnp.maximum(z2[c], jnp.log2(1.0 + jnp.exp2(jnp.minimum(z2[c], EXP2_CLAMP))))
            if diagonal_first and c[2] == 0:
                softplus2 = jnp.where(strictly_before, softplus2, 0.0)
            hi, lo = _split2(softplus2)
            log2_w[c] = z2[c] + jnp.dot(jnp.concatenate([hi, lo], axis=1), m, preferred_element_type=F32)
            tot[c] = (jnp.sum(softplus2[:, :blk], axis=1, keepdims=True),
                      jnp.sum(softplus2[:, blk:], axis=1, keepdims=True))
        out = {}
        for u in range(n_sub):
            for p in range(n_pairs):
                acc, run0, run1 = carry[u][p]
                for d in range(depth):
                    c = (u, p, d)
                    use0, use1 = run0, run1
                    if not never_negative(u, d):
                        finished = first_block + u - d < 0
                        use0 = jnp.where(finished, -jnp.inf, run0)
                        use1 = jnp.where(finished, -jnp.inf, run1)
                    w = jnp.concatenate([jnp.exp2(log2_w[c][:, :blk] + use0), jnp.exp2(log2_w[c][:, blk:] + use1)],
                                        axis=1)
                    if diagonal_first and d == 0:
                        w = jnp.where(strictly_before, w, 0.0)
                    acc = acc + jnp.dot(w.astype(BF16), vbd[c], preferred_element_type=F32)
                    run0 = run0 - tot[c][0]
                    run1 = run1 - tot[c][1]
                out[u, p] = (acc, run0, run1)
        return tuple(tuple(out[u, p] for p in range(n_pairs)) for u in range(n_sub))

    zero_col = jnp.zeros((blk, 1), F32)
    carry = tuple(tuple((jnp.zeros((blk, pw), F32), zero_col, zero_col) for _ in range(n_pairs))
                  for _ in range(n_sub))
    carry = step(n_sub * i, carry, SB_FIRST_DEPTH, True, lambda u, d: u >= d)

    def any_weight_left(c):
        top = functools.reduce(jnp.maximum, [r for sub in c for _, run0, run1 in sub for r in (run0, run1)])
        return (jnp.max(top) > EXP2_F32_ZERO_BELOW).astype(jnp.int32)

    def sweep(state):
        t, _, c = state
        c = step(n_sub * i - SB_FIRST_DEPTH - t, c, 1, False, lambda u, d: u == n_sub - 1)
        return t + 1, any_weight_left(c), c

    n_trips = n_sub * i + n_sub - SB_FIRST_DEPTH
    _, _, carry = lax.while_loop(lambda state: (state[0] < n_trips) & (state[1] > 0), sweep,
                                 (jnp.int32(0), any_weight_left(carry), carry))
    for u in range(n_sub):
        for p in range(n_pairs):
            o_ref[u * blk:(u + 1) * blk, p * pw:(p + 1) * pw] = carry[u][p][0].astype(BF16)


def _stickbreak(proj_b, tq):
    b_, s_, _ = proj_b.shape
    gw = GROUP_WIDTH
    blk = SB_BLOCK
    idx = np.arange(2 * blk)
    same_head = (idx[:, None] // blk) == (idx[None, :] // blk)
    m = -(same_head & (idx[:, None] >= idx[None, :])).astype(np.float32)
    m = jnp.asarray(np.concatenate([m, m], axis=0), BF16)
    return pl.pallas_call(
        _stickbreak_kernel,
        grid=(b_, s_ // tq),
        in_specs=[pl.BlockSpec((None, tq, gw), lambda b, i: (b, i, S_Q)),
                  pl.BlockSpec((None, s_, gw), lambda b, i: (b, 0, S_K)),
                  pl.BlockSpec((None, s_, gw), lambda b, i: (b, 0, S_V)),
                  _const_spec((4 * blk, 2 * blk))],
        out_specs=pl.BlockSpec((None, tq, gw), lambda b, i: (b, i, 0)),
        out_shape=jax.ShapeDtypeStruct((b_, s_, gw), BF16),
        scratch_shapes=[pltpu.VMEM((2 * s_ * (N_GROUP_HEADS // 2), 2 * HEAD_DIM), BF16),
                        pltpu.VMEM((2 * s_ * (N_GROUP_HEADS // 2), 2 * HEAD_DIM), BF16)],
        compiler_params=_params("parallel", "arbitrary"),
        name="stickbreak",
    )(proj_b, proj_b, proj_b, m)


def _split3(x):
    p1 = x.astype(BF16)
    r1 = x - p1.astype(F32)
    p2 = r1.astype(BF16)
    p3 = (r1 - p2.astype(F32)).astype(BF16)
    return p1, p2, p3


def _dot3(a, parts):
    return sum(jnp.dot(a, p, preferred_element_type=F32) for p in parts)


def _hgrn_stages(layer, base, sub, first_of_sequence, q_ref, f_ref, v_ref, g_ref, lbl_ref, ng_ref, tri_ref, sel_ref,
                 bias_ref, eb_ref, state_ref, b2s_ref, keys_ref, out_ref):
    ts = HGRN_ROWS
    gw = GROUP_WIDTH
    c = HGRN_CHUNK
    n_chunks = ts // c
    pw = 2 * HEAD_DIM
    n_pairs = N_GROUP_HEADS // 2
    tile_rows = slice(base, base + ts)
    b2_ref = b2s_ref.at[sub]
    key_ref = keys_ref.at[sub]

    logits = lbl_ref[...]
    e = jnp.exp(logits - jnp.max(logits, axis=0, keepdims=True))
    lb_p = e / jnp.sum(e, axis=0, keepdims=True)
    lb = jnp.sum(lb_p[1:layer + 1, :], axis=0, keepdims=True) if layer > 0 else jnp.zeros((1, gw), F32)

    f_pre = f_ref[tile_rows, :]
    q = q_ref[tile_rows, :]
    f_gate = lb + (1.0 - lb) * jax.nn.sigmoid(f_pre)
    log_f = jnp.log(jnp.maximum(f_gate, GATE_FLOOR))
    kk = (1.0 - lb) * jax.nn.sigmoid(-f_pre)
    parts = _split3(log_f)
    b = _dot3(tri_ref[...], parts)
    b_tot = _dot3(sel_ref[...], parts)
    yield
    qe = (q * jnp.exp(b)).astype(BF16)
    kd = (kk * jnp.exp(b_tot - b)).astype(BF16)
    decay = jnp.exp(b_tot)
    b2_ref[...] = b * math.log2(math.e)
    key_ref[...] = (b - jnp.log(kk)) * math.log2(math.e)

    eb = eb_ref[...]
    vb = v_ref[tile_rows, :].astype(BF16)
    same_head = eb[0:pw, 0:pw] > 0
    rows = [slice(n * c, (n + 1) * c) for n in range(n_chunks)]
    lanes = [slice(p * pw, (p + 1) * pw) for p in range(n_pairs)]
    kv_t = [[lax.dot_general(vb[r, l], kd[r, l], (((0,), (0,)), ((), ())), preferred_element_type=F32)
             for r in rows] for l in lanes]
    states = []
    for p in range(n_pairs):
        state = state_ref[p]
        if first_of_sequence is not None:
            state = jnp.where(first_of_sequence, 0.0, state)
        entering = []
        for n in range(n_chunks):
            entering.append(state.astype(BF16))
            state = state * decay[n * c:n * c + 1, lanes[p]] + jnp.where(same_head, kv_t[p][n], 0.0)
        state_ref[p] = state
        states.append(entering)
    o_state = [jnp.concatenate([lax.dot_general(qe[rows[n], lanes[p]], states[p][n], (((1,), (1,)), ((), ())),
                                                preferred_element_type=F32) for n in range(n_chunks)], axis=0)
               for p in range(n_pairs)]
    yield

    half = c // 2
    o_intra = []
    for n in range(n_chunks):
        r0 = n * c
        blocks = []
        for s in range(c):
            lo = 0 if s < half else half
            key_s = key_ref[r0 + s:r0 + s + 1, :]
            exponent = (b2_ref[r0 + lo:r0 + c, :] - key_s) + bias_ref[s, lo:c, :]
            blocks.append(jnp.exp2(exponent) * q_ref[base + r0 + lo:base + r0 + c, :])
        g = jnp.dot(jnp.concatenate(blocks, axis=0).astype(BF16), eb, preferred_element_type=F32)
        top = jnp.zeros((half, gw), F32)
        bottom = jnp.zeros((half, gw), F32)
        for s in range(c):
            vs = v_ref[base + r0 + s:base + r0 + s + 1, :]
            if s < half:
                top += g[s * c:s * c + half, :] * vs
                bottom += g[s * c + half:(s + 1) * c, :] * vs
            else:
                start = half * c + (s - half) * half
                bottom += g[start:start + half, :] * vs
        o_intra.append(top)
        o_intra.append(bottom)
        if n == n_chunks // 2 - 1:
            yield
    o_intra = jnp.concatenate(o_intra, axis=0)

    o = jnp.concatenate(o_state, axis=1) + o_intra
    ms = _head_mean(o * o, eb)
    o = o * lax.rsqrt(ms + NORM_EPS)
    out_ref[tile_rows, :] = (o * ng_ref[...] * jax.nn.silu(g_ref[tile_rows, :])).astype(BF16)
    yield


def kernel(x, ln_in_g, ln_in_b, w_in, conv_w, conv_b, rg_wa, rg_ba, rg_wx, rg_bx, rg_lambda, ret_norm_g,
           hgrn_lb_logits, hgrn_norm_g, w_out, ln1_g, ln1_b, w_up, w_down, ln2_g, ln2_b):
    b_, s_, d = x.shape
    depth = w_in.shape[0]
    t = b_ * s_
    alpha = (2 * depth) ** 0.25
    tm = min(512, t)

    head = np.arange(GROUP_WIDTH) // HEAD_DIM
    eb = jnp.asarray(head[:, None] == head[None, :], BF16)

    h = hb = None
    gw = GROUP_WIDTH
    columns = lambda w, slices: jnp.concatenate([w[:, s * gw:(s + 1) * gw] for s in slices], axis=1).astype(BF16)
    for l in range(depth):
        wf, wb = columns(w_in[l], F32_SLICES), columns(w_in[l], BF16_SLICES)
        if l == 0:
            h, pf, pb = _proj(x.reshape(t, d), wf, wb, tm, ln=(ln_in_g, ln_in_b))
        else:
            pf, pb = _proj(hb, wf, wb, tm)
        pf3 = pf.reshape(b_, s_, -1)
        pb3 = pb.reshape(b_, s_, -1)
        y_b = _retention(pf3, pb3, ret_norm_g[l], eb, min(1024, s_))
        y_c = _stickbreak(pb3, min(256, s_))
        ys = [y.reshape(t, gw) for y in (y_b, y_c)]
        rg = (conv_w[l], conv_b[l], rg_wa[l], rg_ba[l], rg_wx[l], rg_bx[l], rg_lambda[l])
        hg = (hgrn_lb_logits, hgrn_norm_g[l], eb)
        h, hb = _block_tail(pf, rg, hg, ys, w_out[l].astype(BF16), h, ln1_g[l], ln1_b[l], w_up[l].astype(BF16),
                            w_down[l].astype(BF16), ln2_g[l], ln2_b[l], alpha, tm, 1024, s_, l)
    return h.reshape(b_, s_, d).astype(x.dtype)
```

```python
import functools
import math

import jax
import jax.numpy as jnp
import numpy as np
from jax import lax
from jax.experimental import pallas as pl
from jax.experimental.pallas import tpu as pltpu

F32 = jnp.float32
BF16 = jnp.bfloat16

HEAD_DIM = 64
N_GROUP_HEADS = 4
GROUP_WIDTH = HEAD_DIM * N_GROUP_HEADS
F32_SLICES = (0, 1, 2, 3, 5, 9, 10, 11, 12)
BF16_SLICES = (4, 6, 7, 8)
A_X, A_G, R_Q, R_K, R_G, D_Q, D_F, D_V, D_G = range(9)
R_V, S_Q, S_K, S_V = range(4)
CONV_WIDTH = 4
RG_LRU_C = 8.0
RET_CHUNK = 128
SB_BLOCK = 128
SB_FIRST_DEPTH = 3
HGRN_CHUNK = 16
ROPE_BASE = 10000.0
LN_EPS = 1e-5
NORM_EPS = 1e-6
GATE_FLOOR = 1e-30
EXP2_F32_ZERO_BELOW = -150.1
EXP2_CLAMP = 126.0
LOG2_E = math.log2(math.e)

VMEM_LIMIT_BYTES = 60 * 1024 * 1024
SUBLANES = 8
DENSE_ROW_PARTS = 2
DENSE_ROWS = 1024
FF_CHUNK = 1024
RGLRU_ROWS = 1024
RETENTION_ROWS = 1024
STICKBREAK_ROWS = 512
HGRN_ROWS = 256


def _params(*semantics):
    return pltpu.CompilerParams(dimension_semantics=semantics, vmem_limit_bytes=VMEM_LIMIT_BYTES)


def _const_spec(shape):
    zeros = (0,) * len(shape)
    return pl.BlockSpec(shape, lambda *_: zeros)


def _layer_norm_rows(x, g, b):
    mu = jnp.mean(x, axis=-1, keepdims=True)
    xc = x - mu
    var = jnp.mean(xc * xc, axis=-1, keepdims=True)
    return xc * lax.rsqrt(var + LN_EPS) * g + b


def _split2(x):
    hi = x.astype(BF16)
    lo = (x - hi.astype(F32)).astype(BF16)
    return hi, lo


def _head_mean(x, eb):
    hi, lo = _split2(x)
    s = jnp.dot(hi, eb, preferred_element_type=F32) + jnp.dot(lo, eb, preferred_element_type=F32)
    return s * (1.0 / HEAD_DIM)


def _proj_kernel(h_ref, wf_ref, wb_ref, pf_ref, pb_ref):
    h = h_ref[...]
    pf_ref[...] = jnp.dot(h, wf_ref[...], preferred_element_type=F32)
    pb_ref[...] = jnp.dot(h, wb_ref[...], preferred_element_type=F32).astype(BF16)


def _ln_proj_kernel(x_ref, g_ref, b_ref, wf_ref, wb_ref, h_ref, pf_ref, pb_ref):
    tm = x_ref.shape[0]
    for r in range(DENSE_ROW_PARTS):
        rows = slice(r * (tm // DENSE_ROW_PARTS), (r + 1) * (tm // DENSE_ROW_PARTS))
        h = _layer_norm_rows(x_ref[rows, :], g_ref[...], b_ref[...])
        h_ref[rows, :] = h
        hb = h.astype(BF16)
        pf_ref[rows, :] = jnp.dot(hb, wf_ref[...], preferred_element_type=F32)
        pb_ref[rows, :] = jnp.dot(hb, wb_ref[...], preferred_element_type=F32).astype(BF16)


def _proj(h, wf, wb, tm, ln=None):
    t, d = h.shape
    nf, nb = wf.shape[1], wb.shape[1]
    resident = lambda shape: pl.BlockSpec(shape, lambda i: (0, 0), pipeline_mode=pl.Buffered(1))
    row = lambda n: pl.BlockSpec((tm, n), lambda i: (i, 0))
    proj_shapes = [jax.ShapeDtypeStruct((t, nf), F32), jax.ShapeDtypeStruct((t, nb), BF16)]
    if ln is None:
        return pl.pallas_call(
            _proj_kernel,
            grid=(t // tm,),
            in_specs=[row(d), resident((d, nf)), resident((d, nb))],
            out_specs=[row(nf), row(nb)],
            out_shape=proj_shapes,
            compiler_params=_params("parallel"),
            name="in_proj",
        )(h, wf, wb)
    g, b = ln
    return pl.pallas_call(
        _ln_proj_kernel,
        grid=(t // tm,),
        in_specs=[row(d), _const_spec((1, d)), _const_spec((1, d)), resident((d, nf)), resident((d, nb))],
        out_specs=[row(d), row(nf), row(nb)],
        out_shape=[jax.ShapeDtypeStruct((t, d), F32)] + proj_shapes,
        compiler_params=_params("parallel"),
        name="ln_in_proj",
    )(h, g.reshape(1, d), b.reshape(1, d), wf, wb)


def _block_tail_kernel(alpha, ff_chunk, ya_ref, yb_ref, yc_ref, yd_ref, wo_ref, h_ref, g1_ref, b1_ref,
                       wu_ref, wd_ref, g2_ref, b2_ref, o_ref, ob_ref):
    gw = GROUP_WIDTH
    tm = h_ref.shape[0]
    parts = [slice(r * (tm // DENSE_ROW_PARTS), (r + 1) * (tm // DENSE_ROW_PARTS)) for r in range(DENSE_ROW_PARTS)]
    y_refs = (ya_ref, yb_ref, yc_ref, yd_ref)
    mix = [sum(jnp.dot(y_ref[rows, :], wo_ref[s * gw:(s + 1) * gw, :], preferred_element_type=F32)
               for s, y_ref in enumerate(y_refs)) for rows in parts]
    h1 = [_layer_norm_rows(alpha * h_ref[rows, :] + mix_r, g1_ref[...], b1_ref[...]) for rows, mix_r in zip(parts, mix)]
    hb = [h.astype(BF16) for h in h1]
    d_ff = wu_ref.shape[1]
    acc = [jnp.zeros(h.shape, F32) for h in h1]
    for c in range(d_ff // ff_chunk):
        cols = slice(c * ff_chunk, (c + 1) * ff_chunk)
        for r in range(DENSE_ROW_PARTS):
            u = jnp.dot(hb[r], wu_ref[:, cols], preferred_element_type=F32)
            u = jnp.square(jnp.maximum(u, 0.0)).astype(BF16)
            acc[r] += jnp.dot(u, wd_ref[cols, :], preferred_element_type=F32)
    for rows, h1_r, acc_r in zip(parts, h1, acc):
        h2 = _layer_norm_rows(alpha * h1_r + acc_r, g2_ref[...], b2_ref[...])
        o_ref[rows, :] = h2
        ob_ref[rows, :] = h2.astype(BF16)


def _block_tail(ys, wo, h, g1, b1, wu, wd, g2, b2, alpha, tm, ff_chunk):
    t, d = h.shape
    d_ff = wu.shape[1]
    gw = GROUP_WIDTH
    y_spec = pl.BlockSpec((tm, gw), lambda i: (i, 0))
    row_spec = pl.BlockSpec((tm, d), lambda i: (i, 0))
    resident = lambda shape: pl.BlockSpec(shape, lambda i: (0, 0), pipeline_mode=pl.Buffered(1))
    vec = _const_spec((1, d))
    return pl.pallas_call(
        functools.partial(_block_tail_kernel, alpha, ff_chunk),
        grid=(t // tm,),
        in_specs=[y_spec, y_spec, y_spec, y_spec, resident((d, d)), row_spec, vec, vec,
                  resident((d, d_ff)), resident((d_ff, d)), vec, vec],
        out_specs=[row_spec, row_spec],
        out_shape=[jax.ShapeDtypeStruct((t, d), F32), jax.ShapeDtypeStruct((t, d), BF16)],
        compiler_params=_params("parallel"),
        name="out_proj_mlp",
    )(*ys, wo, h, g1.reshape(1, d), b1.reshape(1, d), wu, wd, g2.reshape(1, d), b2.reshape(1, d))


def _rglru_kernel(xa_ref, ga_ref, cw_ref, cb_ref, wg_ref, bg_ref, lam_ref, o_ref, ext_ref, hprev_ref):
    ts, gw = xa_ref.shape
    si = pl.program_id(1)

    pad = SUBLANES

    @pl.when(si == 0)
    def _():
        ext_ref[0:pad, :] = jnp.zeros((pad, gw), F32)
        hprev_ref[...] = jnp.zeros((SUBLANES, gw), F32)

    @pl.when(si > 0)
    def _():
        ext_ref[0:pad, :] = ext_ref[ts:ts + pad, :]

    x = xa_ref[...]
    ext_ref[pad:ts + pad, :] = x
    cw = cw_ref[...]
    last = CONV_WIDTH - 1
    xc = x * cw[last:last + 1, :] + cb_ref[...]
    for k in range(1, CONV_WIDTH):
        xc += ext_ref[pad - k:pad - k + ts, :] * cw[last - k:last - k + 1, :]

    gates = jnp.dot(xc.astype(BF16), wg_ref[...], preferred_element_type=F32) + bg_ref[...]
    r = jax.nn.sigmoid(gates[:, :gw])
    i = jax.nn.sigmoid(gates[:, gw:])
    lam = lam_ref[...]
    log_sig_lam = -(jnp.maximum(-lam, 0.0) + jnp.log1p(jnp.exp(-jnp.abs(lam))))
    log_a = RG_LRU_C * r * log_sig_lam
    a = jnp.exp(log_a)
    th = jnp.tanh(log_a)
    one_minus_a2 = -2.0 * th / (1.0 - th)
    u = jnp.sqrt(jnp.maximum(one_minus_a2, 0.0)) * (i * xc)

    sub = lax.broadcasted_iota(jnp.int32, (SUBLANES, gw), 0)
    h_in = hprev_ref[SUBLANES - 1:SUBLANES, :]
    tiles = []
    for t0 in range(0, ts, SUBLANES):
        a_t = a[t0:t0 + SUBLANES, :]
        u_t = u[t0:t0 + SUBLANES, :]
        k = 1
        while k < SUBLANES:
            live = sub >= k
            a_sh = jnp.where(live, pltpu.roll(a_t, k, 0), 1.0)
            u_sh = jnp.where(live, pltpu.roll(u_t, k, 0), 0.0)
            u_t = a_t * u_sh + u_t
            a_t = a_t * a_sh
            k *= 2
        h_t = u_t + a_t * h_in
        h_in = h_t[SUBLANES - 1:SUBLANES, :]
        tiles.append(h_t)
    h = jnp.concatenate(tiles, axis=0)
    hprev_ref[...] = tiles[-1]
    o_ref[...] = (jax.nn.gelu(ga_ref[...], approximate=True) * h).astype(BF16)


def _blockdiag_heads(w):
    h, di, dj = w.shape
    eye = jnp.eye(h, dtype=w.dtype)
    return (eye[:, None, :, None] * w[:, :, None, :]).reshape(h * di, h * dj)


def _rglru(proj, conv_w, conv_b, wa, ba, wx, bx, lam, ts):
    b_, s_, _ = proj.shape
    gw = GROUP_WIDTH
    wg = jnp.concatenate([_blockdiag_heads(wa), _blockdiag_heads(wx)], axis=1).astype(BF16)
    bg = jnp.concatenate([ba.reshape(1, gw), bx.reshape(1, gw)], axis=1)
    col = lambda j: pl.BlockSpec((None, ts, gw), lambda b, s: (b, s, j))
    return pl.pallas_call(
        _rglru_kernel,
        grid=(b_, s_ // ts),
        in_specs=[col(A_X), col(A_G), _const_spec((CONV_WIDTH, gw)), _const_spec((1, gw)),
                  _const_spec((gw, 2 * gw)), _const_spec((1, 2 * gw)), _const_spec((1, gw))],
        out_specs=pl.BlockSpec((None, ts, gw), lambda b, s: (b, s, 0)),
        out_shape=jax.ShapeDtypeStruct((b_, s_, gw), BF16),
        scratch_shapes=[pltpu.VMEM((ts + SUBLANES, gw), F32), pltpu.VMEM((SUBLANES, gw), F32)],
        compiler_params=_params("parallel", "arbitrary"),
        name="rglru",
    )(proj, proj, conv_w, conv_b.reshape(1, gw), wg, bg, lam.reshape(1, gw))


def _retention_kernel(q_ref, k_ref, v_ref, g_ref, cos_ref, sin_ref, qdec_ref, kdec_ref, cdec_ref,
                      dmask_ref, ng_ref, eb_ref, o_ref, state_ref):
    ts, gw = q_ref.shape
    c = RET_CHUNK
    pw = 2 * HEAD_DIM
    n_pairs = N_GROUP_HEADS // 2

    @pl.when(pl.program_id(1) == 0)
    def _():
        state_ref[...] = jnp.zeros((gw, gw), F32)

    lane = lax.broadcasted_iota(jnp.int32, (ts, gw), 1)
    first_half = (lane % HEAD_DIM) < (HEAD_DIM // 2)
    cos = cos_ref[...]
    sin = sin_ref[...]

    def rotary(t):
        partner = jnp.where(first_half, pltpu.roll(t, gw - HEAD_DIM // 2, 1), pltpu.roll(t, HEAD_DIM // 2, 1))
        return t * cos + partner * sin

    q = rotary(q_ref[...])
    k = rotary(k_ref[...]) * (HEAD_DIM ** -0.5)
    qb = q.astype(BF16)
    kb = k.astype(BF16)
    q_dec = (q * qdec_ref[...]).astype(BF16)
    k_dec = (k * kdec_ref[...]).astype(BF16)
    vb = v_ref[...]
    eb = eb_ref[...]
    same_head = eb > 0
    lane_kv = lax.broadcasted_iota(jnp.int32, (c, pw), 1)
    chunks = [slice(n * c, (n + 1) * c) for n in range(ts // c)]

    kv = [lax.dot_general(k_dec[r, :], vb[r, :], (((0,), (0,)), ((), ())), preferred_element_type=F32)
          for r in chunks]
    state = state_ref[...]
    entering = []
    for kv_n in kv:
        entering.append(state.astype(BF16))
        state = state * cdec_ref[...] + jnp.where(same_head, kv_n, 0.0)
    state_ref[...] = state

    outs = []
    for r, state_n in zip(chunks, entering):
        inter = jnp.dot(q_dec[r, :], state_n, preferred_element_type=F32)
        intra = []
        for p in range(n_pairs):
            lanes = slice(p * pw, (p + 1) * pw)
            kbd = _pair_blockdiag(kb[r, lanes], lane_kv)
            vbd = _pair_blockdiag(vb[r, lanes], lane_kv)
            scores = lax.dot_general(qb[r, lanes], kbd, (((1,), (1,)), ((), ())), preferred_element_type=F32)
            intra.append(jnp.dot((scores * dmask_ref[p]).astype(BF16), vbd, preferred_element_type=F32))
        outs.append(inter + jnp.concatenate(intra, axis=1))
    o = jnp.concatenate(outs, axis=0)

    mu = _head_mean(o, eb)
    oc = o - mu
    var = _head_mean(oc * oc, eb)
    o = oc * lax.rsqrt(var + NORM_EPS) * ng_ref[...]
    o_ref[...] = (jax.nn.silu(g_ref[...]) * o).astype(BF16)


def _retention_tables(s_, rows):
    inv_freq = ROPE_BASE ** (-np.arange(0, HEAD_DIM, 2, dtype=np.float64) / HEAD_DIM)
    ang = np.arange(s_, dtype=np.float64)[:, None] * inv_freq[None, :]
    cos, sin = np.cos(ang), np.sin(ang)
    cos_t = np.tile(np.concatenate([cos, cos], axis=-1), (1, N_GROUP_HEADS))
    sin_t = np.tile(np.concatenate([-sin, sin], axis=-1), (1, N_GROUP_HEADS))
    c_ = RET_CHUNK
    log_gamma = np.log1p(-np.exp2(-5.0 - np.arange(N_GROUP_HEADS, dtype=np.float64)))
    pos = np.arange(c_, dtype=np.float64)
    diff = pos[:, None] - pos[None, :]
    dmask = np.where(diff >= 0, np.exp(log_gamma[:, None, None] * np.maximum(diff, 0.0)), 0.0)
    lanes = lambda per_head: np.repeat(per_head, HEAD_DIM, axis=-1)
    kdec = lanes(np.exp(log_gamma[None, :] * (c_ - 1.0 - pos)[:, None]))
    qdec = lanes(np.exp(log_gamma[None, :] * (pos + 1.0)[:, None]))
    cdec = lanes(np.exp(log_gamma * c_)[None, :])
    dmask = dmask.reshape(N_GROUP_HEADS // 2, 2, c_, c_).transpose(0, 2, 1, 3).reshape(N_GROUP_HEADS // 2, c_, 2 * c_)
    tables = (cos_t, sin_t, np.tile(qdec, (rows // c_, 1)), np.tile(kdec, (rows // c_, 1)), cdec, dmask)
    return tuple(jnp.asarray(t, F32) for t in tables)


def _retention(proj, proj_b, norm_g, eb, ts):
    b_, s_, _ = proj.shape
    gw = GROUP_WIDTH
    c_ = RET_CHUNK
    cos_t, sin_t, qdec, kdec, cdec, dmask = _retention_tables(s_, ts)
    col = lambda j: pl.BlockSpec((None, ts, gw), lambda b, n: (b, n, j))
    pos_spec = pl.BlockSpec((ts, gw), lambda b, n: (n, 0))
    return pl.pallas_call(
        _retention_kernel,
        grid=(b_, s_ // ts),
        in_specs=[col(R_Q), col(R_K), col(R_V), col(R_G), pos_spec, pos_spec,
                  _const_spec((ts, gw)), _const_spec((ts, gw)), _const_spec((1, gw)),
                  _const_spec((N_GROUP_HEADS // 2, c_, 2 * c_)), _const_spec((1, gw)), _const_spec((gw, gw))],
        out_specs=pl.BlockSpec((None, ts, gw), lambda b, n: (b, n, 0)),
        out_shape=jax.ShapeDtypeStruct((b_, s_, gw), BF16),
        scratch_shapes=[pltpu.VMEM((gw, gw), F32)],
        compiler_params=_params("parallel", "arbitrary"),
        name="retention",
    )(proj, proj, proj_b, proj, cos_t, sin_t, qdec, kdec, cdec, dmask, norm_g.reshape(1, gw), eb)


def _pair_blockdiag(x, lane):
    zero = jnp.zeros_like(x)
    return jnp.concatenate([jnp.where(lane < HEAD_DIM, x, zero), jnp.where(lane >= HEAD_DIM, x, zero)], axis=0)


def _stickbreak_kernel(q_ref, k_ref, v_ref, m_ref, o_ref, kbd_ref, vbd_ref):
    tq = q_ref.shape[0]
    blk = SB_BLOCK
    n_sub = tq // blk
    pw = 2 * HEAD_DIM
    n_pairs = N_GROUP_HEADS // 2
    n_blocks = k_ref.shape[0] // blk
    i = pl.program_id(1)
    scale = HEAD_DIM ** -0.5

    @pl.when(i == 0)
    def _():
        lane_kv = lax.broadcasted_iota(jnp.int32, (blk, pw), 1)

        def build(j, _):
            rows = pl.ds(pl.multiple_of(j * blk, blk), blk)
            for p in range(n_pairs):
                dst = pl.ds(pl.multiple_of((j * n_pairs + p) * (2 * blk), 2 * blk), 2 * blk)
                kbd_ref[dst, :] = _pair_blockdiag(k_ref[rows, p * pw:(p + 1) * pw], lane_kv)
                vbd_ref[dst, :] = _pair_blockdiag(v_ref[rows, p * pw:(p + 1) * pw], lane_kv)
            return 0

        lax.fori_loop(0, n_blocks, build, 0)

    row = lax.broadcasted_iota(jnp.int32, (blk, 2 * blk), 0)
    key_off = lax.broadcasted_iota(jnp.int32, (blk, 2 * blk), 1) % blk
    strictly_before = key_off < row
    m = m_ref[...]
    qs = [[q_ref[u * blk:(u + 1) * blk, p * pw:(p + 1) * pw] * scale for p in range(n_pairs)]
          for u in range(n_sub)]

    def step(first_block, carry, depth, diagonal_first, never_negative):
        chains = [(u, p, d) for u in range(n_sub) for p in range(n_pairs) for d in range(depth)]
        z2, vbd, log2_w, tot = {}, {}, {}, {}
        for u, p, d in chains:
            j = jnp.maximum(first_block + u - d, 0)
            base = pl.multiple_of((j * n_pairs + p) * (2 * blk), 2 * blk)
            vbd[u, p, d] = vbd_ref[pl.ds(base, 2 * blk), :]
            z2[u, p, d] = lax.dot_general(qs[u][p], kbd_ref[pl.ds(base, 2 * blk), :], (((1,), (1,)), ((), ())),
                                          preferred_element_type=F32) * LOG2_E
        for c in chains:
            softplus2 = jnp.maximum(z2[c], jnp.log2(1.0 + jnp.exp2(jnp.minimum(z2[c], EXP2_CLAMP))))
            if diagonal_first and c[2] == 0:
                softplus2 = jnp.where(strictly_before, softplus2, 0.0)
            hi, lo = _split2(softplus2)
            log2_w[c] = z2[c] + jnp.dot(jnp.concatenate([hi, lo], axis=1), m, preferred_element_type=F32)
            tot[c] = (jnp.sum(softplus2[:, :blk], axis=1, keepdims=True),
                      jnp.sum(softplus2[:, blk:], axis=1, keepdims=True))
        out = {}
        for u in range(n_sub):
            for p in range(n_pairs):
                acc, run0, run1 = carry[u][p]
                for d in range(depth):
                    c = (u, p, d)
                    use0, use1 = run0, run1
                    if not never_negative(u, d):
                        finished = first_block + u - d < 0
                        use0 = jnp.where(finished, -jnp.inf, run0)
                        use1 = jnp.where(finished, -jnp.inf, run1)
                    w = jnp.concatenate([jnp.exp2(log2_w[c][:, :blk] + use0), jnp.exp2(log2_w[c][:, blk:] + use1)],
                                        axis=1)
                    if diagonal_first and d == 0:
                        w = jnp.where(strictly_before, w, 0.0)
                    acc = acc + jnp.dot(w.astype(BF16), vbd[c], preferred_element_type=F32)
                    run0 = run0 - tot[c][0]
                    run1 = run1 - tot[c][1]
                out[u, p] = (acc, run0, run1)
        return tuple(tuple(out[u, p] for p in range(n_pairs)) for u in range(n_sub))

    zero_col = jnp.zeros((blk, 1), F32)
    carry = tuple(tuple((jnp.zeros((blk, pw), F32), zero_col, zero_col) for _ in range(n_pairs))
                  for _ in range(n_sub))
    carry = step(n_sub * i, carry, SB_FIRST_DEPTH, True, lambda u, d: u >= d)

    def any_weight_left(c):
        top = functools.reduce(jnp.maximum, [r for sub in c for _, run0, run1 in sub for r in (run0, run1)])
        return (jnp.max(top) > EXP2_F32_ZERO_BELOW).astype(jnp.int32)

    def sweep(state):
        t, _, c = state
        c = step(n_sub * i - SB_FIRST_DEPTH - t, c, 1, False, lambda u, d: u == n_sub - 1)
        return t + 1, any_weight_left(c), c

    n_trips = n_sub * i + n_sub - SB_FIRST_DEPTH
    _, _, carry = lax.while_loop(lambda state: (state[0] < n_trips) & (state[1] > 0), sweep,
                                 (jnp.int32(0), any_weight_left(carry), carry))
    for u in range(n_sub):
        for p in range(n_pairs):
            o_ref[u * blk:(u + 1) * blk, p * pw:(p + 1) * pw] = carry[u][p][0].astype(BF16)


def _stickbreak(proj_b, tq):
    b_, s_, _ = proj_b.shape
    gw = GROUP_WIDTH
    blk = SB_BLOCK
    idx = np.arange(2 * blk)
    same_head = (idx[:, None] // blk) == (idx[None, :] // blk)
    m = -(same_head & (idx[:, None] >= idx[None, :])).astype(np.float32)
    m = jnp.asarray(np.concatenate([m, m], axis=0), BF16)
    return pl.pallas_call(
        _stickbreak_kernel,
        grid=(b_, s_ // tq),
        in_specs=[pl.BlockSpec((None, tq, gw), lambda b, i: (b, i, S_Q)),
                  pl.BlockSpec((None, s_, gw), lambda b, i: (b, 0, S_K)),
                  pl.BlockSpec((None, s_, gw), lambda b, i: (b, 0, S_V)),
                  _const_spec((4 * blk, 2 * blk))],
        out_specs=pl.BlockSpec((None, tq, gw), lambda b, i: (b, i, 0)),
        out_shape=jax.ShapeDtypeStruct((b_, s_, gw), BF16),
        scratch_shapes=[pltpu.VMEM((2 * s_ * (N_GROUP_HEADS // 2), 2 * HEAD_DIM), BF16),
                        pltpu.VMEM((2 * s_ * (N_GROUP_HEADS // 2), 2 * HEAD_DIM), BF16)],
        compiler_params=_params("parallel", "arbitrary"),
        name="stickbreak",
    )(proj_b, proj_b, proj_b, m)


def _split3(x):
    p1 = x.astype(BF16)
    r1 = x - p1.astype(F32)
    p2 = r1.astype(BF16)
    p3 = (r1 - p2.astype(F32)).astype(BF16)
    return p1, p2, p3


def _dot3(a, parts):
    return sum(jnp.dot(a, p, preferred_element_type=F32) for p in parts)


def _hgrn_kernel(layer, q_ref, f_ref, v_ref, g_ref, lbl_ref, ng_ref, tri_ref, sel_ref, bias_ref, eb_ref, o_ref,
                 state_ref, b2_ref, kk_ref):
    ts, gw = q_ref.shape
    c = HGRN_CHUNK
    n_chunks = ts // c
    pw = 2 * HEAD_DIM
    n_pairs = N_GROUP_HEADS // 2

    @pl.when(pl.program_id(1) == 0)
    def _():
        state_ref[...] = jnp.zeros(state_ref.shape, F32)

    logits = lbl_ref[...]
    e = jnp.exp(logits - jnp.max(logits, axis=0, keepdims=True))
    lb_p = e / jnp.sum(e, axis=0, keepdims=True)
    lb = jnp.sum(lb_p[1:layer + 1, :], axis=0, keepdims=True) if layer > 0 else jnp.zeros((1, gw), F32)

    f_pre = f_ref[...]
    q = q_ref[...]
    f_gate = lb + (1.0 - lb) * jax.nn.sigmoid(f_pre)
    log_f = jnp.log(jnp.maximum(f_gate, GATE_FLOOR))
    kk = (1.0 - lb) * jax.nn.sigmoid(-f_pre)
    parts = _split3(log_f)
    b = _dot3(tri_ref[...], parts)
    b_tot = _dot3(sel_ref[...], parts)
    qe = (q * jnp.exp(b)).astype(BF16)
    kd = (kk * jnp.exp(b_tot - b)).astype(BF16)
    decay = jnp.exp(b_tot)
    b2_ref[...] = b * math.log2(math.e)
    kk_ref[...] = (b - jnp.log(kk)) * math.log2(math.e)

    eb = eb_ref[...]
    vb = v_ref[...].astype(BF16)
    same_head = eb[0:pw, 0:pw] > 0
    rows = [slice(n * c, (n + 1) * c) for n in range(n_chunks)]
    lanes = [slice(p * pw, (p + 1) * pw) for p in range(n_pairs)]
    kv_t = [[lax.dot_general(vb[r, l], kd[r, l], (((0,), (0,)), ((), ())), preferred_element_type=F32)
             for r in rows] for l in lanes]
    states = []
    for p in range(n_pairs):
        state = state_ref[p]
        entering = []
        for n in range(n_chunks):
            entering.append(state.astype(BF16))
            state = state * decay[n * c:n * c + 1, lanes[p]] + jnp.where(same_head, kv_t[p][n], 0.0)
        state_ref[p] = state
        states.append(entering)
    o_state = [jnp.concatenate([lax.dot_general(qe[rows[n], lanes[p]], states[p][n], (((1,), (1,)), ((), ())),
                                                preferred_element_type=F32) for n in range(n_chunks)], axis=0)
               for p in range(n_pairs)]

    half = c // 2
    o_intra = []
    for n in range(n_chunks):
        r0 = n * c
        blocks = []
        for s in range(c):
            lo = 0 if s < half else half
            key_s = kk_ref[r0 + s:r0 + s + 1, :]
            exponent = (b2_ref[r0 + lo:r0 + c, :] - key_s) + bias_ref[s, lo:c, :]
            blocks.append(jnp.exp2(exponent) * q_ref[r0 + lo:r0 + c, :])
        g = jnp.dot(jnp.concatenate(blocks, axis=0).astype(BF16), eb, preferred_element_type=F32)
        top = jnp.zeros((half, gw), F32)
        bottom = jnp.zeros((half, gw), F32)
        for s in range(c):
            vs = v_ref[r0 + s:r0 + s + 1, :]
            if s < half:
                top += g[s * c:s * c + half, :] * vs
                bottom += g[s * c + half:(s + 1) * c, :] * vs
            else:
                start = half * c + (s - half) * half
                bottom += g[start:start + half, :] * vs
        o_intra.append(top)
        o_intra.append(bottom)
    o_intra = jnp.concatenate(o_intra, axis=0)

    o = jnp.concatenate(o_state, axis=1) + o_intra
    ms = _head_mean(o * o, eb)
    o = o * lax.rsqrt(ms + NORM_EPS)
    o_ref[...] = (o * ng_ref[...] * jax.nn.silu(g_ref[...])).astype(BF16)


def _hgrn(proj, lb_logits, norm_g, eb, layer, ts):
    b_, s_, _ = proj.shape
    gw = GROUP_WIDTH
    c = HGRN_CHUNK
    depth = lb_logits.shape[0]
    row = np.arange(ts)
    same_chunk = (row[:, None] // c) == (row[None, :] // c)
    tri = jnp.asarray(same_chunk & (row[:, None] >= row[None, :]), BF16)
    sel = jnp.asarray(same_chunk, BF16)
    pos = np.arange(c)
    bias = np.where(pos[None, :, None] >= pos[:, None, None], 0.0, -1e30)
    bias = jnp.asarray(np.broadcast_to(bias, (c, c, gw)), F32)
    col = lambda j: pl.BlockSpec((None, ts, gw), lambda b, s: (b, s, j))
    return pl.pallas_call(
        functools.partial(_hgrn_kernel, layer),
        grid=(b_, s_ // ts),
        in_specs=[col(D_Q), col(D_F), col(D_V), col(D_G), _const_spec((depth, gw)), _const_spec((1, gw)),
                  _const_spec((ts, ts)), _const_spec((ts, ts)), _const_spec((c, c, gw)), _const_spec((gw, gw))],
        out_specs=pl.BlockSpec((None, ts, gw), lambda b, s: (b, s, 0)),
        out_shape=jax.ShapeDtypeStruct((b_, s_, gw), BF16),
        scratch_shapes=[pltpu.VMEM((N_GROUP_HEADS // 2, 2 * HEAD_DIM, 2 * HEAD_DIM), F32),
                        pltpu.VMEM((ts, gw), F32), pltpu.VMEM((ts, gw), F32)],
        compiler_params=_params("parallel", "arbitrary"),
        name="hgrn2",
    )(proj, proj, proj, proj, lb_logits, norm_g.reshape(1, gw), tri, sel, bias, eb)


def kernel(x, ln_in_g, ln_in_b, w_in, conv_w, conv_b, rg_wa, rg_ba, rg_wx, rg_bx, rg_lambda, ret_norm_g,
           hgrn_lb_logits, hgrn_norm_g, w_out, ln1_g, ln1_b, w_up, w_down, ln2_g, ln2_b):
    b_, s_, d = x.shape
    depth = w_in.shape[0]
    t = b_ * s_
    alpha = (2 * depth) ** 0.25
    tm = min(DENSE_ROWS, t)

    head = np.arange(GROUP_WIDTH) // HEAD_DIM
    eb = jnp.asarray(head[:, None] == head[None, :], BF16)

    h = hb = None
    gw = GROUP_WIDTH
    columns = lambda w, slices: jnp.concatenate([w[:, s * gw:(s + 1) * gw] for s in slices], axis=1).astype(BF16)
    for l in range(depth):
        wf, wb = columns(w_in[l], F32_SLICES), columns(w_in[l], BF16_SLICES)
        if l == 0:
            h, pf, pb = _proj(x.reshape(t, d), wf, wb, tm, ln=(ln_in_g, ln_in_b))
        else:
            pf, pb = _proj(hb, wf, wb, tm)
        pf = pf.reshape(b_, s_, -1)
        pb = pb.reshape(b_, s_, -1)
        y_a = _rglru(pf, conv_w[l], conv_b[l], rg_wa[l], rg_ba[l], rg_wx[l], rg_bx[l], rg_lambda[l],
                     min(RGLRU_ROWS, s_))
        y_b = _retention(pf, pb, ret_norm_g[l], eb, min(RETENTION_ROWS, s_))
        y_c = _stickbreak(pb, min(STICKBREAK_ROWS, s_))
        y_d = _hgrn(pf, hgrn_lb_logits, hgrn_norm_g[l], eb, l, min(HGRN_ROWS, s_))
        ys = [y.reshape(t, gw) for y in (y_a, y_b, y_c, y_d)]
        h, hb = _block_tail(ys, w_out[l].astype(BF16), h, ln1_g[l], ln1_b[l], w_up[l].astype(BF16),
                            w_down[l].astype(BF16), ln2_g[l], ln2_b[l], alpha, tm, FF_CHUNK)
    return h.reshape(b_, s_, d).astype(x.dtype)
```

```python
import functools
import math

import jax
import jax.numpy as jnp
import numpy as np
from jax import lax
from jax.experimental import pallas as pl
from jax.experimental.pallas import tpu as pltpu

F32 = jnp.float32
BF16 = jnp.bfloat16

HEAD_DIM = 64
N_GROUP_HEADS = 4
GROUP_WIDTH = HEAD_DIM * N_GROUP_HEADS
F32_SLICES = (0, 1, 2, 3, 5, 9, 10, 11, 12)
BF16_SLICES = (4, 6, 7, 8)
A_X, A_G, R_Q, R_K, R_G, D_Q, D_F, D_V, D_G = range(9)
R_V, S_Q, S_K, S_V = range(4)
CONV_WIDTH = 4
RG_LRU_C = 8.0
RET_CHUNK = 128
SB_BLOCK = 128
SB_FIRST_DEPTH = 3
HGRN_CHUNK = 16
ROPE_BASE = 10000.0
LN_EPS = 1e-5
NORM_EPS = 1e-6
GATE_FLOOR = 1e-30
EXP2_F32_ZERO_BELOW = -150.1
EXP2_CLAMP = 126.0
LOG2_E = math.log2(math.e)

VMEM_LIMIT_BYTES = 56 * 1024 * 1024
SUBLANES = 8
DENSE_ROW_PARTS = 2
PROJ_ROWS = 1024
TAIL_ROWS = 512
FF_CHUNK = 1024
RGLRU_ROWS = 2048
RETENTION_ROWS = 2048
STICKBREAK_ROWS = 512
HGRN_ROWS = 512


def _params(*semantics):
    return pltpu.CompilerParams(dimension_semantics=semantics, vmem_limit_bytes=VMEM_LIMIT_BYTES)


def _const_spec(shape):
    zeros = (0,) * len(shape)
    return pl.BlockSpec(shape, lambda *_: zeros)


def _layer_norm_rows(x, g, b):
    mu = jnp.mean(x, axis=-1, keepdims=True)
    xc = x - mu
    var = jnp.mean(xc * xc, axis=-1, keepdims=True)
    return xc * lax.rsqrt(var + LN_EPS) * g + b


def _split2(x):
    hi = x.astype(BF16)
    lo = (x - hi.astype(F32)).astype(BF16)
    return hi, lo


def _head_mean(x, eb):
    hi, lo = _split2(x)
    s = jnp.dot(hi, eb, preferred_element_type=F32) + jnp.dot(lo, eb, preferred_element_type=F32)
    return s * (1.0 / HEAD_DIM)


def _proj_kernel(h_ref, wf_ref, wb_ref, pf_ref, pb_ref):
    h = h_ref[...]
    pf_ref[...] = jnp.dot(h, wf_ref[...], preferred_element_type=F32)
    pb_ref[...] = jnp.dot(h, wb_ref[...], preferred_element_type=F32).astype(BF16)


def _ln_proj_kernel(x_ref, g_ref, b_ref, wf_ref, wb_ref, h_ref, pf_ref, pb_ref):
    tm = x_ref.shape[0]
    for r in range(DENSE_ROW_PARTS):
        rows = slice(r * (tm // DENSE_ROW_PARTS), (r + 1) * (tm // DENSE_ROW_PARTS))
        h = _layer_norm_rows(x_ref[rows, :], g_ref[...], b_ref[...])
        h_ref[rows, :] = h
        hb = h.astype(BF16)
        pf_ref[rows, :] = jnp.dot(hb, wf_ref[...], preferred_element_type=F32)
        pb_ref[rows, :] = jnp.dot(hb, wb_ref[...], preferred_element_type=F32).astype(BF16)


def _proj(h, wf, wb, tm, ln=None):
    t, d = h.shape
    nf, nb = wf.shape[1], wb.shape[1]
    resident = lambda shape: pl.BlockSpec(shape, lambda i: (0, 0), pipeline_mode=pl.Buffered(1))
    row = lambda n: pl.BlockSpec((tm, n), lambda i: (i, 0))
    proj_shapes = [jax.ShapeDtypeStruct((t, nf), F32), jax.ShapeDtypeStruct((t, nb), BF16)]
    if ln is None:
        return pl.pallas_call(
            _proj_kernel,
            grid=(t // tm,),
            in_specs=[row(d), resident((d, nf)), resident((d, nb))],
            out_specs=[row(nf), row(nb)],
            out_shape=proj_shapes,
            compiler_params=_params("parallel"),
            name="in_proj",
        )(h, wf, wb)
    g, b = ln
    return pl.pallas_call(
        _ln_proj_kernel,
        grid=(t // tm,),
        in_specs=[row(d), _const_spec((1, d)), _const_spec((1, d)), resident((d, nf)), resident((d, nb))],
        out_specs=[row(d), row(nf), row(nb)],
        out_shape=[jax.ShapeDtypeStruct((t, d), F32)] + proj_shapes,
        compiler_params=_params("parallel"),
        name="ln_in_proj",
    )(h, g.reshape(1, d), b.reshape(1, d), wf, wb)


def _block_tail_kernel(alpha, ff_chunk, ya_ref, yb_ref, yc_ref, yd_ref, wo_ref, h_ref, g1_ref, b1_ref,
                       wu_ref, wd_ref, g2_ref, b2_ref, o_ref, ob_ref):
    gw = GROUP_WIDTH
    tm = h_ref.shape[0]
    parts = [slice(r * (tm // DENSE_ROW_PARTS), (r + 1) * (tm // DENSE_ROW_PARTS)) for r in range(DENSE_ROW_PARTS)]
    y_refs = (ya_ref, yb_ref, yc_ref, yd_ref)
    mix = [sum(jnp.dot(y_ref[rows, :], wo_ref[s * gw:(s + 1) * gw, :], preferred_element_type=F32)
               for s, y_ref in enumerate(y_refs)) for rows in parts]
    h1 = [_layer_norm_rows(alpha * h_ref[rows, :] + mix_r, g1_ref[...], b1_ref[...]) for rows, mix_r in zip(parts, mix)]
    hb = [h.astype(BF16) for h in h1]
    d_ff = wu_ref.shape[1]
    acc = [jnp.zeros(h.shape, F32) for h in h1]
    for c in range(d_ff // ff_chunk):
        cols = slice(c * ff_chunk, (c + 1) * ff_chunk)
        for r in range(DENSE_ROW_PARTS):
            u = jnp.dot(hb[r], wu_ref[:, cols], preferred_element_type=F32)
            u = jnp.square(jnp.maximum(u, 0.0)).astype(BF16)
            acc[r] += jnp.dot(u, wd_ref[cols, :], preferred_element_type=F32)
    for rows, h1_r, acc_r in zip(parts, h1, acc):
        h2 = _layer_norm_rows(alpha * h1_r + acc_r, g2_ref[...], b2_ref[...])
        o_ref[rows, :] = h2
        ob_ref[rows, :] = h2.astype(BF16)


def _block_tail(ys, wo, h, g1, b1, wu, wd, g2, b2, alpha, tm, ff_chunk):
    t, d = h.shape
    d_ff = wu.shape[1]
    gw = GROUP_WIDTH
    y_spec = pl.BlockSpec((tm, gw), lambda i: (i, 0))
    row_spec = pl.BlockSpec((tm, d), lambda i: (i, 0))
    resident = lambda shape: pl.BlockSpec(shape, lambda i: (0, 0), pipeline_mode=pl.Buffered(1))
    vec = _const_spec((1, d))
    return pl.pallas_call(
        functools.partial(_block_tail_kernel, alpha, ff_chunk),
        grid=(t // tm,),
        in_specs=[y_spec, y_spec, y_spec, y_spec, resident((d, d)), row_spec, vec, vec,
                  resident((d, d_ff)), resident((d_ff, d)), vec, vec],
        out_specs=[row_spec, row_spec],
        out_shape=[jax.ShapeDtypeStruct((t, d), F32), jax.ShapeDtypeStruct((t, d), BF16)],
        compiler_params=_params("parallel"),
        name="out_proj_mlp",
    )(*ys, wo, h, g1.reshape(1, d), b1.reshape(1, d), wu, wd, g2.reshape(1, d), b2.reshape(1, d))


def _rglru_kernel(xa_ref, ga_ref, cw_ref, cb_ref, wg_ref, bg_ref, lam_ref, o_ref, ext_ref, hprev_ref):
    ts, gw = xa_ref.shape
    si = pl.program_id(1)

    pad = SUBLANES

    @pl.when(si == 0)
    def _():
        ext_ref[0:pad, :] = jnp.zeros((pad, gw), F32)
        hprev_ref[...] = jnp.zeros((SUBLANES, gw), F32)

    @pl.when(si > 0)
    def _():
        ext_ref[0:pad, :] = ext_ref[ts:ts + pad, :]

    x = xa_ref[...]
    ext_ref[pad:ts + pad, :] = x
    cw = cw_ref[...]
    last = CONV_WIDTH - 1
    xc = x * cw[last:last + 1, :] + cb_ref[...]
    for k in range(1, CONV_WIDTH):
        xc += ext_ref[pad - k:pad - k + ts, :] * cw[last - k:last - k + 1, :]

    gates = jnp.dot(xc.astype(BF16), wg_ref[...], preferred_element_type=F32) + bg_ref[...]
    r = jax.nn.sigmoid(gates[:, :gw])
    i = jax.nn.sigmoid(gates[:, gw:])
    lam = lam_ref[...]
    log_sig_lam = -(jnp.maximum(-lam, 0.0) + jnp.log1p(jnp.exp(-jnp.abs(lam))))
    log_a = RG_LRU_C * r * log_sig_lam
    a = jnp.exp(log_a)
    th = jnp.tanh(log_a)
    one_minus_a2 = -2.0 * th / (1.0 - th)
    u = jnp.sqrt(jnp.maximum(one_minus_a2, 0.0)) * (i * xc)

    sub = lax.broadcasted_iota(jnp.int32, (SUBLANES, gw), 0)
    h_in = hprev_ref[SUBLANES - 1:SUBLANES, :]
    tiles = []
    for t0 in range(0, ts, SUBLANES):
        a_t = a[t0:t0 + SUBLANES, :]
        u_t = u[t0:t0 + SUBLANES, :]
        k = 1
        while k < SUBLANES:
            live = sub >= k
            a_sh = jnp.where(live, pltpu.roll(a_t, k, 0), 1.0)
            u_sh = jnp.where(live, pltpu.roll(u_t, k, 0), 0.0)
            u_t = a_t * u_sh + u_t
            a_t = a_t * a_sh
            k *= 2
        h_t = u_t + a_t * h_in
        h_in = h_t[SUBLANES - 1:SUBLANES, :]
        tiles.append(h_t)
    h = jnp.concatenate(tiles, axis=0)
    hprev_ref[...] = tiles[-1]
    o_ref[...] = (jax.nn.gelu(ga_ref[...], approximate=True) * h).astype(BF16)


def _blockdiag_heads(w):
    h, di, dj = w.shape
    eye = jnp.eye(h, dtype=w.dtype)
    return (eye[:, None, :, None] * w[:, :, None, :]).reshape(h * di, h * dj)


def _rglru(proj, conv_w, conv_b, wa, ba, wx, bx, lam, ts):
    b_, s_, _ = proj.shape
    gw = GROUP_WIDTH
    wg = jnp.concatenate([_blockdiag_heads(wa), _blockdiag_heads(wx)], axis=1).astype(BF16)
    bg = jnp.concatenate([ba.reshape(1, gw), bx.reshape(1, gw)], axis=1)
    col = lambda j: pl.BlockSpec((None, ts, gw), lambda b, s: (b, s, j))
    return pl.pallas_call(
        _rglru_kernel,
        grid=(b_, s_ // ts),
        in_specs=[col(A_X), col(A_G), _const_spec((CONV_WIDTH, gw)), _const_spec((1, gw)),
                  _const_spec((gw, 2 * gw)), _const_spec((1, 2 * gw)), _const_spec((1, gw))],
        out_specs=pl.BlockSpec((None, ts, gw), lambda b, s: (b, s, 0)),
        out_shape=jax.ShapeDtypeStruct((b_, s_, gw), BF16),
        scratch_shapes=[pltpu.VMEM((ts + SUBLANES, gw), F32), pltpu.VMEM((SUBLANES, gw), F32)],
        compiler_params=_params("parallel", "arbitrary"),
        name="rglru",
    )(proj, proj, conv_w, conv_b.reshape(1, gw), wg, bg, lam.reshape(1, gw))


def _retention_kernel(q_ref, k_ref, v_ref, g_ref, cos_ref, sin_ref, qdec_ref, kdec_ref, cdec_ref,
                      dmask_ref, ng_ref, eb_ref, o_ref, state_ref):
    ts, gw = q_ref.shape
    c = RET_CHUNK
    pw = 2 * HEAD_DIM
    n_pairs = N_GROUP_HEADS // 2

    @pl.when(pl.program_id(1) == 0)
    def _():
        state_ref[...] = jnp.zeros((gw, gw), F32)

    lane = lax.broadcasted_iota(jnp.int32, (ts, gw), 1)
    first_half = (lane % HEAD_DIM) < (HEAD_DIM // 2)
    cos = cos_ref[...]
    sin = sin_ref[...]

    def rotary(t):
        partner = jnp.where(first_half, pltpu.roll(t, gw - HEAD_DIM // 2, 1), pltpu.roll(t, HEAD_DIM // 2, 1))
        return t * cos + partner * sin

    q = rotary(q_ref[...])
    k = rotary(k_ref[...]) * (HEAD_DIM ** -0.5)
    qb = q.astype(BF16)
    kb = k.astype(BF16)
    q_dec = (q * qdec_ref[...]).astype(BF16)
    k_dec = (k * kdec_ref[...]).astype(BF16)
    vb = v_ref[...]
    eb = eb_ref[...]
    same_head = eb > 0
    lane_kv = lax.broadcasted_iota(jnp.int32, (c, pw), 1)
    chunks = [slice(n * c, (n + 1) * c) for n in range(ts // c)]

    kv = [lax.dot_general(k_dec[r, :], vb[r, :], (((0,), (0,)), ((), ())), preferred_element_type=F32)
          for r in chunks]
    state = state_ref[...]
    entering = []
    for kv_n in kv:
        entering.append(state.astype(BF16))
        state = state * cdec_ref[...] + jnp.where(same_head, kv_n, 0.0)
    state_ref[...] = state

    outs = []
    for r, state_n in zip(chunks, entering):
        inter = jnp.dot(q_dec[r, :], state_n, preferred_element_type=F32)
        intra = []
        for p in range(n_pairs):
            lanes = slice(p * pw, (p + 1) * pw)
            kbd = _pair_blockdiag(kb[r, lanes], lane_kv)
            vbd = _pair_blockdiag(vb[r, lanes], lane_kv)
            scores = lax.dot_general(qb[r, lanes], kbd, (((1,), (1,)), ((), ())), preferred_element_type=F32)
            intra.append(jnp.dot((scores * dmask_ref[p]).astype(BF16), vbd, preferred_element_type=F32))
        outs.append(inter + jnp.concatenate(intra, axis=1))
    o = jnp.concatenate(outs, axis=0)

    mu = _head_mean(o, eb)
    oc = o - mu
    var = _head_mean(oc * oc, eb)
    o = oc * lax.rsqrt(var + NORM_EPS) * ng_ref[...]
    o_ref[...] = (jax.nn.silu(g_ref[...]) * o).astype(BF16)


def _retention_tables(s_, rows):
    inv_freq = ROPE_BASE ** (-np.arange(0, HEAD_DIM, 2, dtype=np.float64) / HEAD_DIM)
    ang = np.arange(s_, dtype=np.float64)[:, None] * inv_freq[None, :]
    cos, sin = np.cos(ang), np.sin(ang)
    cos_t = np.tile(np.concatenate([cos, cos], axis=-1), (1, N_GROUP_HEADS))
    sin_t = np.tile(np.concatenate([-sin, sin], axis=-1), (1, N_GROUP_HEADS))
    c_ = RET_CHUNK
    log_gamma = np.log1p(-np.exp2(-5.0 - np.arange(N_GROUP_HEADS, dtype=np.float64)))
    pos = np.arange(c_, dtype=np.float64)
    diff = pos[:, None] - pos[None, :]
    dmask = np.where(diff >= 0, np.exp(log_gamma[:, None, None] * np.maximum(diff, 0.0)), 0.0)
    lanes = lambda per_head: np.repeat(per_head, HEAD_DIM, axis=-1)
    kdec = lanes(np.exp(log_gamma[None, :] * (c_ - 1.0 - pos)[:, None]))
    qdec = lanes(np.exp(log_gamma[None, :] * (pos + 1.0)[:, None]))
    cdec = lanes(np.exp(log_gamma * c_)[None, :])
    dmask = dmask.reshape(N_GROUP_HEADS // 2, 2, c_, c_).transpose(0, 2, 1, 3).reshape(N_GROUP_HEADS // 2, c_, 2 * c_)
    tables = (cos_t, sin_t, np.tile(qdec, (rows // c_, 1)), np.tile(kdec, (rows // c_, 1)), cdec, dmask)
    return tuple(jnp.asarray(t, F32) for t in tables)


def _retention(proj, proj_b, norm_g, eb, ts):
    b_, s_, _ = proj.shape
    gw = GROUP_WIDTH
    c_ = RET_CHUNK
    cos_t, sin_t, qdec, kdec, cdec, dmask = _retention_tables(s_, ts)
    col = lambda j: pl.BlockSpec((None, ts, gw), lambda b, n: (b, n, j))
    pos_spec = pl.BlockSpec((ts, gw), lambda b, n: (n, 0))
    return pl.pallas_call(
        _retention_kernel,
        grid=(b_, s_ // ts),
        in_specs=[col(R_Q), col(R_K), col(R_V), col(R_G), pos_spec, pos_spec,
                  _const_spec((ts, gw)), _const_spec((ts, gw)), _const_spec((1, gw)),
                  _const_spec((N_GROUP_HEADS // 2, c_, 2 * c_)), _const_spec((1, gw)), _const_spec((gw, gw))],
        out_specs=pl.BlockSpec((None, ts, gw), lambda b, n: (b, n, 0)),
        out_shape=jax.ShapeDtypeStruct((b_, s_, gw), BF16),
        scratch_shapes=[pltpu.VMEM((gw, gw), F32)],
        compiler_params=_params("parallel", "arbitrary"),
        name="retention",
    )(proj, proj, proj_b, proj, cos_t, sin_t, qdec, kdec, cdec, dmask, norm_g.reshape(1, gw), eb)


def _pair_blockdiag(x, lane):
    zero = jnp.zeros_like(x)
    return jnp.concatenate([jnp.where(lane < HEAD_DIM, x, zero), jnp.where(lane >= HEAD_DIM, x, zero)], axis=0)


def _stickbreak_kernel(q_ref, k_ref, v_ref, m_ref, o_ref, kbd_ref, vbd_ref):
    tq = q_ref.shape[0]
    blk = SB_BLOCK
    n_sub = tq // blk
    pw = 2 * HEAD_DIM
    n_pairs = N_GROUP_HEADS // 2
    n_blocks = k_ref.shape[0] // blk
    i = pl.program_id(1)
    scale = HEAD_DIM ** -0.5

    @pl.when(i == 0)
    def _():
        lane_kv = lax.broadcasted_iota(jnp.int32, (blk, pw), 1)

        def build(j, _):
            rows = pl.ds(pl.multiple_of(j * blk, blk), blk)
            for p in range(n_pairs):
                dst = pl.ds(pl.multiple_of((j * n_pairs + p) * (2 * blk), 2 * blk), 2 * blk)
                kbd_ref[dst, :] = _pair_blockdiag(k_ref[rows, p * pw:(p + 1) * pw], lane_kv)
                vbd_ref[dst, :] = _pair_blockdiag(v_ref[rows, p * pw:(p + 1) * pw], lane_kv)
            return 0

        lax.fori_loop(0, n_blocks, build, 0)

    row = lax.broadcasted_iota(jnp.int32, (blk, 2 * blk), 0)
    key_off = lax.broadcasted_iota(jnp.int32, (blk, 2 * blk), 1) % blk
    strictly_before = key_off < row
    m = m_ref[...]
    qs = [[q_ref[u * blk:(u + 1) * blk, p * pw:(p + 1) * pw] * scale for p in range(n_pairs)]
          for u in range(n_sub)]

    def step(first_block, carry, depth, diagonal_first, never_negative):
        chains = [(u, p, d) for u in range(n_sub) for p in range(n_pairs) for d in range(depth)]
        z2, vbd, log2_w, tot = {}, {}, {}, {}
        for u, p, d in chains:
            j = jnp.maximum(first_block + u - d, 0)
            base = pl.multiple_of((j * n_pairs + p) * (2 * blk), 2 * blk)
            vbd[u, p, d] = vbd_ref[pl.ds(base, 2 * blk), :]
            z2[u, p, d] = lax.dot_general(qs[u][p], kbd_ref[pl.ds(base, 2 * blk), :], (((1,), (1,)), ((), ())),
                                          preferred_element_type=F32) * LOG2_E
        for c in chains:
            softplus2 = jnp.maximum(z2[c], jnp.log2(1.0 + jnp.exp2(jnp.minimum(z2[c], EXP2_CLAMP))))
            if diagonal_first and c[2] == 0:
                softplus2 = jnp.where(strictly_before, softplus2, 0.0)
            hi, lo = _split2(softplus2)
            log2_w[c] = z2[c] + jnp.dot(jnp.concatenate([hi, lo], axis=1), m, preferred_element_type=F32)
            tot[c] = (jnp.sum(softplus2[:, :blk], axis=1, keepdims=True),
                      jnp.sum(softplus2[:, blk:], axis=1, keepdims=True))
        out = {}
        for u in range(n_sub):
            for p in range(n_pairs):
                acc, run0, run1 = carry[u][p]
                for d in range(depth):
                    c = (u, p, d)
                    use0, use1 = run0, run1
                    if not never_negative(u, d):
                        finished = first_block + u - d < 0
                        use0 = jnp.where(finished, -jnp.inf, run0)
                        use1 = jnp.where(finished, -jnp.inf, run1)
                    w = jnp.concatenate([jnp.exp2(log2_w[c][:, :blk] + use0), jnp.exp2(log2_w[c][:, blk:] + use1)],
                                        axis=1)
                    if diagonal_first and d == 0:
                        w = jnp.where(strictly_before, w, 0.0)
                    acc = acc + jnp.dot(w.astype(BF16), vbd[c], preferred_element_type=F32)
                    run0 = run0 - tot[c][0]
                    run1 = run1 - tot[c][1]
                out[u, p] = (acc, run0, run1)
        return tuple(tuple(out[u, p] for p in range(n_pairs)) for u in range(n_sub))

    zero_col = jnp.zeros((blk, 1), F32)
    carry = tuple(tuple((jnp.zeros((blk, pw), F32), zero_col, zero_col) for _ in range(n_pairs))
                  for _ in range(n_sub))
    carry = step(n_sub * i, carry, SB_FIRST_DEPTH, True, lambda u, d: u >= d)

    def any_weight_left(c):
        top = functools.reduce(jnp.maximum, [r for sub in c for _, run0, run1 in sub for r in (run0, run1)])
        return (jnp.max(top) > EXP2_F32_ZERO_BELOW).astype(jnp.int32)

    def sweep(state):
        t, _, c = state
        c = step(n_sub * i - SB_FIRST_DEPTH - t, c, 1, False, lambda u, d: u == n_sub - 1)
        return t + 1, any_weight_left(c), c

    n_trips = n_sub * i + n_sub - SB_FIRST_DEPTH
    _, _, carry = lax.while_loop(lambda state: (state[0] < n_trips) & (state[1] > 0), sweep,
                                 (jnp.int32(0), any_weight_left(carry), carry))
    for u in range(n_sub):
        for p in range(n_pairs):
            o_ref[u * blk:(u + 1) * blk, p * pw:(p + 1) * pw] = carry[u][p][0].astype(BF16)


def _stickbreak(proj_b, tq):
    b_, s_, _ = proj_b.shape
    gw = GROUP_WIDTH
    blk = SB_BLOCK
    idx = np.arange(2 * blk)
    same_head = (idx[:, None] // blk) == (idx[None, :] // blk)
    m = -(same_head & (idx[:, None] >= idx[None, :])).astype(np.float32)
    m = jnp.asarray(np.concatenate([m, m], axis=0), BF16)
    return pl.pallas_call(
        _stickbreak_kernel,
        grid=(b_, s_ // tq),
        in_specs=[pl.BlockSpec((None, tq, gw), lambda b, i: (b, i, S_Q)),
                  pl.BlockSpec((None, s_, gw), lambda b, i: (b, 0, S_K)),
                  pl.BlockSpec((None, s_, gw), lambda b, i: (b, 0, S_V)),
                  _const_spec((4 * blk, 2 * blk))],
        out_specs=pl.BlockSpec((None, tq, gw), lambda b, i: (b, i, 0)),
        out_shape=jax.ShapeDtypeStruct((b_, s_, gw), BF16),
        scratch_shapes=[pltpu.VMEM((2 * s_ * (N_GROUP_HEADS // 2), 2 * HEAD_DIM), BF16),
                        pltpu.VMEM((2 * s_ * (N_GROUP_HEADS // 2), 2 * HEAD_DIM), BF16)],
        compiler_params=_params("parallel", "arbitrary"),
        name="stickbreak",
    )(proj_b, proj_b, proj_b, m)


def _split3(x):
    p1 = x.astype(BF16)
    r1 = x - p1.astype(F32)
    p2 = r1.astype(BF16)
    p3 = (r1 - p2.astype(F32)).astype(BF16)
    return p1, p2, p3


def _dot3(a, parts):
    return sum(jnp.dot(a, p, preferred_element_type=F32) for p in parts)


def _hgrn_kernel(layer, q_ref, f_ref, v_ref, g_ref, lbl_ref, ng_ref, tri_ref, sel_ref, bias_ref, eb_ref, o_ref,
                 state_ref, b2_ref, kk_ref):
    ts, gw = q_ref.shape
    c = HGRN_CHUNK
    n_chunks = ts // c
    pw = 2 * HEAD_DIM
    n_pairs = N_GROUP_HEADS // 2

    @pl.when(pl.program_id(1) == 0)
    def _():
        state_ref[...] = jnp.zeros(state_ref.shape, F32)

    logits = lbl_ref[...]
    e = jnp.exp(logits - jnp.max(logits, axis=0, keepdims=True))
    lb_p = e / jnp.sum(e, axis=0, keepdims=True)
    lb = jnp.sum(lb_p[1:layer + 1, :], axis=0, keepdims=True) if layer > 0 else jnp.zeros((1, gw), F32)

    f_pre = f_ref[...]
    q = q_ref[...]
    f_gate = lb + (1.0 - lb) * jax.nn.sigmoid(f_pre)
    log_f = jnp.log(jnp.maximum(f_gate, GATE_FLOOR))
    kk = (1.0 - lb) * jax.nn.sigmoid(-f_pre)
    parts = _split3(log_f)
    b = _dot3(tri_ref[...], parts)
    b_tot = _dot3(sel_ref[...], parts)
    qe = (q * jnp.exp(b)).astype(BF16)
    kd = (kk * jnp.exp(b_tot - b)).astype(BF16)
    decay = jnp.exp(b_tot)
    b2_ref[...] = b * math.log2(math.e)
    kk_ref[...] = (b - jnp.log(kk)) * math.log2(math.e)

    eb = eb_ref[...]
    vb = v_ref[...].astype(BF16)
    same_head = eb[0:pw, 0:pw] > 0
    rows = [slice(n * c, (n + 1) * c) for n in range(n_chunks)]
    lanes = [slice(p * pw, (p + 1) * pw) for p in range(n_pairs)]
    kv_t = [[lax.dot_general(vb[r, l], kd[r, l], (((0,), (0,)), ((), ())), preferred_element_type=F32)
             for r in rows] for l in lanes]
    states = []
    for p in range(n_pairs):
        state = state_ref[p]
        entering = []
        for n in range(n_chunks):
            entering.append(state.astype(BF16))
            state = state * decay[n * c:n * c + 1, lanes[p]] + jnp.where(same_head, kv_t[p][n], 0.0)
        state_ref[p] = state
        states.append(entering)
    o_state = [jnp.concatenate([lax.dot_general(qe[rows[n], lanes[p]], states[p][n], (((1,), (1,)), ((), ())),
                                                preferred_element_type=F32) for n in range(n_chunks)], axis=0)
               for p in range(n_pairs)]

    half = c // 2
    o_intra = []
    for n in range(n_chunks):
        r0 = n * c
        blocks = []
        for s in range(c):
            lo = 0 if s < half else half
            key_s = kk_ref[r0 + s:r0 + s + 1, :]
            exponent = (b2_ref[r0 + lo:r0 + c, :] - key_s) + bias_ref[s, lo:c, :]
            blocks.append(jnp.exp2(exponent) * q_ref[r0 + lo:r0 + c, :])
        g = jnp.dot(jnp.concatenate(blocks, axis=0).astype(BF16), eb, preferred_element_type=F32)
        top = jnp.zeros((half, gw), F32)
        bottom = jnp.zeros((half, gw), F32)
        for s in range(c):
            vs = v_ref[r0 + s:r0 + s + 1, :]
            if s < half:
                top += g[s * c:s * c + half, :] * vs
                bottom += g[s * c + half:(s + 1) * c, :] * vs
            else:
                start = half * c + (s - half) * half
                bottom += g[start:start + half, :] * vs
        o_intra.append(top)
        o_intra.append(bottom)
    o_intra = jnp.concatenate(o_intra, axis=0)

    o = jnp.concatenate(o_state, axis=1) + o_intra
    ms = _head_mean(o * o, eb)
    o = o * lax.rsqrt(ms + NORM_EPS)
    o_ref[...] = (o * ng_ref[...] * jax.nn.silu(g_ref[...])).astype(BF16)


def _hgrn(proj, lb_logits, norm_g, eb, layer, ts):
    b_, s_, _ = proj.shape
    gw = GROUP_WIDTH
    c = HGRN_CHUNK
    depth = lb_logits.shape[0]
    row = np.arange(ts)
    same_chunk = (row[:, None] // c) == (row[None, :] // c)
    tri = jnp.asarray(same_chunk & (row[:, None] >= row[None, :]), BF16)
    sel = jnp.asarray(same_chunk, BF16)
    pos = np.arange(c)
    bias = np.where(pos[None, :, None] >= pos[:, None, None], 0.0, -1e30)
    bias = jnp.asarray(np.broadcast_to(bias, (c, c, gw)), F32)
    col = lambda j: pl.BlockSpec((None, ts, gw), lambda b, s: (b, s, j))
    return pl.pallas_call(
        functools.partial(_hgrn_kernel, layer),
        grid=(b_, s_ // ts),
        in_specs=[col(D_Q), col(D_F), col(D_V), col(D_G), _const_spec((depth, gw)), _const_spec((1, gw)),
                  _const_spec((ts, ts)), _const_spec((ts, ts)), _const_spec((c, c, gw)), _const_spec((gw, gw))],
        out_specs=pl.BlockSpec((None, ts, gw), lambda b, s: (b, s, 0)),
        out_shape=jax.ShapeDtypeStruct((b_, s_, gw), BF16),
        scratch_shapes=[pltpu.VMEM((N_GROUP_HEADS // 2, 2 * HEAD_DIM, 2 * HEAD_DIM), F32),
                        pltpu.VMEM((ts, gw), F32), pltpu.VMEM((ts, gw), F32)],
        compiler_params=_params("parallel", "arbitrary"),
        name="hgrn2",
    )(proj, proj, proj, proj, lb_logits, norm_g.reshape(1, gw), tri, sel, bias, eb)


def kernel(x, ln_in_g, ln_in_b, w_in, conv_w, conv_b, rg_wa, rg_ba, rg_wx, rg_bx, rg_lambda, ret_norm_g,
           hgrn_lb_logits, hgrn_norm_g, w_out, ln1_g, ln1_b, w_up, w_down, ln2_g, ln2_b):
    b_, s_, d = x.shape
    depth = w_in.shape[0]
    t = b_ * s_
    alpha = (2 * depth) ** 0.25
    proj_rows = min(PROJ_ROWS, t)
    tail_rows = min(TAIL_ROWS, t)

    head = np.arange(GROUP_WIDTH) // HEAD_DIM
    eb = jnp.asarray(head[:, None] == head[None, :], BF16)

    h = hb = None
    gw = GROUP_WIDTH
    columns = lambda w, slices: jnp.concatenate([w[:, s * gw:(s + 1) * gw] for s in slices], axis=1).astype(BF16)
    for l in range(depth):
        wf, wb = columns(w_in[l], F32_SLICES), columns(w_in[l], BF16_SLICES)
        if l == 0:
            h, pf, pb = _proj(x.reshape(t, d), wf, wb, proj_rows, ln=(ln_in_g, ln_in_b))
        else:
            pf, pb = _proj(hb, wf, wb, proj_rows)
        pf = pf.reshape(b_, s_, -1)
        pb = pb.reshape(b_, s_, -1)
        y_a = _rglru(pf, conv_w[l], conv_b[l], rg_wa[l], rg_ba[l], rg_wx[l], rg_bx[l], rg_lambda[l],
                     min(RGLRU_ROWS, s_))
        y_b = _retention(pf, pb, ret_norm_g[l], eb, min(RETENTION_ROWS, s_))
        y_c = _stickbreak(pb, min(STICKBREAK_ROWS, s_))
        y_d = _hgrn(pf, hgrn_lb_logits, hgrn_norm_g[l], eb, l, min(HGRN_ROWS, s_))
        ys = [y.reshape(t, gw) for y in (y_a, y_b, y_c, y_d)]
        h, hb = _block_tail(ys, w_out[l].astype(BF16), h, ln1_g[l], ln1_b[l], w_up[l].astype(BF16),
                            w_down[l].astype(BF16), ln2_g[l], ln2_b[l], alpha, tail_rows, FF_CHUNK)
    return h.reshape(b_, s_, d).astype(x.dtype)
```

```python
import functools
import math

import jax
import jax.numpy as jnp
import numpy as np
from jax import lax
from jax.experimental import pallas as pl
from jax.experimental.pallas import tpu as pltpu

F32 = jnp.float32
BF16 = jnp.bfloat16

HEAD_DIM = 64
N_GROUP_HEADS = 4
GROUP_WIDTH = HEAD_DIM * N_GROUP_HEADS
F32_SLICES = (0, 1, 2, 3, 5, 9, 10, 11, 12)
BF16_SLICES = (4, 6, 7, 8)
A_X, A_G, R_Q, R_K, R_G, D_Q, D_F, D_V, D_G = range(9)
R_V, S_Q, S_K, S_V = range(4)
CONV_WIDTH = 4
RG_LRU_C = 8.0
RET_CHUNK = 128
SB_BLOCK = 128
SB_FIRST_DEPTH = 3
HGRN_CHUNK = 16
ROPE_BASE = 10000.0
LN_EPS = 1e-5
NORM_EPS = 1e-6
GATE_FLOOR = 1e-30
EXP2_F32_ZERO_BELOW = -150.1
EXP2_CLAMP = 126.0
LOG2_E = math.log2(math.e)

VMEM_LIMIT_BYTES = 56 * 1024 * 1024
SUBLANES = 8
DENSE_ROW_PARTS = 2
PROJ_ROWS = 1024
TAIL_ROWS = 512
FF_CHUNK = 1024
RGLRU_ROWS = 2048
RETENTION_ROWS = 2048
STICKBREAK_ROWS = 512
HGRN_ROWS = 512


def _params(*semantics):
    return pltpu.CompilerParams(dimension_semantics=semantics, vmem_limit_bytes=VMEM_LIMIT_BYTES)


def _const_spec(shape):
    zeros = (0,) * len(shape)
    return pl.BlockSpec(shape, lambda *_: zeros)


def _layer_norm_rows(x, g, b):
    mu = jnp.mean(x, axis=-1, keepdims=True)
    xc = x - mu
    var = jnp.mean(xc * xc, axis=-1, keepdims=True)
    return xc * lax.rsqrt(var + LN_EPS) * g + b


def _split2(x):
    hi = x.astype(BF16)
    lo = (x - hi.astype(F32)).astype(BF16)
    return hi, lo


def _head_mean(x, eb):
    hi, lo = _split2(x)
    s = jnp.dot(hi, eb, preferred_element_type=F32) + jnp.dot(lo, eb, preferred_element_type=F32)
    return s * (1.0 / HEAD_DIM)


def _proj_kernel(h_ref, wf_ref, wb_ref, pf_ref, pb_ref):
    h = h_ref[...]
    pf_ref[...] = jnp.dot(h, wf_ref[...], preferred_element_type=F32)
    pb_ref[...] = jnp.dot(h, wb_ref[...], preferred_element_type=F32).astype(BF16)


def _ln_proj_kernel(x_ref, g_ref, b_ref, wf_ref, wb_ref, h_ref, pf_ref, pb_ref):
    tm = x_ref.shape[0]
    for r in range(DENSE_ROW_PARTS):
        rows = slice(r * (tm // DENSE_ROW_PARTS), (r + 1) * (tm // DENSE_ROW_PARTS))
        h = _layer_norm_rows(x_ref[rows, :], g_ref[...], b_ref[...])
        h_ref[rows, :] = h
        hb = h.astype(BF16)
        pf_ref[rows, :] = jnp.dot(hb, wf_ref[...], preferred_element_type=F32)
        pb_ref[rows, :] = jnp.dot(hb, wb_ref[...], preferred_element_type=F32).astype(BF16)


def _proj(h, wf, wb, tm, ln=None):
    t, d = h.shape
    nf, nb = wf.shape[1], wb.shape[1]
    resident = lambda shape: pl.BlockSpec(shape, lambda i: (0, 0), pipeline_mode=pl.Buffered(1))
    row = lambda n: pl.BlockSpec((tm, n), lambda i: (i, 0))
    proj_shapes = [jax.ShapeDtypeStruct((t, nf), F32), jax.ShapeDtypeStruct((t, nb), BF16)]
    if ln is None:
        return pl.pallas_call(
            _proj_kernel,
            grid=(t // tm,),
            in_specs=[row(d), resident((d, nf)), resident((d, nb))],
            out_specs=[row(nf), row(nb)],
            out_shape=proj_shapes,
            compiler_params=_params("parallel"),
            name="in_proj",
        )(h, wf, wb)
    g, b = ln
    return pl.pallas_call(
        _ln_proj_kernel,
        grid=(t // tm,),
        in_specs=[row(d), _const_spec((1, d)), _const_spec((1, d)), resident((d, nf)), resident((d, nb))],
        out_specs=[row(d), row(nf), row(nb)],
        out_shape=[jax.ShapeDtypeStruct((t, d), F32)] + proj_shapes,
        compiler_params=_params("parallel"),
        name="ln_in_proj",
    )(h, g.reshape(1, d), b.reshape(1, d), wf, wb)


def _block_tail_kernel(alpha, ff_chunk, ya_ref, yb_ref, yc_ref, yd_ref, wo_ref, h_ref, g1_ref, b1_ref,
                       wu_ref, wd_ref, g2_ref, b2_ref, o_ref, ob_ref):
    gw = GROUP_WIDTH
    tm = h_ref.shape[0]
    parts = [slice(r * (tm // DENSE_ROW_PARTS), (r + 1) * (tm // DENSE_ROW_PARTS)) for r in range(DENSE_ROW_PARTS)]
    y_refs = (ya_ref, yb_ref, yc_ref, yd_ref)
    mix = [sum(jnp.dot(y_ref[rows, :], wo_ref[s * gw:(s + 1) * gw, :], preferred_element_type=F32)
               for s, y_ref in enumerate(y_refs)) for rows in parts]
    h1 = [_layer_norm_rows(alpha * h_ref[rows, :] + mix_r, g1_ref[...], b1_ref[...]) for rows, mix_r in zip(parts, mix)]
    hb = [h.astype(BF16) for h in h1]
    d_ff = wu_ref.shape[1]
    acc = [jnp.zeros(h.shape, F32) for h in h1]
    for c in range(d_ff // ff_chunk):
        cols = slice(c * ff_chunk, (c + 1) * ff_chunk)
        for r in range(DENSE_ROW_PARTS):
            u = jnp.dot(hb[r], wu_ref[:, cols], preferred_element_type=F32)
            u = jnp.square(jnp.maximum(u, 0.0)).astype(BF16)
            acc[r] += jnp.dot(u, wd_ref[cols, :], preferred_element_type=F32)
    for rows, h1_r, acc_r in zip(parts, h1, acc):
        h2 = _layer_norm_rows(alpha * h1_r + acc_r, g2_ref[...], b2_ref[...])
        o_ref[rows, :] = h2
        ob_ref[rows, :] = h2.astype(BF16)


def _block_tail(ys, wo, h, g1, b1, wu, wd, g2, b2, alpha, tm, ff_chunk):
    t, d = h.shape
    d_ff = wu.shape[1]
    gw = GROUP_WIDTH
    y_spec = pl.BlockSpec((tm, gw), lambda i: (i, 0))
    row_spec = pl.BlockSpec((tm, d), lambda i: (i, 0))
    resident = lambda shape: pl.BlockSpec(shape, lambda i: (0, 0), pipeline_mode=pl.Buffered(1))
    vec = _const_spec((1, d))
    return pl.pallas_call(
        functools.partial(_block_tail_kernel, alpha, ff_chunk),
        grid=(t // tm,),
        in_specs=[y_spec, y_spec, y_spec, y_spec, resident((d, d)), row_spec, vec, vec,
                  resident((d, d_ff)), resident((d_ff, d)), vec, vec],
        out_specs=[row_spec, row_spec],
        out_shape=[jax.ShapeDtypeStruct((t, d), F32), jax.ShapeDtypeStruct((t, d), BF16)],
        compiler_params=_params("parallel"),
        name="out_proj_mlp",
    )(*ys, wo, h, g1.reshape(1, d), b1.reshape(1, d), wu, wd, g2.reshape(1, d), b2.reshape(1, d))


def _rglru_kernel(xa_ref, ga_ref, cw_ref, cb_ref, wg_ref, bg_ref, lam_ref, o_ref, ext_ref, hprev_ref):
    ts, gw = xa_ref.shape
    si = pl.program_id(1)

    pad = SUBLANES

    @pl.when(si == 0)
    def _():
        ext_ref[0:pad, :] = jnp.zeros((pad, gw), F32)
        hprev_ref[...] = jnp.zeros((SUBLANES, gw), F32)

    @pl.when(si > 0)
    def _():
        ext_ref[0:pad, :] = ext_ref[ts:ts + pad, :]

    x = xa_ref[...]
    ext_ref[pad:ts + pad, :] = x
    cw = cw_ref[...]
    last = CONV_WIDTH - 1
    xc = x * cw[last:last + 1, :] + cb_ref[...]
    for k in range(1, CONV_WIDTH):
        xc += ext_ref[pad - k:pad - k + ts, :] * cw[last - k:last - k + 1, :]

    gates = jnp.dot(xc.astype(BF16), wg_ref[...], preferred_element_type=F32) + bg_ref[...]
    r = jax.nn.sigmoid(gates[:, :gw])
    i = jax.nn.sigmoid(gates[:, gw:])
    lam = lam_ref[...]
    log_sig_lam = -(jnp.maximum(-lam, 0.0) + jnp.log1p(jnp.exp(-jnp.abs(lam))))
    log_a = RG_LRU_C * r * log_sig_lam
    a = jnp.exp(log_a)
    th = jnp.tanh(log_a)
    one_minus_a2 = -2.0 * th / (1.0 - th)
    u = jnp.sqrt(jnp.maximum(one_minus_a2, 0.0)) * (i * xc)

    sub = lax.broadcasted_iota(jnp.int32, (SUBLANES, gw), 0)
    h_in = hprev_ref[SUBLANES - 1:SUBLANES, :]
    tiles = []
    for t0 in range(0, ts, SUBLANES):
        a_t = a[t0:t0 + SUBLANES, :]
        u_t = u[t0:t0 + SUBLANES, :]
        k = 1
        while k < SUBLANES:
            live = sub >= k
            a_sh = jnp.where(live, pltpu.roll(a_t, k, 0), 1.0)
            u_sh = jnp.where(live, pltpu.roll(u_t, k, 0), 0.0)
            u_t = a_t * u_sh + u_t
            a_t = a_t * a_sh
            k *= 2
        h_t = u_t + a_t * h_in
        h_in = h_t[SUBLANES - 1:SUBLANES, :]
        tiles.append(h_t)
    h = jnp.concatenate(tiles, axis=0)
    hprev_ref[...] = tiles[-1]
    o_ref[...] = (jax.nn.gelu(ga_ref[...], approximate=True) * h).astype(BF16)


def _blockdiag_heads(w):
    h, di, dj = w.shape
    eye = jnp.eye(h, dtype=w.dtype)
    return (eye[:, None, :, None] * w[:, :, None, :]).reshape(h * di, h * dj)


def _rglru(proj, conv_w, conv_b, wa, ba, wx, bx, lam, ts):
    b_, s_, _ = proj.shape
    gw = GROUP_WIDTH
    wg = jnp.concatenate([_blockdiag_heads(wa), _blockdiag_heads(wx)], axis=1).astype(BF16)
    bg = jnp.concatenate([ba.reshape(1, gw), bx.reshape(1, gw)], axis=1)
    col = lambda j: pl.BlockSpec((None, ts, gw), lambda b, s: (b, s, j))
    return pl.pallas_call(
        _rglru_kernel,
        grid=(b_, s_ // ts),
        in_specs=[col(A_X), col(A_G), _const_spec((CONV_WIDTH, gw)), _const_spec((1, gw)),
                  _const_spec((gw, 2 * gw)), _const_spec((1, 2 * gw)), _const_spec((1, gw))],
        out_specs=pl.BlockSpec((None, ts, gw), lambda b, s: (b, s, 0)),
        out_shape=jax.ShapeDtypeStruct((b_, s_, gw), BF16),
        scratch_shapes=[pltpu.VMEM((ts + SUBLANES, gw), F32), pltpu.VMEM((SUBLANES, gw), F32)],
        compiler_params=_params("parallel", "arbitrary"),
        name="rglru",
    )(proj, proj, conv_w, conv_b.reshape(1, gw), wg, bg, lam.reshape(1, gw))


def _retention_kernel(q_ref, k_ref, v_ref, g_ref, cos_ref, sin_ref, qdec_ref, kdec_ref, cdec_ref,
                      dmask_ref, ng_ref, eb_ref, o_ref, state_ref):
    ts, gw = q_ref.shape
    c = RET_CHUNK
    pw = 2 * HEAD_DIM
    n_pairs = N_GROUP_HEADS // 2

    @pl.when(pl.program_id(1) == 0)
    def _():
        state_ref[...] = jnp.zeros((gw, gw), F32)

    lane = lax.broadcasted_iota(jnp.int32, (ts, gw), 1)
    first_half = (lane % HEAD_DIM) < (HEAD_DIM // 2)
    cos = cos_ref[...]
    sin = sin_ref[...]

    def rotary(t):
        partner = jnp.where(first_half, pltpu.roll(t, gw - HEAD_DIM // 2, 1), pltpu.roll(t, HEAD_DIM // 2, 1))
        return t * cos + partner * sin

    q = rotary(q_ref[...])
    k = rotary(k_ref[...]) * (HEAD_DIM ** -0.5)
    qb = q.astype(BF16)
    kb = k.astype(BF16)
    q_dec = (q * qdec_ref[...]).astype(BF16)
    k_dec = (k * kdec_ref[...]).astype(BF16)
    vb = v_ref[...]
    eb = eb_ref[...]
    same_head = eb > 0
    lane_kv = lax.broadcasted_iota(jnp.int32, (c, pw), 1)
    chunks = [slice(n * c, (n + 1) * c) for n in range(ts // c)]

    kv = [lax.dot_general(k_dec[r, :], vb[r, :], (((0,), (0,)), ((), ())), preferred_element_type=F32)
          for r in chunks]
    state = state_ref[...]
    entering = []
    for kv_n in kv:
        entering.append(state.astype(BF16))
        state = state * cdec_ref[...] + jnp.where(same_head, kv_n, 0.0)
    state_ref[...] = state

    outs = []
    for r, state_n in zip(chunks, entering):
        inter = jnp.dot(q_dec[r, :], state_n, preferred_element_type=F32)
        intra = []
        for p in range(n_pairs):
            lanes = slice(p * pw, (p + 1) * pw)
            kbd = _pair_blockdiag(kb[r, lanes], lane_kv)
            vbd = _pair_blockdiag(vb[r, lanes], lane_kv)
            scores = lax.dot_general(qb[r, lanes], kbd, (((1,), (1,)), ((), ())), preferred_element_type=F32)
            intra.append(jnp.dot((scores * dmask_ref[p]).astype(BF16), vbd, preferred_element_type=F32))
        outs.append(inter + jnp.concatenate(intra, axis=1))
    o = jnp.concatenate(outs, axis=0)

    mu = _head_mean(o, eb)
    oc = o - mu
    var = _head_mean(oc * oc, eb)
    o = oc * lax.rsqrt(var + NORM_EPS) * ng_ref[...]
    o_ref[...] = (jax.nn.silu(g_ref[...]) * o).astype(BF16)


def _retention_tables(s_, rows):
    inv_freq = ROPE_BASE ** (-np.arange(0, HEAD_DIM, 2, dtype=np.float64) / HEAD_DIM)
    ang = np.arange(s_, dtype=np.float64)[:, None] * inv_freq[None, :]
    cos, sin = np.cos(ang), np.sin(ang)
    cos_t = np.tile(np.concatenate([cos, cos], axis=-1), (1, N_GROUP_HEADS))
    sin_t = np.tile(np.concatenate([-sin, sin], axis=-1), (1, N_GROUP_HEADS))
    c_ = RET_CHUNK
    log_gamma = np.log1p(-np.exp2(-5.0 - np.arange(N_GROUP_HEADS, dtype=np.float64)))
    pos = np.arange(c_, dtype=np.float64)
    diff = pos[:, None] - pos[None, :]
    dmask = np.where(diff >= 0, np.exp(log_gamma[:, None, None] * np.maximum(diff, 0.0)), 0.0)
    lanes = lambda per_head: np.repeat(per_head, HEAD_DIM, axis=-1)
    kdec = lanes(np.exp(log_gamma[None, :] * (c_ - 1.0 - pos)[:, None]))
    qdec = lanes(np.exp(log_gamma[None, :] * (pos + 1.0)[:, None]))
    cdec = lanes(np.exp(log_gamma * c_)[None, :])
    dmask = dmask.reshape(N_GROUP_HEADS // 2, 2, c_, c_).transpose(0, 2, 1, 3).reshape(N_GROUP_HEADS // 2, c_, 2 * c_)
    tables = (cos_t, sin_t, np.tile(qdec, (rows // c_, 1)), np.tile(kdec, (rows // c_, 1)), cdec, dmask)
    return tuple(jnp.asarray(t, F32) for t in tables)


def _retention(proj, proj_b, norm_g, eb, ts):
    b_, s_, _ = proj.shape
    gw = GROUP_WIDTH
    c_ = RET_CHUNK
    cos_t, sin_t, qdec, kdec, cdec, dmask = _retention_tables(s_, ts)
    col = lambda j: pl.BlockSpec((None, ts, gw), lambda b, n: (b, n, j))
    pos_spec = pl.BlockSpec((ts, gw), lambda b, n: (n, 0))
    return pl.pallas_call(
        _retention_kernel,
        grid=(b_, s_ // ts),
        in_specs=[col(R_Q), col(R_K), col(R_V), col(R_G), pos_spec, pos_spec,
                  _const_spec((ts, gw)), _const_spec((ts, gw)), _const_spec((1, gw)),
                  _const_spec((N_GROUP_HEADS // 2, c_, 2 * c_)), _const_spec((1, gw)), _const_spec((gw, gw))],
        out_specs=pl.BlockSpec((None, ts, gw), lambda b, n: (b, n, 0)),
        out_shape=jax.ShapeDtypeStruct((b_, s_, gw), BF16),
        scratch_shapes=[pltpu.VMEM((gw, gw), F32)],
        compiler_params=_params("parallel", "arbitrary"),
        name="retention",
    )(proj, proj, proj_b, proj, cos_t, sin_t, qdec, kdec, cdec, dmask, norm_g.reshape(1, gw), eb)


def _pair_blockdiag(x, lane):
    zero = jnp.zeros_like(x)
    return jnp.concatenate([jnp.where(lane < HEAD_DIM, x, zero), jnp.where(lane >= HEAD_DIM, x, zero)], axis=0)


def _stickbreak_kernel(q_ref, k_ref, v_ref, m_ref, o_ref, kbd_ref, vbd_ref):
    tq = q_ref.shape[0]
    blk = SB_BLOCK
    n_sub = tq // blk
    pw = 2 * HEAD_DIM
    n_pairs = N_GROUP_HEADS // 2
    n_blocks = k_ref.shape[0] // blk
    i = pl.program_id(1)
    scale = HEAD_DIM ** -0.5

    @pl.when(i == 0)
    def _():
        lane_kv = lax.broadcasted_iota(jnp.int32, (blk, pw), 1)

        def build(j, _):
            rows = pl.ds(pl.multiple_of(j * blk, blk), blk)
            for p in range(n_pairs):
                dst = pl.ds(pl.multiple_of((j * n_pairs + p) * (2 * blk), 2 * blk), 2 * blk)
                kbd_ref[dst, :] = _pair_blockdiag(k_ref[rows, p * pw:(p + 1) * pw], lane_kv)
                vbd_ref[dst, :] = _pair_blockdiag(v_ref[rows, p * pw:(p + 1) * pw], lane_kv)
            return 0

        lax.fori_loop(0, n_blocks, build, 0)

    row = lax.broadcasted_iota(jnp.int32, (blk, 2 * blk), 0)
    key_off = lax.broadcasted_iota(jnp.int32, (blk, 2 * blk), 1) % blk
    strictly_before = key_off < row
    m = m_ref[...]
    qs = [[q_ref[u * blk:(u + 1) * blk, p * pw:(p + 1) * pw] * scale for p in range(n_pairs)]
          for u in range(n_sub)]

    def step(first_block, carry, depth, diagonal_first, never_negative):
        chains = [(u, p, d) for u in range(n_sub) for p in range(n_pairs) for d in range(depth)]
        z2, vbd, log2_w, tot = {}, {}, {}, {}
        for u, p, d in chains:
            j = jnp.maximum(first_block + u - d, 0)
            base = pl.multiple_of((j * n_pairs + p) * (2 * blk), 2 * blk)
            vbd[u, p, d] = vbd_ref[pl.ds(base, 2 * blk), :]
            z2[u, p, d] = lax.dot_general(qs[u][p], kbd_ref[pl.ds(base, 2 * blk), :], (((1,), (1,)), ((), ())),
                                          preferred_element_type=F32) * LOG2_E
        for c in chains:
            softplus2 = jnp.maximum(z2[c], jnp.log2(1.0 + jnp.exp2(jnp.minimum(z2[c], EXP2_CLAMP))))
            if diagonal_first and c[2] == 0:
                softplus2 = jnp.where(strictly_before, softplus2, 0.0)
            hi, lo = _split2(softplus2)
            log2_w[c] = z2[c] + jnp.dot(jnp.concatenate([hi, lo], axis=1), m, preferred_element_type=F32)
            tot[c] = (jnp.sum(softplus2[:, :blk], axis=1, keepdims=True),
                      jnp.sum(softplus2[:, blk:], axis=1, keepdims=True))
        out = {}
        for u in range(n_sub):
            for p in range(n_pairs):
                acc, run0, run1 = carry[u][p]
                for d in range(depth):
                    c = (u, p, d)
                    use0, use1 = run0, run1
                    if not never_negative(u, d):
                        finished = first_block + u - d < 0
                        use0 = jnp.where(finished, -jnp.inf, run0)
                        use1 = jnp.where(finished, -jnp.inf, run1)
                    w = jnp.concatenate([jnp.exp2(log2_w[c][:, :blk] + use0), jnp.exp2(log2_w[c][:, blk:] + use1)],
                                        axis=1)
                    if diagonal_first and d == 0:
                        w = jnp.where(strictly_before, w, 0.0)
                    acc = acc + jnp.dot(w.astype(BF16), vbd[c], preferred_element_type=F32)
                    run0 = run0 - tot[c][0]
                    run1 = run1 - tot[c][1]
                out[u, p] = (acc, run0, run1)
        return tuple(tuple(out[u, p] for p in range(n_pairs)) for u in range(n_sub))

    zero_col = jnp.zeros((blk, 1), F32)
    carry = tuple(tuple((jnp.zeros((blk, pw), F32), zero_col, zero_col) for _ in range(n_pairs))
                  for _ in range(n_sub))
    carry = step(n_sub * i, carry, SB_FIRST_DEPTH, True, lambda u, d: u >= d)

    def any_weight_left(c):
        top = functools.reduce(jnp.maximum, [r for sub in c for _, run0, run1 in sub for r in (run0, run1)])
        return (jnp.max(top) > EXP2_F32_ZERO_BELOW).astype(jnp.int32)

    def sweep(state):
        t, _, c = state
        c = step(n_sub * i - SB_FIRST_DEPTH - t, c, 1, False, lambda u, d: u == n_sub - 1)
        return t + 1, any_weight_left(c), c

    n_trips = n_sub * i + n_sub - SB_FIRST_DEPTH
    _, _, carry = lax.while_loop(lambda state: (state[0] < n_trips) & (state[1] > 0), sweep,
                                 (jnp.int32(0), any_weight_left(carry), carry))
    for u in range(n_sub):
        for p in range(n_pairs):
            o_ref[u * blk:(u + 1) * blk, p * pw:(p + 1) * pw] = carry[u][p][0].astype(BF16)


def _stickbreak(proj_b, tq):
    b_, s_, _ = proj_b.shape
    gw = GROUP_WIDTH
    blk = SB_BLOCK
    idx = np.arange(2 * blk)
    same_head = (idx[:, None] // blk) == (idx[None, :] // blk)
    m = -(same_head & (idx[:, None] >= idx[None, :])).astype(np.float32)
    m = jnp.asarray(np.concatenate([m, m], axis=0), BF16)
    return pl.pallas_call(
        _stickbreak_kernel,
        grid=(b_, s_ // tq),
        in_specs=[pl.BlockSpec((None, tq, gw), lambda b, i: (b, i, S_Q)),
                  pl.BlockSpec((None, s_, gw), lambda b, i: (b, 0, S_K)),
                  pl.BlockSpec((None, s_, gw), lambda b, i: (b, 0, S_V)),
                  _const_spec((4 * blk, 2 * blk))],
        out_specs=pl.BlockSpec((None, tq, gw), lambda b, i: (b, i, 0)),
        out_shape=jax.ShapeDtypeStruct((b_, s_, gw), BF16),
        scratch_shapes=[pltpu.VMEM((2 * s_ * (N_GROUP_HEADS // 2), 2 * HEAD_DIM), BF16),
                        pltpu.VMEM((2 * s_ * (N_GROUP_HEADS // 2), 2 * HEAD_DIM), BF16)],
        compiler_params=_params("parallel", "arbitrary"),
        name="stickbreak",
    )(proj_b, proj_b, proj_b, m)


def _split3(x):
    p1 = x.astype(BF16)
    r1 = x - p1.astype(F32)
    p2 = r1.astype(BF16)
    p3 = (r1 - p2.astype(F32)).astype(BF16)
    return p1, p2, p3


def _dot3(a, parts):
    return sum(jnp.dot(a, p, preferred_element_type=F32) for p in parts)


def _hgrn_kernel(layer, q_ref, f_ref, v_ref, g_ref, lbl_ref, ng_ref, tri_ref, sel_ref, bias_ref, eb_ref, o_ref,
                 state_ref, b2_ref, key_ref):
    ts, gw = q_ref.shape
    c = HGRN_CHUNK
    n_chunks = ts // c
    pw = 2 * HEAD_DIM
    n_pairs = N_GROUP_HEADS // 2

    @pl.when(pl.program_id(1) == 0)
    def _():
        state_ref[...] = jnp.zeros(state_ref.shape, F32)

    logits = lbl_ref[...]
    e = jnp.exp(logits - jnp.max(logits, axis=0, keepdims=True))
    lb_p = e / jnp.sum(e, axis=0, keepdims=True)
    lb = jnp.sum(lb_p[1:layer + 1, :], axis=0, keepdims=True) if layer > 0 else jnp.zeros((1, gw), F32)

    f_pre = f_ref[...]
    q = q_ref[...]
    f_gate = lb + (1.0 - lb) * jax.nn.sigmoid(f_pre)
    log_f = jnp.log(jnp.maximum(f_gate, GATE_FLOOR))
    kk = (1.0 - lb) * jax.nn.sigmoid(-f_pre)
    parts = _split3(log_f)
    b = _dot3(tri_ref[...], parts)
    b_tot = _dot3(sel_ref[...], parts)
    qe = (q * jnp.exp(b)).astype(BF16)
    kd = (kk * jnp.exp(b_tot - b)).astype(BF16)
    decay = jnp.exp(b_tot)
    b2_ref[...] = b * math.log2(math.e)
    key_ref[...] = (b - jnp.log(kk)) * math.log2(math.e)

    eb = eb_ref[...]
    vb = v_ref[...].astype(BF16)
    same_head = eb[0:pw, 0:pw] > 0
    rows = [slice(n * c, (n + 1) * c) for n in range(n_chunks)]
    lanes = [slice(p * pw, (p + 1) * pw) for p in range(n_pairs)]
    kv_t = [[lax.dot_general(vb[r, l], kd[r, l], (((0,), (0,)), ((), ())), preferred_element_type=F32)
             for r in rows] for l in lanes]
    states = []
    for p in range(n_pairs):
        state = state_ref[p]
        entering = []
        for n in range(n_chunks):
            entering.append(state.astype(BF16))
            state = state * decay[n * c:n * c + 1, lanes[p]] + jnp.where(same_head, kv_t[p][n], 0.0)
        state_ref[p] = state
        states.append(entering)
    o_state = [jnp.concatenate([lax.dot_general(qe[rows[n], lanes[p]], states[p][n], (((1,), (1,)), ((), ())),
                                                preferred_element_type=F32) for n in range(n_chunks)], axis=0)
               for p in range(n_pairs)]

    half = c // 2
    o_intra = []
    for n in range(n_chunks):
        r0 = n * c
        blocks = []
        for s in range(c):
            lo = 0 if s < half else half
            key_s = key_ref[r0 + s:r0 + s + 1, :]
            exponent = (b2_ref[r0 + lo:r0 + c, :] - key_s) + bias_ref[s, lo:c, :]
            blocks.append(jnp.exp2(exponent) * q_ref[r0 + lo:r0 + c, :])
        g = jnp.dot(jnp.concatenate(blocks, axis=0).astype(BF16), eb, preferred_element_type=F32)
        top = jnp.zeros((half, gw), F32)
        bottom = jnp.zeros((half, gw), F32)
        for s in range(c):
            vs = v_ref[r0 + s:r0 + s + 1, :]
            if s < half:
                top += g[s * c:s * c + half, :] * vs
                bottom += g[s * c + half:(s + 1) * c, :] * vs
            else:
                start = half * c + (s - half) * half
                bottom += g[start:start + half, :] * vs
        o_intra.append(top)
        o_intra.append(bottom)
    o_intra = jnp.concatenate(o_intra, axis=0)

    o = jnp.concatenate(o_state, axis=1) + o_intra
    ms = _head_mean(o * o, eb)
    o = o * lax.rsqrt(ms + NORM_EPS)
    o_ref[...] = (o * ng_ref[...] * jax.nn.silu(g_ref[...])).astype(BF16)


def _hgrn(proj, lb_logits, norm_g, eb, layer, ts):
    b_, s_, _ = proj.shape
    gw = GROUP_WIDTH
    c = HGRN_CHUNK
    depth = lb_logits.shape[0]
    row = np.arange(ts)
    same_chunk = (row[:, None] // c) == (row[None, :] // c)
    tri = jnp.asarray(same_chunk & (row[:, None] >= row[None, :]), BF16)
    sel = jnp.asarray(same_chunk, BF16)
    pos = np.arange(c)
    bias = np.where(pos[None, :, None] >= pos[:, None, None], 0.0, -1e30)
    bias = jnp.asarray(np.broadcast_to(bias, (c, c, gw)), F32)
    col = lambda j: pl.BlockSpec((None, ts, gw), lambda b, s: (b, s, j))
    return pl.pallas_call(
        functools.partial(_hgrn_kernel, layer),
        grid=(b_, s_ // ts),
        in_specs=[col(D_Q), col(D_F), col(D_V), col(D_G), _const_spec((depth, gw)), _const_spec((1, gw)),
                  _const_spec((ts, ts)), _const_spec((ts, ts)), _const_spec((c, c, gw)), _const_spec((gw, gw))],
        out_specs=pl.BlockSpec((None, ts, gw), lambda b, s: (b, s, 0)),
        out_shape=jax.ShapeDtypeStruct((b_, s_, gw), BF16),
        scratch_shapes=[pltpu.VMEM((N_GROUP_HEADS // 2, 2 * HEAD_DIM, 2 * HEAD_DIM), F32),
                        pltpu.VMEM((ts, gw), F32), pltpu.VMEM((ts, gw), F32)],
        compiler_params=_params("parallel", "arbitrary"),
        name="hgrn2",
    )(proj, proj, proj, proj, lb_logits, norm_g.reshape(1, gw), tri, sel, bias, eb)


def kernel(x, ln_in_g, ln_in_b, w_in, conv_w, conv_b, rg_wa, rg_ba, rg_wx, rg_bx, rg_lambda, ret_norm_g,
           hgrn_lb_logits, hgrn_norm_g, w_out, ln1_g, ln1_b, w_up, w_down, ln2_g, ln2_b):
    b_, s_, d = x.shape
    depth = w_in.shape[0]
    t = b_ * s_
    alpha = (2 * depth) ** 0.25
    proj_rows = min(PROJ_ROWS, t)
    tail_rows = min(TAIL_ROWS, t)

    head = np.arange(GROUP_WIDTH) // HEAD_DIM
    eb = jnp.asarray(head[:, None] == head[None, :], BF16)

    h = hb = None
    gw = GROUP_WIDTH
    columns = lambda w, slices: jnp.concatenate([w[:, s * gw:(s + 1) * gw] for s in slices], axis=1).astype(BF16)
    for l in range(depth):
        wf, wb = columns(w_in[l], F32_SLICES), columns(w_in[l], BF16_SLICES)
        if l == 0:
            h, pf, pb = _proj(x.reshape(t, d), wf, wb, proj_rows, ln=(ln_in_g, ln_in_b))
        else:
            pf, pb = _proj(hb, wf, wb, proj_rows)
        pf = pf.reshape(b_, s_, -1)
        pb = pb.reshape(b_, s_, -1)
        y_a = _rglru(pf, conv_w[l], conv_b[l], rg_wa[l], rg_ba[l], rg_wx[l], rg_bx[l], rg_lambda[l],
                     min(RGLRU_ROWS, s_))
        y_b = _retention(pf, pb, ret_norm_g[l], eb, min(RETENTION_ROWS, s_))
        y_c = _stickbreak(pb, min(STICKBREAK_ROWS, s_))
        y_d = _hgrn(pf, hgrn_lb_logits, hgrn_norm_g[l], eb, l, min(HGRN_ROWS, s_))
        ys = [y.reshape(t, gw) for y in (y_a, y_b, y_c, y_d)]
        h, hb = _block_tail(ys, w_out[l].astype(BF16), h, ln1_g[l], ln1_b[l], w_up[l].astype(BF16),
                            w_down[l].astype(BF16), ln2_g[l], ln2_b[l], alpha, tail_rows, FF_CHUNK)
    return h.reshape(b_, s_, d).astype(x.dtype)
```

```python
import functools
import math

import jax
import jax.numpy as jnp
import numpy as np
from jax import lax
from jax.experimental import pallas as pl
from jax.experimental.pallas import tpu as pltpu

F32 = jnp.float32
BF16 = jnp.bfloat16

HEAD_DIM = 64
N_GROUP_HEADS = 4
GROUP_WIDTH = HEAD_DIM * N_GROUP_HEADS
F32_SLICES = (0, 1, 2, 3, 5, 9, 10, 11, 12)
BF16_SLICES = (4, 6, 7, 8)
A_X, A_G, R_Q, R_K, R_G, D_Q, D_F, D_V, D_G = range(9)
R_V, S_Q, S_K, S_V = range(4)
CONV_WIDTH = 4
RG_LRU_C = 8.0
RET_CHUNK = 128
SB_BLOCK = 128
SB_FIRST_DEPTH = 3
HGRN_CHUNK = 16
ROPE_BASE = 10000.0
LN_EPS = 1e-5
NORM_EPS = 1e-6
GATE_FLOOR = 1e-30
EXP2_F32_ZERO_BELOW = -150.1
EXP2_CLAMP = 126.0
LOG2_E = math.log2(math.e)

VMEM_LIMIT_BYTES = 56 * 1024 * 1024
SUBLANES = 8
DENSE_ROW_PARTS = 2
PROJ_ROWS = 1024
TAIL_ROWS = 512
FF_CHUNK = 1024
RGLRU_ROWS = 2048
RETENTION_ROWS = 2048
STICKBREAK_ROWS = 1024
HGRN_ROWS = 512


def _params(*semantics):
    return pltpu.CompilerParams(dimension_semantics=semantics, vmem_limit_bytes=VMEM_LIMIT_BYTES)


def _const_spec(shape):
    zeros = (0,) * len(shape)
    return pl.BlockSpec(shape, lambda *_: zeros)


def _layer_norm_rows(x, g, b):
    mu = jnp.mean(x, axis=-1, keepdims=True)
    xc = x - mu
    var = jnp.mean(xc * xc, axis=-1, keepdims=True)
    return xc * lax.rsqrt(var + LN_EPS) * g + b


def _split2(x):
    hi = x.astype(BF16)
    lo = (x - hi.astype(F32)).astype(BF16)
    return hi, lo


def _head_mean(x, eb):
    hi, lo = _split2(x)
    s = jnp.dot(hi, eb, preferred_element_type=F32) + jnp.dot(lo, eb, preferred_element_type=F32)
    return s * (1.0 / HEAD_DIM)


def _proj_kernel(h_ref, wf_ref, wb_ref, pf_ref, pb_ref):
    h = h_ref[...]
    pf_ref[...] = jnp.dot(h, wf_ref[...], preferred_element_type=F32)
    pb_ref[...] = jnp.dot(h, wb_ref[...], preferred_element_type=F32).astype(BF16)


def _ln_proj_kernel(x_ref, g_ref, b_ref, wf_ref, wb_ref, h_ref, pf_ref, pb_ref):
    tm = x_ref.shape[0]
    for r in range(DENSE_ROW_PARTS):
        rows = slice(r * (tm // DENSE_ROW_PARTS), (r + 1) * (tm // DENSE_ROW_PARTS))
        h = _layer_norm_rows(x_ref[rows, :], g_ref[...], b_ref[...])
        h_ref[rows, :] = h
        hb = h.astype(BF16)
        pf_ref[rows, :] = jnp.dot(hb, wf_ref[...], preferred_element_type=F32)
        pb_ref[rows, :] = jnp.dot(hb, wb_ref[...], preferred_element_type=F32).astype(BF16)


def _proj(h, wf, wb, tm, ln=None):
    t, d = h.shape
    nf, nb = wf.shape[1], wb.shape[1]
    resident = lambda shape: pl.BlockSpec(shape, lambda i: (0, 0), pipeline_mode=pl.Buffered(1))
    row = lambda n: pl.BlockSpec((tm, n), lambda i: (i, 0))
    proj_shapes = [jax.ShapeDtypeStruct((t, nf), F32), jax.ShapeDtypeStruct((t, nb), BF16)]
    if ln is None:
        return pl.pallas_call(
            _proj_kernel,
            grid=(t // tm,),
            in_specs=[row(d), resident((d, nf)), resident((d, nb))],
            out_specs=[row(nf), row(nb)],
            out_shape=proj_shapes,
            compiler_params=_params("parallel"),
            name="in_proj",
        )(h, wf, wb)
    g, b = ln
    return pl.pallas_call(
        _ln_proj_kernel,
        grid=(t // tm,),
        in_specs=[row(d), _const_spec((1, d)), _const_spec((1, d)), resident((d, nf)), resident((d, nb))],
        out_specs=[row(d), row(nf), row(nb)],
        out_shape=[jax.ShapeDtypeStruct((t, d), F32)] + proj_shapes,
        compiler_params=_params("parallel"),
        name="ln_in_proj",
    )(h, g.reshape(1, d), b.reshape(1, d), wf, wb)


def _block_tail_kernel(alpha, ff_chunk, ya_ref, yb_ref, yc_ref, yd_ref, wo_ref, h_ref, g1_ref, b1_ref,
                       wu_ref, wd_ref, g2_ref, b2_ref, o_ref, ob_ref):
    tm = h_ref.shape[0]
    parts = [slice(r * (tm // DENSE_ROW_PARTS), (r + 1) * (tm // DENSE_ROW_PARTS)) for r in range(DENSE_ROW_PARTS)]
    y_refs = (ya_ref, yb_ref, yc_ref, yd_ref)
    mix = [jnp.dot(jnp.concatenate([y_ref[rows, :] for y_ref in y_refs], axis=1), wo_ref[...],
                   preferred_element_type=F32) for rows in parts]
    h1 = [_layer_norm_rows(alpha * h_ref[rows, :] + mix_r, g1_ref[...], b1_ref[...]) for rows, mix_r in zip(parts, mix)]
    hb = [h.astype(BF16) for h in h1]
    d_ff = wu_ref.shape[1]
    acc = [jnp.zeros(h.shape, F32) for h in h1]
    for c in range(d_ff // ff_chunk):
        cols = slice(c * ff_chunk, (c + 1) * ff_chunk)
        for r in range(DENSE_ROW_PARTS):
            u = jnp.dot(hb[r], wu_ref[:, cols], preferred_element_type=F32)
            u = jnp.square(jnp.maximum(u, 0.0)).astype(BF16)
            acc[r] += jnp.dot(u, wd_ref[cols, :], preferred_element_type=F32)
    for rows, h1_r, acc_r in zip(parts, h1, acc):
        h2 = _layer_norm_rows(alpha * h1_r + acc_r, g2_ref[...], b2_ref[...])
        o_ref[rows, :] = h2
        ob_ref[rows, :] = h2.astype(BF16)


def _block_tail(ys, wo, h, g1, b1, wu, wd, g2, b2, alpha, tm, ff_chunk):
    t, d = h.shape
    d_ff = wu.shape[1]
    gw = GROUP_WIDTH
    y_spec = pl.BlockSpec((tm, gw), lambda i: (i, 0))
    row_spec = pl.BlockSpec((tm, d), lambda i: (i, 0))
    resident = lambda shape: pl.BlockSpec(shape, lambda i: (0, 0), pipeline_mode=pl.Buffered(1))
    vec = _const_spec((1, d))
    return pl.pallas_call(
        functools.partial(_block_tail_kernel, alpha, ff_chunk),
        grid=(t // tm,),
        in_specs=[y_spec, y_spec, y_spec, y_spec, resident((d, d)), row_spec, vec, vec,
                  resident((d, d_ff)), resident((d_ff, d)), vec, vec],
        out_specs=[row_spec, row_spec],
        out_shape=[jax.ShapeDtypeStruct((t, d), F32), jax.ShapeDtypeStruct((t, d), BF16)],
        compiler_params=_params("parallel"),
        name="out_proj_mlp",
    )(*ys, wo, h, g1.reshape(1, d), b1.reshape(1, d), wu, wd, g2.reshape(1, d), b2.reshape(1, d))


def _rglru_kernel(xa_ref, ga_ref, cw_ref, cb_ref, wg_ref, bg_ref, lam_ref, o_ref, ext_ref, hprev_ref):
    ts, gw = xa_ref.shape
    si = pl.program_id(1)

    pad = SUBLANES

    @pl.when(si == 0)
    def _():
        ext_ref[0:pad, :] = jnp.zeros((pad, gw), F32)
        hprev_ref[...] = jnp.zeros((SUBLANES, gw), F32)

    @pl.when(si > 0)
    def _():
        ext_ref[0:pad, :] = ext_ref[ts:ts + pad, :]

    x = xa_ref[...]
    ext_ref[pad:ts + pad, :] = x
    cw = cw_ref[...]
    last = CONV_WIDTH - 1
    xc = x * cw[last:last + 1, :] + cb_ref[...]
    for k in range(1, CONV_WIDTH):
        xc += ext_ref[pad - k:pad - k + ts, :] * cw[last - k:last - k + 1, :]

    gates = jnp.dot(xc.astype(BF16), wg_ref[...], preferred_element_type=F32) + bg_ref[...]
    r = jax.nn.sigmoid(gates[:, :gw])
    i = jax.nn.sigmoid(gates[:, gw:])
    lam = lam_ref[...]
    log_sig_lam = -(jnp.maximum(-lam, 0.0) + jnp.log1p(jnp.exp(-jnp.abs(lam))))
    log_a = RG_LRU_C * r * log_sig_lam
    a = jnp.exp(log_a)
    th = jnp.tanh(log_a)
    one_minus_a2 = -2.0 * th / (1.0 - th)
    u = jnp.sqrt(jnp.maximum(one_minus_a2, 0.0)) * (i * xc)

    sub = lax.broadcasted_iota(jnp.int32, (SUBLANES, gw), 0)
    h_in = hprev_ref[SUBLANES - 1:SUBLANES, :]
    tiles = []
    for t0 in range(0, ts, SUBLANES):
        a_t = a[t0:t0 + SUBLANES, :]
        u_t = u[t0:t0 + SUBLANES, :]
        k = 1
        while k < SUBLANES:
            live = sub >= k
            a_sh = jnp.where(live, pltpu.roll(a_t, k, 0), 1.0)
            u_sh = jnp.where(live, pltpu.roll(u_t, k, 0), 0.0)
            u_t = a_t * u_sh + u_t
            a_t = a_t * a_sh
            k *= 2
        h_t = u_t + a_t * h_in
        h_in = h_t[SUBLANES - 1:SUBLANES, :]
        tiles.append(h_t)
    h = jnp.concatenate(tiles, axis=0)
    hprev_ref[...] = tiles[-1]
    o_ref[...] = (jax.nn.gelu(ga_ref[...], approximate=True) * h).astype(BF16)


def _blockdiag_heads(w):
    h, di, dj = w.shape
    eye = jnp.eye(h, dtype=w.dtype)
    return (eye[:, None, :, None] * w[:, :, None, :]).reshape(h * di, h * dj)


def _rglru(proj, conv_w, conv_b, wa, ba, wx, bx, lam, ts):
    b_, s_, _ = proj.shape
    gw = GROUP_WIDTH
    wg = jnp.concatenate([_blockdiag_heads(wa), _blockdiag_heads(wx)], axis=1).astype(BF16)
    bg = jnp.concatenate([ba.reshape(1, gw), bx.reshape(1, gw)], axis=1)
    col = lambda j: pl.BlockSpec((None, ts, gw), lambda b, s: (b, s, j))
    return pl.pallas_call(
        _rglru_kernel,
        grid=(b_, s_ // ts),
        in_specs=[col(A_X), col(A_G), _const_spec((CONV_WIDTH, gw)), _const_spec((1, gw)),
                  _const_spec((gw, 2 * gw)), _const_spec((1, 2 * gw)), _const_spec((1, gw))],
        out_specs=pl.BlockSpec((None, ts, gw), lambda b, s: (b, s, 0)),
        out_shape=jax.ShapeDtypeStruct((b_, s_, gw), BF16),
        scratch_shapes=[pltpu.VMEM((ts + SUBLANES, gw), F32), pltpu.VMEM((SUBLANES, gw), F32)],
        compiler_params=_params("parallel", "arbitrary"),
        name="rglru",
    )(proj, proj, conv_w, conv_b.reshape(1, gw), wg, bg, lam.reshape(1, gw))


def _retention_kernel(q_ref, k_ref, v_ref, g_ref, cos_ref, sin_ref, qdec_ref, kdec_ref, cdec_ref,
                      dmask_ref, ng_ref, eb_ref, o_ref, state_ref):
    ts, gw = q_ref.shape
    c = RET_CHUNK
    pw = 2 * HEAD_DIM
    n_pairs = N_GROUP_HEADS // 2

    @pl.when(pl.program_id(1) == 0)
    def _():
        state_ref[...] = jnp.zeros((gw, gw), F32)

    lane = lax.broadcasted_iota(jnp.int32, (ts, gw), 1)
    first_half = (lane % HEAD_DIM) < (HEAD_DIM // 2)
    cos = cos_ref[...]
    sin = sin_ref[...]

    def rotary(t):
        partner = jnp.where(first_half, pltpu.roll(t, gw - HEAD_DIM // 2, 1), pltpu.roll(t, HEAD_DIM // 2, 1))
        return t * cos + partner * sin

    q = rotary(q_ref[...])
    k = rotary(k_ref[...]) * (HEAD_DIM ** -0.5)
    qb = q.astype(BF16)
    kb = k.astype(BF16)
    q_dec = (q * qdec_ref[...]).astype(BF16)
    k_dec = (k * kdec_ref[...]).astype(BF16)
    vb = v_ref[...]
    eb = eb_ref[...]
    same_head = eb > 0
    lane_kv = lax.broadcasted_iota(jnp.int32, (c, pw), 1)
    chunks = [slice(n * c, (n + 1) * c) for n in range(ts // c)]

    kv = [lax.dot_general(k_dec[r, :], vb[r, :], (((0,), (0,)), ((), ())), preferred_element_type=F32)
          for r in chunks]
    state = state_ref[...]
    entering = []
    for kv_n in kv:
        entering.append(state.astype(BF16))
        state = state * cdec_ref[...] + jnp.where(same_head, kv_n, 0.0)
    state_ref[...] = state

    outs = []
    for r, state_n in zip(chunks, entering):
        inter = jnp.dot(q_dec[r, :], state_n, preferred_element_type=F32)
        intra = []
        for p in range(n_pairs):
            lanes = slice(p * pw, (p + 1) * pw)
            kbd = _pair_blockdiag(kb[r, lanes], lane_kv)
            vbd = _pair_blockdiag(vb[r, lanes], lane_kv)
            scores = lax.dot_general(qb[r, lanes], kbd, (((1,), (1,)), ((), ())), preferred_element_type=F32)
            intra.append(jnp.dot((scores * dmask_ref[p]).astype(BF16), vbd, preferred_element_type=F32))
        outs.append(inter + jnp.concatenate(intra, axis=1))
    o = jnp.concatenate(outs, axis=0)

    mu = _head_mean(o, eb)
    oc = o - mu
    var = _head_mean(oc * oc, eb)
    o = oc * lax.rsqrt(var + NORM_EPS) * ng_ref[...]
    o_ref[...] = (jax.nn.silu(g_ref[...]) * o).astype(BF16)


def _retention_tables(s_, rows):
    inv_freq = ROPE_BASE ** (-np.arange(0, HEAD_DIM, 2, dtype=np.float64) / HEAD_DIM)
    ang = np.arange(s_, dtype=np.float64)[:, None] * inv_freq[None, :]
    cos, sin = np.cos(ang), np.sin(ang)
    cos_t = np.tile(np.concatenate([cos, cos], axis=-1), (1, N_GROUP_HEADS))
    sin_t = np.tile(np.concatenate([-sin, sin], axis=-1), (1, N_GROUP_HEADS))
    c_ = RET_CHUNK
    log_gamma = np.log1p(-np.exp2(-5.0 - np.arange(N_GROUP_HEADS, dtype=np.float64)))
    pos = np.arange(c_, dtype=np.float64)
    diff = pos[:, None] - pos[None, :]
    dmask = np.where(diff >= 0, np.exp(log_gamma[:, None, None] * np.maximum(diff, 0.0)), 0.0)
    lanes = lambda per_head: np.repeat(per_head, HEAD_DIM, axis=-1)
    kdec = lanes(np.exp(log_gamma[None, :] * (c_ - 1.0 - pos)[:, None]))
    qdec = lanes(np.exp(log_gamma[None, :] * (pos + 1.0)[:, None]))
    cdec = lanes(np.exp(log_gamma * c_)[None, :])
    dmask = dmask.reshape(N_GROUP_HEADS // 2, 2, c_, c_).transpose(0, 2, 1, 3).reshape(N_GROUP_HEADS // 2, c_, 2 * c_)
    tables = (cos_t, sin_t, np.tile(qdec, (rows // c_, 1)), np.tile(kdec, (rows // c_, 1)), cdec, dmask)
    return tuple(jnp.asarray(t, F32) for t in tables)


def _retention(proj, proj_b, norm_g, eb, ts):
    b_, s_, _ = proj.shape
    gw = GROUP_WIDTH
    c_ = RET_CHUNK
    cos_t, sin_t, qdec, kdec, cdec, dmask = _retention_tables(s_, ts)
    col = lambda j: pl.BlockSpec((None, ts, gw), lambda b, n: (b, n, j))
    pos_spec = pl.BlockSpec((ts, gw), lambda b, n: (n, 0))
    return pl.pallas_call(
        _retention_kernel,
        grid=(b_, s_ // ts),
        in_specs=[col(R_Q), col(R_K), col(R_V), col(R_G), pos_spec, pos_spec,
                  _const_spec((ts, gw)), _const_spec((ts, gw)), _const_spec((1, gw)),
                  _const_spec((N_GROUP_HEADS // 2, c_, 2 * c_)), _const_spec((1, gw)), _const_spec((gw, gw))],
        out_specs=pl.BlockSpec((None, ts, gw), lambda b, n: (b, n, 0)),
        out_shape=jax.ShapeDtypeStruct((b_, s_, gw), BF16),
        scratch_shapes=[pltpu.VMEM((gw, gw), F32)],
        compiler_params=_params("parallel", "arbitrary"),
        name="retention",
    )(proj, proj, proj_b, proj, cos_t, sin_t, qdec, kdec, cdec, dmask, norm_g.reshape(1, gw), eb)


def _pair_blockdiag(x, lane):
    zero = jnp.zeros_like(x)
    return jnp.concatenate([jnp.where(lane < HEAD_DIM, x, zero), jnp.where(lane >= HEAD_DIM, x, zero)], axis=0)


def _stickbreak_kernel(q_ref, k_ref, v_ref, m_ref, o_ref, kbd_ref, vbd_ref):
    tq = q_ref.shape[0]
    blk = SB_BLOCK
    n_sub = tq // blk
    pw = 2 * HEAD_DIM
    n_pairs = N_GROUP_HEADS // 2
    n_blocks = k_ref.shape[0] // blk
    i = pl.program_id(1)
    scale = HEAD_DIM ** -0.5

    @pl.when(i == 0)
    def _():
        lane_kv = lax.broadcasted_iota(jnp.int32, (blk, pw), 1)

        def build(j, _):
            rows = pl.ds(pl.multiple_of(j * blk, blk), blk)
            for p in range(n_pairs):
                dst = pl.ds(pl.multiple_of((j * n_pairs + p) * (2 * blk), 2 * blk), 2 * blk)
                kbd_ref[dst, :] = _pair_blockdiag(k_ref[rows, p * pw:(p + 1) * pw], lane_kv)
                vbd_ref[dst, :] = _pair_blockdiag(v_ref[rows, p * pw:(p + 1) * pw], lane_kv)
            return 0

        lax.fori_loop(0, n_blocks, build, 0)

    row = lax.broadcasted_iota(jnp.int32, (blk, 2 * blk), 0)
    key_off = lax.broadcasted_iota(jnp.int32, (blk, 2 * blk), 1) % blk
    strictly_before = key_off < row
    m = m_ref[...]
    qs = [[q_ref[u * blk:(u + 1) * blk, p * pw:(p + 1) * pw] * scale for p in range(n_pairs)]
          for u in range(n_sub)]

    def step(first_block, carry, depth, diagonal_first, never_negative):
        chains = [(u, p, d) for u in range(n_sub) for p in range(n_pairs) for d in range(depth)]
        z2, vbd, log2_w, tot = {}, {}, {}, {}
        for u, p, d in chains:
            j = jnp.maximum(first_block + u - d, 0)
            base = pl.multiple_of((j * n_pairs + p) * (2 * blk), 2 * blk)
            vbd[u, p, d] = vbd_ref[pl.ds(base, 2 * blk), :]
            z2[u, p, d] = lax.dot_general(qs[u][p], kbd_ref[pl.ds(base, 2 * blk), :], (((1,), (1,)), ((), ())),
                                          preferred_element_type=F32) * LOG2_E
        for c in chains:
            softplus2 = jnp.maximum(z2[c], jnp.log2(1.0 + jnp.exp2(jnp.minimum(z2[c], EXP2_CLAMP))))
            if diagonal_first and c[2] == 0:
                softplus2 = jnp.where(strictly_before, softplus2, 0.0)
            hi, lo = _split2(softplus2)
            log2_w[c] = z2[c] + jnp.dot(jnp.concatenate([hi, lo], axis=1), m, preferred_element_type=F32)
            tot[c] = (jnp.sum(softplus2[:, :blk], axis=1, keepdims=True),
                      jnp.sum(softplus2[:, blk:], axis=1, keepdims=True))
        out = {}
        for u in range(n_sub):
            for p in range(n_pairs):
                acc, run0, run1 = carry[u][p]
                for d in range(depth):
                    c = (u, p, d)
                    use0, use1 = run0, run1
                    if not never_negative(u, d):
                        finished = first_block + u - d < 0
                        use0 = jnp.where(finished, -jnp.inf, run0)
                        use1 = jnp.where(finished, -jnp.inf, run1)
                    w = jnp.concatenate([jnp.exp2(log2_w[c][:, :blk] + use0), jnp.exp2(log2_w[c][:, blk:] + use1)],
                                        axis=1)
                    if diagonal_first and d == 0:
                        w = jnp.where(strictly_before, w, 0.0)
                    acc = acc + jnp.dot(w.astype(BF16), vbd[c], preferred_element_type=F32)
                    run0 = run0 - tot[c][0]
                    run1 = run1 - tot[c][1]
                out[u, p] = (acc, run0, run1)
        return tuple(tuple(out[u, p] for p in range(n_pairs)) for u in range(n_sub))

    zero_col = jnp.zeros((blk, 1), F32)
    carry = tuple(tuple((jnp.zeros((blk, pw), F32), zero_col, zero_col) for _ in range(n_pairs))
                  for _ in range(n_sub))
    carry = step(n_sub * i, carry, SB_FIRST_DEPTH, True, lambda u, d: u >= d)

    def any_weight_left(c):
        top = functools.reduce(jnp.maximum, [r for sub in c for _, run0, run1 in sub for r in (run0, run1)])
        return (jnp.max(top) > EXP2_F32_ZERO_BELOW).astype(jnp.int32)

    def sweep(state):
        t, _, c = state
        c = step(n_sub * i - SB_FIRST_DEPTH - t, c, 1, False, lambda u, d: u == n_sub - 1)
        return t + 1, any_weight_left(c), c

    n_trips = n_sub * i + n_sub - SB_FIRST_DEPTH
    _, _, carry = lax.while_loop(lambda state: (state[0] < n_trips) & (state[1] > 0), sweep,
                                 (jnp.int32(0), any_weight_left(carry), carry))
    for u in range(n_sub):
        for p in range(n_pairs):
            o_ref[u * blk:(u + 1) * blk, p * pw:(p + 1) * pw] = carry[u][p][0].astype(BF16)


def _stickbreak(proj_b, tq):
    b_, s_, _ = proj_b.shape
    gw = GROUP_WIDTH
    blk = SB_BLOCK
    idx = np.arange(2 * blk)
    same_head = (idx[:, None] // blk) == (idx[None, :] // blk)
    m = -(same_head & (idx[:, None] >= idx[None, :])).astype(np.float32)
    m = jnp.asarray(np.concatenate([m, m], axis=0), BF16)
    return pl.pallas_call(
        _stickbreak_kernel,
        grid=(b_, s_ // tq),
        in_specs=[pl.BlockSpec((None, tq, gw), lambda b, i: (b, i, S_Q)),
                  pl.BlockSpec((None, s_, gw), lambda b, i: (b, 0, S_K)),
                  pl.BlockSpec((None, s_, gw), lambda b, i: (b, 0, S_V)),
                  _const_spec((4 * blk, 2 * blk))],
        out_specs=pl.BlockSpec((None, tq, gw), lambda b, i: (b, i, 0)),
        out_shape=jax.ShapeDtypeStruct((b_, s_, gw), BF16),
        scratch_shapes=[pltpu.VMEM((2 * s_ * (N_GROUP_HEADS // 2), 2 * HEAD_DIM), BF16),
                        pltpu.VMEM((2 * s_ * (N_GROUP_HEADS // 2), 2 * HEAD_DIM), BF16)],
        compiler_params=_params("parallel", "arbitrary"),
        name="stickbreak",
    )(proj_b, proj_b, proj_b, m)


def _split3(x):
    p1 = x.astype(BF16)
    r1 = x - p1.astype(F32)
    p2 = r1.astype(BF16)
    p3 = (r1 - p2.astype(F32)).astype(BF16)
    return p1, p2, p3


def _dot3(a, parts):
    return sum(jnp.dot(a, p, preferred_element_type=F32) for p in parts)


def _hgrn_kernel(layer, q_ref, f_ref, v_ref, g_ref, lbl_ref, ng_ref, tri_ref, sel_ref, bias_ref, eb_ref, o_ref,
                 state_ref, b2_ref, key_ref):
    ts, gw = q_ref.shape
    c = HGRN_CHUNK
    n_chunks = ts // c
    pw = 2 * HEAD_DIM
    n_pairs = N_GROUP_HEADS // 2

    @pl.when(pl.program_id(1) == 0)
    def _():
        state_ref[...] = jnp.zeros(state_ref.shape, F32)

    logits = lbl_ref[...]
    e = jnp.exp(logits - jnp.max(logits, axis=0, keepdims=True))
    lb_p = e / jnp.sum(e, axis=0, keepdims=True)
    lb = jnp.sum(lb_p[1:layer + 1, :], axis=0, keepdims=True) if layer > 0 else jnp.zeros((1, gw), F32)

    f_pre = f_ref[...]
    q = q_ref[...]
    f_gate = lb + (1.0 - lb) * jax.nn.sigmoid(f_pre)
    log_f = jnp.log(jnp.maximum(f_gate, GATE_FLOOR))
    kk = (1.0 - lb) * jax.nn.sigmoid(-f_pre)
    parts = _split3(log_f)
    b = _dot3(tri_ref[...], parts)
    b_tot = _dot3(sel_ref[...], parts)
    qe = (q * jnp.exp(b)).astype(BF16)
    kd = (kk * jnp.exp(b_tot - b)).astype(BF16)
    decay = jnp.exp(b_tot)
    b2_ref[...] = b * math.log2(math.e)
    key_ref[...] = (b - jnp.log(kk)) * math.log2(math.e)

    eb = eb_ref[...]
    vb = v_ref[...].astype(BF16)
    same_head = eb[0:pw, 0:pw] > 0
    rows = [slice(n * c, (n + 1) * c) for n in range(n_chunks)]
    lanes = [slice(p * pw, (p + 1) * pw) for p in range(n_pairs)]
    kv_t = [[lax.dot_general(vb[r, l], kd[r, l], (((0,), (0,)), ((), ())), preferred_element_type=F32)
             for r in rows] for l in lanes]
    states = []
    for p in range(n_pairs):
        state = state_ref[p]
        entering = []
        for n in range(n_chunks):
            entering.append(state.astype(BF16))
            state = state * decay[n * c:n * c + 1, lanes[p]] + jnp.where(same_head, kv_t[p][n], 0.0)
        state_ref[p] = state
        states.append(entering)
    o_state = [jnp.concatenate([lax.dot_general(qe[rows[n], lanes[p]], states[p][n], (((1,), (1,)), ((), ())),
                                                preferred_element_type=F32) for n in range(n_chunks)], axis=0)
               for p in range(n_pairs)]

    half = c // 2
    o_intra = []
    for n in range(n_chunks):
        r0 = n * c
        blocks = []
        for s in range(c):
            lo = 0 if s < half else half
            key_s = key_ref[r0 + s:r0 + s + 1, :]
            exponent = (b2_ref[r0 + lo:r0 + c, :] - key_s) + bias_ref[s, lo:c, :]
            blocks.append(jnp.exp2(exponent) * q_ref[r0 + lo:r0 + c, :])
        g = jnp.dot(jnp.concatenate(blocks, axis=0).astype(BF16), eb, preferred_element_type=F32)
        top = jnp.zeros((half, gw), F32)
        bottom = jnp.zeros((half, gw), F32)
        for s in range(c):
            vs = v_ref[r0 + s:r0 + s + 1, :]
            if s < half:
                top += g[s * c:s * c + half, :] * vs
                bottom += g[s * c + half:(s + 1) * c, :] * vs
            else:
                start = half * c + (s - half) * half
                bottom += g[start:start + half, :] * vs
        o_intra.append(top)
        o_intra.append(bottom)
    o_intra = jnp.concatenate(o_intra, axis=0)

    o = jnp.concatenate(o_state, axis=1) + o_intra
    ms = _head_mean(o * o, eb)
    o = o * lax.rsqrt(ms + NORM_EPS)
    o_ref[...] = (o * ng_ref[...] * jax.nn.silu(g_ref[...])).astype(BF16)


def _hgrn(proj, lb_logits, norm_g, eb, layer, ts):
    b_, s_, _ = proj.shape
    gw = GROUP_WIDTH
    c = HGRN_CHUNK
    depth = lb_logits.shape[0]
    row = np.arange(ts)
    same_chunk = (row[:, None] // c) == (row[None, :] // c)
    tri = jnp.asarray(same_chunk & (row[:, None] >= row[None, :]), BF16)
    sel = jnp.asarray(same_chunk, BF16)
    pos = np.arange(c)
    bias = np.where(pos[None, :, None] >= pos[:, None, None], 0.0, -1e30)
    bias = jnp.asarray(np.broadcast_to(bias, (c, c, gw)), F32)
    col = lambda j: pl.BlockSpec((None, ts, gw), lambda b, s: (b, s, j))
    return pl.pallas_call(
        functools.partial(_hgrn_kernel, layer),
        grid=(b_, s_ // ts),
        in_specs=[col(D_Q), col(D_F), col(D_V), col(D_G), _const_spec((depth, gw)), _const_spec((1, gw)),
                  _const_spec((ts, ts)), _const_spec((ts, ts)), _const_spec((c, c, gw)), _const_spec((gw, gw))],
        out_specs=pl.BlockSpec((None, ts, gw), lambda b, s: (b, s, 0)),
        out_shape=jax.ShapeDtypeStruct((b_, s_, gw), BF16),
        scratch_shapes=[pltpu.VMEM((N_GROUP_HEADS // 2, 2 * HEAD_DIM, 2 * HEAD_DIM), F32),
                        pltpu.VMEM((ts, gw), F32), pltpu.VMEM((ts, gw), F32)],
        compiler_params=_params("parallel", "arbitrary"),
        name="hgrn2",
    )(proj, proj, proj, proj, lb_logits, norm_g.reshape(1, gw), tri, sel, bias, eb)


def kernel(x, ln_in_g, ln_in_b, w_in, conv_w, conv_b, rg_wa, rg_ba, rg_wx, rg_bx, rg_lambda, ret_norm_g,
           hgrn_lb_logits, hgrn_norm_g, w_out, ln1_g, ln1_b, w_up, w_down, ln2_g, ln2_b):
    b_, s_, d = x.shape
    depth = w_in.shape[0]
    t = b_ * s_
    alpha = (2 * depth) ** 0.25
    proj_rows = min(PROJ_ROWS, t)
    tail_rows = min(TAIL_ROWS, t)

    head = np.arange(GROUP_WIDTH) // HEAD_DIM
    eb = jnp.asarray(head[:, None] == head[None, :], BF16)

    h = hb = None
    gw = GROUP_WIDTH
    columns = lambda w, slices: jnp.concatenate([w[:, s * gw:(s + 1) * gw] for s in slices], axis=1).astype(BF16)
    for l in range(depth):
        wf, wb = columns(w_in[l], F32_SLICES), columns(w_in[l], BF16_SLICES)
        if l == 0:
            h, pf, pb = _proj(x.reshape(t, d), wf, wb, proj_rows, ln=(ln_in_g, ln_in_b))
        else:
            pf, pb = _proj(hb, wf, wb, proj_rows)
        pf = pf.reshape(b_, s_, -1)
        pb = pb.reshape(b_, s_, -1)
        y_a = _rglru(pf, conv_w[l], conv_b[l], rg_wa[l], rg_ba[l], rg_wx[l], rg_bx[l], rg_lambda[l],
                     min(RGLRU_ROWS, s_))
        y_b = _retention(pf, pb, ret_norm_g[l], eb, min(RETENTION_ROWS, s_))
        y_c = _stickbreak(pb, min(STICKBREAK_ROWS, s_))
        y_d = _hgrn(pf, hgrn_lb_logits, hgrn_norm_g[l], eb, l, min(HGRN_ROWS, s_))
        ys = [y.reshape(t, gw) for y in (y_a, y_b, y_c, y_d)]
        h, hb = _block_tail(ys, w_out[l].astype(BF16), h, ln1_g[l], ln1_b[l], w_up[l].astype(BF16),
                            w_down[l].astype(BF16), ln2_g[l], ln2_b[l], alpha, tail_rows, FF_CHUNK)
    return h.reshape(b_, s_, d).astype(x.dtype)
```

```python
import functools
import math

import jax
import jax.numpy as jnp
import numpy as np
from jax import lax
from jax.experimental import pallas as pl
from jax.experimental.pallas import tpu as pltpu

F32 = jnp.float32
BF16 = jnp.bfloat16

HEAD_DIM = 64
N_GROUP_HEADS = 4
GROUP_WIDTH = HEAD_DIM * N_GROUP_HEADS
F32_SLICES = (0, 1, 2, 3, 5, 9, 10, 11, 12)
BF16_SLICES = (4, 6, 7, 8)
A_X, A_G, R_Q, R_K, R_G, D_Q, D_F, D_V, D_G = range(9)
R_V, S_Q, S_K, S_V = range(4)
CONV_WIDTH = 4
RG_LRU_C = 8.0
RET_CHUNK = 128
SB_BLOCK = 128
SB_FIRST_DEPTH = 3
HGRN_CHUNK = 16
ROPE_BASE = 10000.0
LN_EPS = 1e-5
NORM_EPS = 1e-6
GATE_FLOOR = 1e-30
EXP2_F32_ZERO_BELOW = -150.1
EXP2_CLAMP = 126.0
LOG2_E = math.log2(math.e)

VMEM_LIMIT_BYTES = 56 * 1024 * 1024
SUBLANES = 8
DENSE_ROW_PARTS = 2
PROJ_ROWS = 1024
TAIL_ROWS = 512
FF_CHUNK = 1024
RGLRU_ROWS = 2048
RETENTION_ROWS = 2048
STICKBREAK_ROWS = 1024
HGRN_ROWS = 1024
HGRN_CUMSUM_ROWS = 64


def _params(*semantics):
    return pltpu.CompilerParams(dimension_semantics=semantics, vmem_limit_bytes=VMEM_LIMIT_BYTES)


def _const_spec(shape):
    zeros = (0,) * len(shape)
    return pl.BlockSpec(shape, lambda *_: zeros)


def _layer_norm_rows(x, g, b):
    mu = jnp.mean(x, axis=-1, keepdims=True)
    xc = x - mu
    var = jnp.mean(xc * xc, axis=-1, keepdims=True)
    return xc * lax.rsqrt(var + LN_EPS) * g + b


def _split2(x):
    hi = x.astype(BF16)
    lo = (x - hi.astype(F32)).astype(BF16)
    return hi, lo


def _head_mean(x, eb):
    hi, lo = _split2(x)
    s = jnp.dot(hi, eb, preferred_element_type=F32) + jnp.dot(lo, eb, preferred_element_type=F32)
    return s * (1.0 / HEAD_DIM)


def _proj_kernel(h_ref, wf_ref, wb_ref, pf_ref, pb_ref):
    h = h_ref[...]
    pf_ref[...] = jnp.dot(h, wf_ref[...], preferred_element_type=F32)
    pb_ref[...] = jnp.dot(h, wb_ref[...], preferred_element_type=F32).astype(BF16)


def _ln_proj_kernel(x_ref, g_ref, b_ref, wf_ref, wb_ref, h_ref, pf_ref, pb_ref):
    tm = x_ref.shape[0]
    for r in range(DENSE_ROW_PARTS):
        rows = slice(r * (tm // DENSE_ROW_PARTS), (r + 1) * (tm // DENSE_ROW_PARTS))
        h = _layer_norm_rows(x_ref[rows, :], g_ref[...], b_ref[...])
        h_ref[rows, :] = h
        hb = h.astype(BF16)
        pf_ref[rows, :] = jnp.dot(hb, wf_ref[...], preferred_element_type=F32)
        pb_ref[rows, :] = jnp.dot(hb, wb_ref[...], preferred_element_type=F32).astype(BF16)


def _proj(h, wf, wb, tm, ln=None):
    t, d = h.shape
    nf, nb = wf.shape[1], wb.shape[1]
    resident = lambda shape: pl.BlockSpec(shape, lambda i: (0, 0), pipeline_mode=pl.Buffered(1))
    row = lambda n: pl.BlockSpec((tm, n), lambda i: (i, 0))
    proj_shapes = [jax.ShapeDtypeStruct((t, nf), F32), jax.ShapeDtypeStruct((t, nb), BF16)]
    if ln is None:
        return pl.pallas_call(
            _proj_kernel,
            grid=(t // tm,),
            in_specs=[row(d), resident((d, nf)), resident((d, nb))],
            out_specs=[row(nf), row(nb)],
            out_shape=proj_shapes,
            compiler_params=_params("parallel"),
            name="in_proj",
        )(h, wf, wb)
    g, b = ln
    return pl.pallas_call(
        _ln_proj_kernel,
        grid=(t // tm,),
        in_specs=[row(d), _const_spec((1, d)), _const_spec((1, d)), resident((d, nf)), resident((d, nb))],
        out_specs=[row(d), row(nf), row(nb)],
        out_shape=[jax.ShapeDtypeStruct((t, d), F32)] + proj_shapes,
        compiler_params=_params("parallel"),
        name="ln_in_proj",
    )(h, g.reshape(1, d), b.reshape(1, d), wf, wb)


def _block_tail_kernel(alpha, ff_chunk, ya_ref, yb_ref, yc_ref, yd_ref, wo_ref, h_ref, g1_ref, b1_ref,
                       wu_ref, wd_ref, g2_ref, b2_ref, o_ref, ob_ref):
    tm = h_ref.shape[0]
    parts = [slice(r * (tm // DENSE_ROW_PARTS), (r + 1) * (tm // DENSE_ROW_PARTS)) for r in range(DENSE_ROW_PARTS)]
    y_refs = (ya_ref, yb_ref, yc_ref, yd_ref)
    mix = [jnp.dot(jnp.concatenate([y_ref[rows, :] for y_ref in y_refs], axis=1), wo_ref[...],
                   preferred_element_type=F32) for rows in parts]
    h1 = [_layer_norm_rows(alpha * h_ref[rows, :] + mix_r, g1_ref[...], b1_ref[...]) for rows, mix_r in zip(parts, mix)]
    hb = [h.astype(BF16) for h in h1]
    d_ff = wu_ref.shape[1]
    acc = [jnp.zeros(h.shape, F32) for h in h1]
    for c in range(d_ff // ff_chunk):
        cols = slice(c * ff_chunk, (c + 1) * ff_chunk)
        for r in range(DENSE_ROW_PARTS):
            u = jnp.dot(hb[r], wu_ref[:, cols], preferred_element_type=F32)
            u = jnp.square(jnp.maximum(u, 0.0)).astype(BF16)
            acc[r] += jnp.dot(u, wd_ref[cols, :], preferred_element_type=F32)
    for rows, h1_r, acc_r in zip(parts, h1, acc):
        h2 = _layer_norm_rows(alpha * h1_r + acc_r, g2_ref[...], b2_ref[...])
        o_ref[rows, :] = h2
        ob_ref[rows, :] = h2.astype(BF16)


def _block_tail(ys, wo, h, g1, b1, wu, wd, g2, b2, alpha, tm, ff_chunk):
    t, d = h.shape
    d_ff = wu.shape[1]
    gw = GROUP_WIDTH
    y_spec = pl.BlockSpec((tm, gw), lambda i: (i, 0))
    row_spec = pl.BlockSpec((tm, d), lambda i: (i, 0))
    resident = lambda shape: pl.BlockSpec(shape, lambda i: (0, 0), pipeline_mode=pl.Buffered(1))
    vec = _const_spec((1, d))
    return pl.pallas_call(
        functools.partial(_block_tail_kernel, alpha, ff_chunk),
        grid=(t // tm,),
        in_specs=[y_spec, y_spec, y_spec, y_spec, resident((d, d)), row_spec, vec, vec,
                  resident((d, d_ff)), resident((d_ff, d)), vec, vec],
        out_specs=[row_spec, row_spec],
        out_shape=[jax.ShapeDtypeStruct((t, d), F32), jax.ShapeDtypeStruct((t, d), BF16)],
        compiler_params=_params("parallel"),
        name="out_proj_mlp",
    )(*ys, wo, h, g1.reshape(1, d), b1.reshape(1, d), wu, wd, g2.reshape(1, d), b2.reshape(1, d))


def _rglru_kernel(xa_ref, ga_ref, cw_ref, cb_ref, wg_ref, bg_ref, lam_ref, o_ref, ext_ref, hprev_ref):
    ts, gw = xa_ref.shape
    si = pl.program_id(1)

    pad = SUBLANES

    @pl.when(si == 0)
    def _():
        ext_ref[0:pad, :] = jnp.zeros((pad, gw), F32)
        hprev_ref[...] = jnp.zeros((SUBLANES, gw), F32)

    @pl.when(si > 0)
    def _():
        ext_ref[0:pad, :] = ext_ref[ts:ts + pad, :]

    x = xa_ref[...]
    ext_ref[pad:ts + pad, :] = x
    cw = cw_ref[...]
    last = CONV_WIDTH - 1
    xc = x * cw[last:last + 1, :] + cb_ref[...]
    for k in range(1, CONV_WIDTH):
        xc += ext_ref[pad - k:pad - k + ts, :] * cw[last - k:last - k + 1, :]

    gates = jnp.dot(xc.astype(BF16), wg_ref[...], preferred_element_type=F32) + bg_ref[...]
    r = jax.nn.sigmoid(gates[:, :gw])
    i = jax.nn.sigmoid(gates[:, gw:])
    lam = lam_ref[...]
    log_sig_lam = -(jnp.maximum(-lam, 0.0) + jnp.log1p(jnp.exp(-jnp.abs(lam))))
    log_a = RG_LRU_C * r * log_sig_lam
    a = jnp.exp(log_a)
    th = jnp.tanh(log_a)
    one_minus_a2 = -2.0 * th / (1.0 - th)
    u = jnp.sqrt(jnp.maximum(one_minus_a2, 0.0)) * (i * xc)

    sub = lax.broadcasted_iota(jnp.int32, (SUBLANES, gw), 0)
    h_in = hprev_ref[SUBLANES - 1:SUBLANES, :]
    tiles = []
    for t0 in range(0, ts, SUBLANES):
        a_t = a[t0:t0 + SUBLANES, :]
        u_t = u[t0:t0 + SUBLANES, :]
        k = 1
        while k < SUBLANES:
            live = sub >= k
            a_sh = jnp.where(live, pltpu.roll(a_t, k, 0), 1.0)
            u_sh = jnp.where(live, pltpu.roll(u_t, k, 0), 0.0)
            u_t = a_t * u_sh + u_t
            a_t = a_t * a_sh
            k *= 2
        h_t = u_t + a_t * h_in
        h_in = h_t[SUBLANES - 1:SUBLANES, :]
        tiles.append(h_t)
    h = jnp.concatenate(tiles, axis=0)
    hprev_ref[...] = tiles[-1]
    o_ref[...] = (jax.nn.gelu(ga_ref[...], approximate=True) * h).astype(BF16)


def _blockdiag_heads(w):
    h, di, dj = w.shape
    eye = jnp.eye(h, dtype=w.dtype)
    return (eye[:, None, :, None] * w[:, :, None, :]).reshape(h * di, h * dj)


def _rglru(proj, conv_w, conv_b, wa, ba, wx, bx, lam, ts):
    b_, s_, _ = proj.shape
    gw = GROUP_WIDTH
    wg = jnp.concatenate([_blockdiag_heads(wa), _blockdiag_heads(wx)], axis=1).astype(BF16)
    bg = jnp.concatenate([ba.reshape(1, gw), bx.reshape(1, gw)], axis=1)
    col = lambda j: pl.BlockSpec((None, ts, gw), lambda b, s: (b, s, j))
    return pl.pallas_call(
        _rglru_kernel,
        grid=(b_, s_ // ts),
        in_specs=[col(A_X), col(A_G), _const_spec((CONV_WIDTH, gw)), _const_spec((1, gw)),
                  _const_spec((gw, 2 * gw)), _const_spec((1, 2 * gw)), _const_spec((1, gw))],
        out_specs=pl.BlockSpec((None, ts, gw), lambda b, s: (b, s, 0)),
        out_shape=jax.ShapeDtypeStruct((b_, s_, gw), BF16),
        scratch_shapes=[pltpu.VMEM((ts + SUBLANES, gw), F32), pltpu.VMEM((SUBLANES, gw), F32)],
        compiler_params=_params("parallel", "arbitrary"),
        name="rglru",
    )(proj, proj, conv_w, conv_b.reshape(1, gw), wg, bg, lam.reshape(1, gw))


def _retention_kernel(q_ref, k_ref, v_ref, g_ref, cos_ref, sin_ref, qdec_ref, kdec_ref, cdec_ref,
                      dmask_ref, ng_ref, eb_ref, o_ref, state_ref):
    ts, gw = q_ref.shape
    c = RET_CHUNK
    pw = 2 * HEAD_DIM
    n_pairs = N_GROUP_HEADS // 2

    @pl.when(pl.program_id(1) == 0)
    def _():
        state_ref[...] = jnp.zeros((gw, gw), F32)

    lane = lax.broadcasted_iota(jnp.int32, (ts, gw), 1)
    first_half = (lane % HEAD_DIM) < (HEAD_DIM // 2)
    cos = cos_ref[...]
    sin = sin_ref[...]

    def rotary(t):
        partner = jnp.where(first_half, pltpu.roll(t, gw - HEAD_DIM // 2, 1), pltpu.roll(t, HEAD_DIM // 2, 1))
        return t * cos + partner * sin

    q = rotary(q_ref[...])
    k = rotary(k_ref[...]) * (HEAD_DIM ** -0.5)
    qb = q.astype(BF16)
    kb = k.astype(BF16)
    q_dec = (q * qdec_ref[...]).astype(BF16)
    k_dec = (k * kdec_ref[...]).astype(BF16)
    vb = v_ref[...]
    eb = eb_ref[...]
    same_head = eb > 0
    lane_kv = lax.broadcasted_iota(jnp.int32, (c, pw), 1)
    chunks = [slice(n * c, (n + 1) * c) for n in range(ts // c)]

    kv = [lax.dot_general(k_dec[r, :], vb[r, :], (((0,), (0,)), ((), ())), preferred_element_type=F32)
          for r in chunks]
    state = state_ref[...]
    entering = []
    for kv_n in kv:
        entering.append(state.astype(BF16))
        state = state * cdec_ref[...] + jnp.where(same_head, kv_n, 0.0)
    state_ref[...] = state

    outs = []
    for r, state_n in zip(chunks, entering):
        inter = jnp.dot(q_dec[r, :], state_n, preferred_element_type=F32)
        intra = []
        for p in range(n_pairs):
            lanes = slice(p * pw, (p + 1) * pw)
            kbd = _pair_blockdiag(kb[r, lanes], lane_kv)
            vbd = _pair_blockdiag(vb[r, lanes], lane_kv)
            scores = lax.dot_general(qb[r, lanes], kbd, (((1,), (1,)), ((), ())), preferred_element_type=F32)
            intra.append(jnp.dot((scores * dmask_ref[p]).astype(BF16), vbd, preferred_element_type=F32))
        outs.append(inter + jnp.concatenate(intra, axis=1))
    o = jnp.concatenate(outs, axis=0)

    mu = _head_mean(o, eb)
    oc = o - mu
    var = _head_mean(oc * oc, eb)
    o = oc * lax.rsqrt(var + NORM_EPS) * ng_ref[...]
    o_ref[...] = (jax.nn.silu(g_ref[...]) * o).astype(BF16)


def _retention_tables(s_, rows):
    inv_freq = ROPE_BASE ** (-np.arange(0, HEAD_DIM, 2, dtype=np.float64) / HEAD_DIM)
    ang = np.arange(s_, dtype=np.float64)[:, None] * inv_freq[None, :]
    cos, sin = np.cos(ang), np.sin(ang)
    cos_t = np.tile(np.concatenate([cos, cos], axis=-1), (1, N_GROUP_HEADS))
    sin_t = np.tile(np.concatenate([-sin, sin], axis=-1), (1, N_GROUP_HEADS))
    c_ = RET_CHUNK
    log_gamma = np.log1p(-np.exp2(-5.0 - np.arange(N_GROUP_HEADS, dtype=np.float64)))
    pos = np.arange(c_, dtype=np.float64)
    diff = pos[:, None] - pos[None, :]
    dmask = np.where(diff >= 0, np.exp(log_gamma[:, None, None] * np.maximum(diff, 0.0)), 0.0)
    lanes = lambda per_head: np.repeat(per_head, HEAD_DIM, axis=-1)
    kdec = lanes(np.exp(log_gamma[None, :] * (c_ - 1.0 - pos)[:, None]))
    qdec = lanes(np.exp(log_gamma[None, :] * (pos + 1.0)[:, None]))
    cdec = lanes(np.exp(log_gamma * c_)[None, :])
    dmask = dmask.reshape(N_GROUP_HEADS // 2, 2, c_, c_).transpose(0, 2, 1, 3).reshape(N_GROUP_HEADS // 2, c_, 2 * c_)
    tables = (cos_t, sin_t, np.tile(qdec, (rows // c_, 1)), np.tile(kdec, (rows // c_, 1)), cdec, dmask)
    return tuple(jnp.asarray(t, F32) for t in tables)


def _retention(proj, proj_b, norm_g, eb, ts):
    b_, s_, _ = proj.shape
    gw = GROUP_WIDTH
    c_ = RET_CHUNK
    cos_t, sin_t, qdec, kdec, cdec, dmask = _retention_tables(s_, ts)
    col = lambda j: pl.BlockSpec((None, ts, gw), lambda b, n: (b, n, j))
    pos_spec = pl.BlockSpec((ts, gw), lambda b, n: (n, 0))
    return pl.pallas_call(
        _retention_kernel,
        grid=(b_, s_ // ts),
        in_specs=[col(R_Q), col(R_K), col(R_V), col(R_G), pos_spec, pos_spec,
                  _const_spec((ts, gw)), _const_spec((ts, gw)), _const_spec((1, gw)),
                  _const_spec((N_GROUP_HEADS // 2, c_, 2 * c_)), _const_spec((1, gw)), _const_spec((gw, gw))],
        out_specs=pl.BlockSpec((None, ts, gw), lambda b, n: (b, n, 0)),
        out_shape=jax.ShapeDtypeStruct((b_, s_, gw), BF16),
        scratch_shapes=[pltpu.VMEM((gw, gw), F32)],
        compiler_params=_params("parallel", "arbitrary"),
        name="retention",
    )(proj, proj, proj_b, proj, cos_t, sin_t, qdec, kdec, cdec, dmask, norm_g.reshape(1, gw), eb)


def _pair_blockdiag(x, lane):
    zero = jnp.zeros_like(x)
    return jnp.concatenate([jnp.where(lane < HEAD_DIM, x, zero), jnp.where(lane >= HEAD_DIM, x, zero)], axis=0)


def _stickbreak_kernel(q_ref, k_ref, v_ref, m_ref, o_ref, kbd_ref, vbd_ref):
    tq = q_ref.shape[0]
    blk = SB_BLOCK
    n_sub = tq // blk
    pw = 2 * HEAD_DIM
    n_pairs = N_GROUP_HEADS // 2
    n_blocks = k_ref.shape[0] // blk
    i = pl.program_id(1)
    scale = HEAD_DIM ** -0.5

    @pl.when(i == 0)
    def _():
        lane_kv = lax.broadcasted_iota(jnp.int32, (blk, pw), 1)

        def build(j, _):
            rows = pl.ds(pl.multiple_of(j * blk, blk), blk)
            for p in range(n_pairs):
                dst = pl.ds(pl.multiple_of((j * n_pairs + p) * (2 * blk), 2 * blk), 2 * blk)
                kbd_ref[dst, :] = _pair_blockdiag(k_ref[rows, p * pw:(p + 1) * pw], lane_kv)
                vbd_ref[dst, :] = _pair_blockdiag(v_ref[rows, p * pw:(p + 1) * pw], lane_kv)
            return 0

        lax.fori_loop(0, n_blocks, build, 0)

    row = lax.broadcasted_iota(jnp.int32, (blk, 2 * blk), 0)
    key_off = lax.broadcasted_iota(jnp.int32, (blk, 2 * blk), 1) % blk
    strictly_before = key_off < row
    m = m_ref[...]
    qs = [[q_ref[u * blk:(u + 1) * blk, p * pw:(p + 1) * pw] * scale for p in range(n_pairs)]
          for u in range(n_sub)]

    def step(first_block, carry, depth, diagonal_first, never_negative):
        chains = [(u, p, d) for u in range(n_sub) for p in range(n_pairs) for d in range(depth)]
        z2, vbd, log2_w, tot = {}, {}, {}, {}
        for u, p, d in chains:
            j = jnp.maximum(first_block + u - d, 0)
            base = pl.multiple_of((j * n_pairs + p) * (2 * blk), 2 * blk)
            vbd[u, p, d] = vbd_ref[pl.ds(base, 2 * blk), :]
            z2[u, p, d] = lax.dot_general(qs[u][p], kbd_ref[pl.ds(base, 2 * blk), :], (((1,), (1,)), ((), ())),
                                          preferred_element_type=F32) * LOG2_E
        for c in chains:
            softplus2 = jnp.maximum(z2[c], jnp.log2(1.0 + jnp.exp2(jnp.minimum(z2[c], EXP2_CLAMP))))
            if diagonal_first and c[2] == 0:
                softplus2 = jnp.where(strictly_before, softplus2, 0.0)
            hi, lo = _split2(softplus2)
            log2_w[c] = z2[c] + jnp.dot(jnp.concatenate([hi, lo], axis=1), m, preferred_element_type=F32)
            tot[c] = (jnp.sum(softplus2[:, :blk], axis=1, keepdims=True),
                      jnp.sum(softplus2[:, blk:], axis=1, keepdims=True))
        out = {}
        for u in range(n_sub):
            for p in range(n_pairs):
                acc, run0, run1 = carry[u][p]
                for d in range(depth):
                    c = (u, p, d)
                    use0, use1 = run0, run1
                    if not never_negative(u, d):
                        finished = first_block + u - d < 0
                        use0 = jnp.where(finished, -jnp.inf, run0)
                        use1 = jnp.where(finished, -jnp.inf, run1)
                    w = jnp.concatenate([jnp.exp2(log2_w[c][:, :blk] + use0), jnp.exp2(log2_w[c][:, blk:] + use1)],
                                        axis=1)
                    if diagonal_first and d == 0:
                        w = jnp.where(strictly_before, w, 0.0)
                    acc = acc + jnp.dot(w.astype(BF16), vbd[c], preferred_element_type=F32)
                    run0 = run0 - tot[c][0]
                    run1 = run1 - tot[c][1]
                out[u, p] = (acc, run0, run1)
        return tuple(tuple(out[u, p] for p in range(n_pairs)) for u in range(n_sub))

    zero_col = jnp.zeros((blk, 1), F32)
    carry = tuple(tuple((jnp.zeros((blk, pw), F32), zero_col, zero_col) for _ in range(n_pairs))
                  for _ in range(n_sub))
    carry = step(n_sub * i, carry, SB_FIRST_DEPTH, True, lambda u, d: u >= d)

    def any_weight_left(c):
        top = functools.reduce(jnp.maximum, [r for sub in c for _, run0, run1 in sub for r in (run0, run1)])
        return (jnp.max(top) > EXP2_F32_ZERO_BELOW).astype(jnp.int32)

    def sweep(state):
        t, _, c = state
        c = step(n_sub * i - SB_FIRST_DEPTH - t, c, 1, False, lambda u, d: u == n_sub - 1)
        return t + 1, any_weight_left(c), c

    n_trips = n_sub * i + n_sub - SB_FIRST_DEPTH
    _, _, carry = lax.while_loop(lambda state: (state[0] < n_trips) & (state[1] > 0), sweep,
                                 (jnp.int32(0), any_weight_left(carry), carry))
    for u in range(n_sub):
        for p in range(n_pairs):
            o_ref[u * blk:(u + 1) * blk, p * pw:(p + 1) * pw] = carry[u][p][0].astype(BF16)


def _stickbreak(proj_b, tq):
    b_, s_, _ = proj_b.shape
    gw = GROUP_WIDTH
    blk = SB_BLOCK
    idx = np.arange(2 * blk)
    same_head = (idx[:, None] // blk) == (idx[None, :] // blk)
    m = -(same_head & (idx[:, None] >= idx[None, :])).astype(np.float32)
    m = jnp.asarray(np.concatenate([m, m], axis=0), BF16)
    return pl.pallas_call(
        _stickbreak_kernel,
        grid=(b_, s_ // tq),
        in_specs=[pl.BlockSpec((None, tq, gw), lambda b, i: (b, i, S_Q)),
                  pl.BlockSpec((None, s_, gw), lambda b, i: (b, 0, S_K)),
                  pl.BlockSpec((None, s_, gw), lambda b, i: (b, 0, S_V)),
                  _const_spec((4 * blk, 2 * blk))],
        out_specs=pl.BlockSpec((None, tq, gw), lambda b, i: (b, i, 0)),
        out_shape=jax.ShapeDtypeStruct((b_, s_, gw), BF16),
        scratch_shapes=[pltpu.VMEM((2 * s_ * (N_GROUP_HEADS // 2), 2 * HEAD_DIM), BF16),
                        pltpu.VMEM((2 * s_ * (N_GROUP_HEADS // 2), 2 * HEAD_DIM), BF16)],
        compiler_params=_params("parallel", "arbitrary"),
        name="stickbreak",
    )(proj_b, proj_b, proj_b, m)


def _split3(x):
    p1 = x.astype(BF16)
    r1 = x - p1.astype(F32)
    p2 = r1.astype(BF16)
    p3 = (r1 - p2.astype(F32)).astype(BF16)
    return p1, p2, p3


def _dot3(a, parts):
    return sum(jnp.dot(a, p, preferred_element_type=F32) for p in parts)


def _hgrn_kernel(layer, q_ref, f_ref, v_ref, g_ref, lbl_ref, ng_ref, tri_ref, sel_ref, bias_ref, eb_ref, o_ref,
                 state_ref, b2_ref, key_ref):
    ts, gw = q_ref.shape
    c = HGRN_CHUNK
    n_chunks = ts // c
    pw = 2 * HEAD_DIM
    n_pairs = N_GROUP_HEADS // 2

    @pl.when(pl.program_id(1) == 0)
    def _():
        state_ref[...] = jnp.zeros(state_ref.shape, F32)

    logits = lbl_ref[...]
    e = jnp.exp(logits - jnp.max(logits, axis=0, keepdims=True))
    lb_p = e / jnp.sum(e, axis=0, keepdims=True)
    lb = jnp.sum(lb_p[1:layer + 1, :], axis=0, keepdims=True) if layer > 0 else jnp.zeros((1, gw), F32)

    f_pre = f_ref[...]
    q = q_ref[...]
    f_gate = lb + (1.0 - lb) * jax.nn.sigmoid(f_pre)
    log_f = jnp.log(jnp.maximum(f_gate, GATE_FLOOR))
    kk = (1.0 - lb) * jax.nn.sigmoid(-f_pre)
    parts = _split3(log_f)
    blocks = [slice(r, r + HGRN_CUMSUM_ROWS) for r in range(0, ts, HGRN_CUMSUM_ROWS)]
    b = jnp.concatenate([_dot3(tri_ref[...], [p[r, :] for p in parts]) for r in blocks], axis=0)
    b_tot = jnp.concatenate([_dot3(sel_ref[...], [p[r, :] for p in parts]) for r in blocks], axis=0)
    qe = (q * jnp.exp(b)).astype(BF16)
    kd = (kk * jnp.exp(b_tot - b)).astype(BF16)
    decay = jnp.exp(b_tot)
    b2_ref[...] = b * math.log2(math.e)
    key_ref[...] = (b - jnp.log(kk)) * math.log2(math.e)

    eb = eb_ref[...]
    vb = v_ref[...].astype(BF16)
    same_head = eb[0:pw, 0:pw] > 0
    rows = [slice(n * c, (n + 1) * c) for n in range(n_chunks)]
    lanes = [slice(p * pw, (p + 1) * pw) for p in range(n_pairs)]
    kv_t = [[lax.dot_general(vb[r, l], kd[r, l], (((0,), (0,)), ((), ())), preferred_element_type=F32)
             for r in rows] for l in lanes]
    states = []
    for p in range(n_pairs):
        state = state_ref[p]
        entering = []
        for n in range(n_chunks):
            entering.append(state.astype(BF16))
            state = state * decay[n * c:n * c + 1, lanes[p]] + jnp.where(same_head, kv_t[p][n], 0.0)
        state_ref[p] = state
        states.append(entering)
    o_state = [jnp.concatenate([lax.dot_general(qe[rows[n], lanes[p]], states[p][n], (((1,), (1,)), ((), ())),
                                                preferred_element_type=F32) for n in range(n_chunks)], axis=0)
               for p in range(n_pairs)]

    half = c // 2
    o_intra = []
    for n in range(n_chunks):
        r0 = n * c
        blocks = []
        for s in range(c):
            lo = 0 if s < half else half
            key_s = key_ref[r0 + s:r0 + s + 1, :]
            exponent = (b2_ref[r0 + lo:r0 + c, :] - key_s) + bias_ref[s, lo:c, :]
            blocks.append(jnp.exp2(exponent) * q_ref[r0 + lo:r0 + c, :])
        g = jnp.dot(jnp.concatenate(blocks, axis=0).astype(BF16), eb, preferred_element_type=F32)
        top = jnp.zeros((half, gw), F32)
        bottom = jnp.zeros((half, gw), F32)
        for s in range(c):
            vs = v_ref[r0 + s:r0 + s + 1, :]
            if s < half:
                top += g[s * c:s * c + half, :] * vs
                bottom += g[s * c + half:(s + 1) * c, :] * vs
            else:
                start = half * c + (s - half) * half
                bottom += g[start:start + half, :] * vs
        o_intra.append(top)
        o_intra.append(bottom)
    o_intra = jnp.concatenate(o_intra, axis=0)

    o = jnp.concatenate(o_state, axis=1) + o_intra
    ms = _head_mean(o * o, eb)
    o = o * lax.rsqrt(ms + NORM_EPS)
    o_ref[...] = (o * ng_ref[...] * jax.nn.silu(g_ref[...])).astype(BF16)


def _hgrn(proj, lb_logits, norm_g, eb, layer, ts):
    b_, s_, _ = proj.shape
    gw = GROUP_WIDTH
    c = HGRN_CHUNK
    depth = lb_logits.shape[0]
    cs = min(HGRN_CUMSUM_ROWS, ts)
    row = np.arange(cs)
    same_chunk = (row[:, None] // c) == (row[None, :] // c)
    tri = jnp.asarray(same_chunk & (row[:, None] >= row[None, :]), BF16)
    sel = jnp.asarray(same_chunk, BF16)
    pos = np.arange(c)
    bias = np.where(pos[None, :, None] >= pos[:, None, None], 0.0, -1e30)
    bias = jnp.asarray(np.broadcast_to(bias, (c, c, gw)), F32)
    col = lambda j: pl.BlockSpec((None, ts, gw), lambda b, s: (b, s, j))
    return pl.pallas_call(
        functools.partial(_hgrn_kernel, layer),
        grid=(b_, s_ // ts),
        in_specs=[col(D_Q), col(D_F), col(D_V), col(D_G), _const_spec((depth, gw)), _const_spec((1, gw)),
                  _const_spec((cs, cs)), _const_spec((cs, cs)), _const_spec((c, c, gw)), _const_spec((gw, gw))],
        out_specs=pl.BlockSpec((None, ts, gw), lambda b, s: (b, s, 0)),
        out_shape=jax.ShapeDtypeStruct((b_, s_, gw), BF16),
        scratch_shapes=[pltpu.VMEM((N_GROUP_HEADS // 2, 2 * HEAD_DIM, 2 * HEAD_DIM), F32),
                        pltpu.VMEM((ts, gw), F32), pltpu.VMEM((ts, gw), F32)],
        compiler_params=_params("parallel", "arbitrary"),
        name="hgrn2",
    )(proj, proj, proj, proj, lb_logits, norm_g.reshape(1, gw), tri, sel, bias, eb)


def kernel(x, ln_in_g, ln_in_b, w_in, conv_w, conv_b, rg_wa, rg_ba, rg_wx, rg_bx, rg_lambda, ret_norm_g,
           hgrn_lb_logits, hgrn_norm_g, w_out, ln1_g, ln1_b, w_up, w_down, ln2_g, ln2_b):
    b_, s_, d = x.shape
    depth = w_in.shape[0]
    t = b_ * s_
    alpha = (2 * depth) ** 0.25
    proj_rows = min(PROJ_ROWS, t)
    tail_rows = min(TAIL_ROWS, t)

    head = np.arange(GROUP_WIDTH) // HEAD_DIM
    eb = jnp.asarray(head[:, None] == head[None, :], BF16)

    h = hb = None
    gw = GROUP_WIDTH
    columns = lambda w, slices: jnp.concatenate([w[:, s * gw:(s + 1) * gw] for s in slices], axis=1).astype(BF16)
    for l in range(depth):
        wf, wb = columns(w_in[l], F32_SLICES), columns(w_in[l], BF16_SLICES)
        if l == 0:
            h, pf, pb = _proj(x.reshape(t, d), wf, wb, proj_rows, ln=(ln_in_g, ln_in_b))
        else:
            pf, pb = _proj(hb, wf, wb, proj_rows)
        pf = pf.reshape(b_, s_, -1)
        pb = pb.reshape(b_, s_, -1)
        y_a = _rglru(pf, conv_w[l], conv_b[l], rg_wa[l], rg_ba[l], rg_wx[l], rg_bx[l], rg_lambda[l],
                     min(RGLRU_ROWS, s_))
        y_b = _retention(pf, pb, ret_norm_g[l], eb, min(RETENTION_ROWS, s_))
        y_c = _stickbreak(pb, min(STICKBREAK_ROWS, s_))
        y_d = _hgrn(pf, hgrn_lb_logits, hgrn_norm_g[l], eb, l, min(HGRN_ROWS, s_))
        ys = [y.reshape(t, gw) for y in (y_a, y_b, y_c, y_d)]
        h, hb = _block_tail(ys, w_out[l].astype(BF16), h, ln1_g[l], ln1_b[l], w_up[l].astype(BF16),
                            w_down[l].astype(BF16), ln2_g[l], ln2_b[l], alpha, tail_rows, FF_CHUNK)
    return h.reshape(b_, s_, d).astype(x.dtype)
```

```python
import functools
import math

import jax
import jax.numpy as jnp
import numpy as np
from jax import lax
from jax.experimental import pallas as pl
from jax.experimental.pallas import tpu as pltpu

F32 = jnp.float32
BF16 = jnp.bfloat16

HEAD_DIM = 64
N_GROUP_HEADS = 4
GROUP_WIDTH = HEAD_DIM * N_GROUP_HEADS
F32_SLICES = (0, 1, 2, 3, 5, 9, 10, 11, 12)
BF16_SLICES = (4, 6, 7, 8)
A_X, A_G, R_Q, R_K, R_G, D_Q, D_F, D_V, D_G = range(9)
R_V, S_Q, S_K, S_V = range(4)
CONV_WIDTH = 4
RG_LRU_C = 8.0
RET_CHUNK = 128
SB_BLOCK = 128
SB_FIRST_DEPTH = 3
HGRN_CHUNK = 16
ROPE_BASE = 10000.0
LN_EPS = 1e-5
NORM_EPS = 1e-6
GATE_FLOOR = 1e-30
EXP2_F32_ZERO_BELOW = -150.1
EXP2_CLAMP = 126.0
LOG2_E = math.log2(math.e)

VMEM_LIMIT_BYTES = 56 * 1024 * 1024
SUBLANES = 8
DENSE_ROW_PARTS = 2
PROJ_ROWS = 1024
TAIL_ROWS = 512
FF_CHUNK = 1024
RGLRU_ROWS = 2048
RETENTION_ROWS = 2048
STICKBREAK_ROWS = 1024
HGRN_ROWS = 1024
HGRN_CUMSUM_ROWS = 64


def _params(*semantics):
    return pltpu.CompilerParams(dimension_semantics=semantics, vmem_limit_bytes=VMEM_LIMIT_BYTES)


def _const_spec(shape):
    zeros = (0,) * len(shape)
    return pl.BlockSpec(shape, lambda *_: zeros)


def _layer_norm_rows(x, g, b):
    mu = jnp.mean(x, axis=-1, keepdims=True)
    xc = x - mu
    var = jnp.mean(xc * xc, axis=-1, keepdims=True)
    return xc * lax.rsqrt(var + LN_EPS) * g + b


def _split2(x):
    hi = x.astype(BF16)
    lo = (x - hi.astype(F32)).astype(BF16)
    return hi, lo


def _head_mean(x, eb):
    hi, lo = _split2(x)
    s = jnp.dot(hi, eb, preferred_element_type=F32) + jnp.dot(lo, eb, preferred_element_type=F32)
    return s * (1.0 / HEAD_DIM)


def _proj_kernel(h_ref, wf_ref, wb_ref, pf_ref, pb_ref):
    h = h_ref[...]
    pf_ref[...] = jnp.dot(h, wf_ref[...], preferred_element_type=F32)
    pb_ref[...] = jnp.dot(h, wb_ref[...], preferred_element_type=F32).astype(BF16)


def _ln_proj_kernel(x_ref, g_ref, b_ref, wf_ref, wb_ref, h_ref, pf_ref, pb_ref):
    tm = x_ref.shape[0]
    for r in range(DENSE_ROW_PARTS):
        rows = slice(r * (tm // DENSE_ROW_PARTS), (r + 1) * (tm // DENSE_ROW_PARTS))
        h = _layer_norm_rows(x_ref[rows, :], g_ref[...], b_ref[...])
        h_ref[rows, :] = h
        hb = h.astype(BF16)
        pf_ref[rows, :] = jnp.dot(hb, wf_ref[...], preferred_element_type=F32)
        pb_ref[rows, :] = jnp.dot(hb, wb_ref[...], preferred_element_type=F32).astype(BF16)


def _proj(h, wf, wb, tm, ln=None):
    t, d = h.shape
    nf, nb = wf.shape[1], wb.shape[1]
    resident = lambda shape: pl.BlockSpec(shape, lambda i: (0, 0), pipeline_mode=pl.Buffered(1))
    row = lambda n: pl.BlockSpec((tm, n), lambda i: (i, 0))
    proj_shapes = [jax.ShapeDtypeStruct((t, nf), F32), jax.ShapeDtypeStruct((t, nb), BF16)]
    if ln is None:
        return pl.pallas_call(
            _proj_kernel,
            grid=(t // tm,),
            in_specs=[row(d), resident((d, nf)), resident((d, nb))],
            out_specs=[row(nf), row(nb)],
            out_shape=proj_shapes,
            compiler_params=_params("parallel"),
            name="in_proj",
        )(h, wf, wb)
    g, b = ln
    return pl.pallas_call(
        _ln_proj_kernel,
        grid=(t // tm,),
        in_specs=[row(d), _const_spec((1, d)), _const_spec((1, d)), resident((d, nf)), resident((d, nb))],
        out_specs=[row(d), row(nf), row(nb)],
        out_shape=[jax.ShapeDtypeStruct((t, d), F32)] + proj_shapes,
        compiler_params=_params("parallel"),
        name="ln_in_proj",
    )(h, g.reshape(1, d), b.reshape(1, d), wf, wb)


def _block_tail_kernel(alpha, ff_chunk, ya_ref, yb_ref, yc_ref, yd_ref, wo_ref, h_ref, g1_ref, b1_ref,
                       wu_ref, wd_ref, g2_ref, b2_ref, o_ref, ob_ref):
    tm = h_ref.shape[0]
    parts = [slice(r * (tm // DENSE_ROW_PARTS), (r + 1) * (tm // DENSE_ROW_PARTS)) for r in range(DENSE_ROW_PARTS)]
    y_refs = (ya_ref, yb_ref, yc_ref, yd_ref)
    mix = [jnp.dot(jnp.concatenate([y_ref[rows, :] for y_ref in y_refs], axis=1), wo_ref[...],
                   preferred_element_type=F32) for rows in parts]
    h1 = [_layer_norm_rows(alpha * h_ref[rows, :] + mix_r, g1_ref[...], b1_ref[...]) for rows, mix_r in zip(parts, mix)]
    hb = [h.astype(BF16) for h in h1]
    d_ff = wu_ref.shape[1]
    acc = [jnp.zeros(h.shape, F32) for h in h1]
    for c in range(d_ff // ff_chunk):
        cols = slice(c * ff_chunk, (c + 1) * ff_chunk)
        for r in range(DENSE_ROW_PARTS):
            u = jnp.dot(hb[r], wu_ref[:, cols], preferred_element_type=F32)
            u = jnp.square(jnp.maximum(u, 0.0)).astype(BF16)
            acc[r] += jnp.dot(u, wd_ref[cols, :], preferred_element_type=F32)
    for rows, h1_r, acc_r in zip(parts, h1, acc):
        h2 = _layer_norm_rows(alpha * h1_r + acc_r, g2_ref[...], b2_ref[...])
        o_ref[rows, :] = h2
        ob_ref[rows, :] = h2.astype(BF16)


def _block_tail(ys, wo, h, g1, b1, wu, wd, g2, b2, alpha, tm, ff_chunk):
    t, d = h.shape
    d_ff = wu.shape[1]
    gw = GROUP_WIDTH
    y_spec = pl.BlockSpec((tm, gw), lambda i: (i, 0))
    row_spec = pl.BlockSpec((tm, d), lambda i: (i, 0))
    resident = lambda shape: pl.BlockSpec(shape, lambda i: (0, 0), pipeline_mode=pl.Buffered(1))
    vec = _const_spec((1, d))
    return pl.pallas_call(
        functools.partial(_block_tail_kernel, alpha, ff_chunk),
        grid=(t // tm,),
        in_specs=[y_spec, y_spec, y_spec, y_spec, resident((d, d)), row_spec, vec, vec,
                  resident((d, d_ff)), resident((d_ff, d)), vec, vec],
        out_specs=[row_spec, row_spec],
        out_shape=[jax.ShapeDtypeStruct((t, d), F32), jax.ShapeDtypeStruct((t, d), BF16)],
        compiler_params=_params("parallel"),
        name="out_proj_mlp",
    )(*ys, wo, h, g1.reshape(1, d), b1.reshape(1, d), wu, wd, g2.reshape(1, d), b2.reshape(1, d))


def _rglru_kernel(xa_ref, ga_ref, cw_ref, cb_ref, wg_ref, bg_ref, lam_ref, o_ref, ext_ref, hprev_ref):
    ts, gw = xa_ref.shape
    si = pl.program_id(1)

    pad = SUBLANES

    @pl.when(si == 0)
    def _():
        ext_ref[0:pad, :] = jnp.zeros((pad, gw), F32)
        hprev_ref[...] = jnp.zeros((SUBLANES, gw), F32)

    @pl.when(si > 0)
    def _():
        ext_ref[0:pad, :] = ext_ref[ts:ts + pad, :]

    x = xa_ref[...]
    ext_ref[pad:ts + pad, :] = x
    cw = cw_ref[...]
    last = CONV_WIDTH - 1
    xc = x * cw[last:last + 1, :] + cb_ref[...]
    for k in range(1, CONV_WIDTH):
        xc += ext_ref[pad - k:pad - k + ts, :] * cw[last - k:last - k + 1, :]

    gates = jnp.dot(xc.astype(BF16), wg_ref[...], preferred_element_type=F32) + bg_ref[...]
    r = jax.nn.sigmoid(gates[:, :gw])
    i = jax.nn.sigmoid(gates[:, gw:])
    lam = lam_ref[...]
    log_sig_lam = -(jnp.maximum(-lam, 0.0) + jnp.log1p(jnp.exp(-jnp.abs(lam))))
    log_a = RG_LRU_C * r * log_sig_lam
    a = jnp.exp(log_a)
    th = jnp.tanh(log_a)
    one_minus_a2 = -2.0 * th / (1.0 - th)
    u = jnp.sqrt(jnp.maximum(one_minus_a2, 0.0)) * (i * xc)

    sub = lax.broadcasted_iota(jnp.int32, (SUBLANES, gw), 0)
    h_in = hprev_ref[SUBLANES - 1:SUBLANES, :]
    tiles = []
    for t0 in range(0, ts, SUBLANES):
        a_t = a[t0:t0 + SUBLANES, :]
        u_t = u[t0:t0 + SUBLANES, :]
        k = 1
        while k < SUBLANES:
            live = sub >= k
            a_sh = jnp.where(live, pltpu.roll(a_t, k, 0), 1.0)
            u_sh = jnp.where(live, pltpu.roll(u_t, k, 0), 0.0)
            u_t = a_t * u_sh + u_t
            a_t = a_t * a_sh
            k *= 2
        h_t = u_t + a_t * h_in
        h_in = h_t[SUBLANES - 1:SUBLANES, :]
        tiles.append(h_t)
    h = jnp.concatenate(tiles, axis=0)
    hprev_ref[...] = tiles[-1]
    o_ref[...] = (jax.nn.gelu(ga_ref[...], approximate=True) * h).astype(BF16)


def _blockdiag_heads(w):
    h, di, dj = w.shape
    eye = jnp.eye(h, dtype=w.dtype)
    return (eye[:, None, :, None] * w[:, :, None, :]).reshape(h * di, h * dj)


def _rglru(proj, conv_w, conv_b, wa, ba, wx, bx, lam, ts):
    b_, s_, _ = proj.shape
    gw = GROUP_WIDTH
    wg = jnp.concatenate([_blockdiag_heads(wa), _blockdiag_heads(wx)], axis=1).astype(BF16)
    bg = jnp.concatenate([ba.reshape(1, gw), bx.reshape(1, gw)], axis=1)
    col = lambda j: pl.BlockSpec((None, ts, gw), lambda b, s: (b, s, j))
    return pl.pallas_call(
        _rglru_kernel,
        grid=(b_, s_ // ts),
        in_specs=[col(A_X), col(A_G), _const_spec((CONV_WIDTH, gw)), _const_spec((1, gw)),
                  _const_spec((gw, 2 * gw)), _const_spec((1, 2 * gw)), _const_spec((1, gw))],
        out_specs=pl.BlockSpec((None, ts, gw), lambda b, s: (b, s, 0)),
        out_shape=jax.ShapeDtypeStruct((b_, s_, gw), BF16),
        scratch_shapes=[pltpu.VMEM((ts + SUBLANES, gw), F32), pltpu.VMEM((SUBLANES, gw), F32)],
        compiler_params=_params("parallel", "arbitrary"),
        name="rglru",
    )(proj, proj, conv_w, conv_b.reshape(1, gw), wg, bg, lam.reshape(1, gw))


def _retention_kernel(q_ref, k_ref, v_ref, g_ref, cos_ref, sin_ref, qdec_ref, kdec_ref, cdec_ref,
                      dmask_ref, ng_ref, eb_ref, o_ref, state_ref):
    ts, gw = q_ref.shape
    c = RET_CHUNK
    pw = 2 * HEAD_DIM
    n_pairs = N_GROUP_HEADS // 2

    @pl.when(pl.program_id(1) == 0)
    def _():
        state_ref[...] = jnp.zeros((gw, gw), F32)

    lane = lax.broadcasted_iota(jnp.int32, (ts, gw), 1)
    first_half = (lane % HEAD_DIM) < (HEAD_DIM // 2)
    cos = cos_ref[...]
    sin = sin_ref[...]

    def rotary(t):
        partner = jnp.where(first_half, pltpu.roll(t, gw - HEAD_DIM // 2, 1), pltpu.roll(t, HEAD_DIM // 2, 1))
        return t * cos + partner * sin

    q = rotary(q_ref[...])
    k = rotary(k_ref[...]) * (HEAD_DIM ** -0.5)
    qb = q.astype(BF16)
    kb = k.astype(BF16)
    q_dec = (q * qdec_ref[...]).astype(BF16)
    k_dec = (k * kdec_ref[...]).astype(BF16)
    vb = v_ref[...]
    eb = eb_ref[...]
    same_head = eb > 0
    lane_kv = lax.broadcasted_iota(jnp.int32, (c, pw), 1)
    chunks = [slice(n * c, (n + 1) * c) for n in range(ts // c)]

    identity = jnp.where(lax.broadcasted_iota(jnp.int32, (gw, gw), 0) == lax.broadcasted_iota(jnp.int32, (gw, gw), 1),
                         1.0, 0.0).astype(BF16)
    k_t = [lax.dot_general(identity, k_dec[r, :], (((1,), (1,)), ((), ())), preferred_element_type=F32)
           for r in chunks]
    kv = [jnp.dot(kt.astype(BF16), vb[r, :], preferred_element_type=F32) for kt, r in zip(k_t, chunks)]
    state = state_ref[...]
    entering = []
    for kv_n in kv:
        entering.append(state.astype(BF16))
        state = state * cdec_ref[...] + jnp.where(same_head, kv_n, 0.0)
    state_ref[...] = state

    pair_lanes = [slice(p * pw, (p + 1) * pw) for p in range(n_pairs)]
    inter = [jnp.dot(q_dec[r, :], state_n, preferred_element_type=F32) for r, state_n in zip(chunks, entering)]
    scores = [[lax.dot_general(qb[r, l], _pair_blockdiag(kb[r, l], lane_kv), (((1,), (1,)), ((), ())),
                               preferred_element_type=F32) for l in pair_lanes] for r in chunks]
    intra = [[jnp.dot((scores[n][p] * dmask_ref[p]).astype(BF16), _pair_blockdiag(vb[r, l], lane_kv),
                      preferred_element_type=F32) for p, l in enumerate(pair_lanes)] for n, r in enumerate(chunks)]
    o = jnp.concatenate([inter[n] + jnp.concatenate(intra[n], axis=1) for n in range(len(chunks))], axis=0)

    mu = _head_mean(o, eb)
    oc = o - mu
    var = _head_mean(oc * oc, eb)
    o = oc * lax.rsqrt(var + NORM_EPS) * ng_ref[...]
    o_ref[...] = (jax.nn.silu(g_ref[...]) * o).astype(BF16)


def _retention_tables(s_, rows):
    inv_freq = ROPE_BASE ** (-np.arange(0, HEAD_DIM, 2, dtype=np.float64) / HEAD_DIM)
    ang = np.arange(s_, dtype=np.float64)[:, None] * inv_freq[None, :]
    cos, sin = np.cos(ang), np.sin(ang)
    cos_t = np.tile(np.concatenate([cos, cos], axis=-1), (1, N_GROUP_HEADS))
    sin_t = np.tile(np.concatenate([-sin, sin], axis=-1), (1, N_GROUP_HEADS))
    c_ = RET_CHUNK
    log_gamma = np.log1p(-np.exp2(-5.0 - np.arange(N_GROUP_HEADS, dtype=np.float64)))
    pos = np.arange(c_, dtype=np.float64)
    diff = pos[:, None] - pos[None, :]
    dmask = np.where(diff >= 0, np.exp(log_gamma[:, None, None] * np.maximum(diff, 0.0)), 0.0)
    lanes = lambda per_head: np.repeat(per_head, HEAD_DIM, axis=-1)
    kdec = lanes(np.exp(log_gamma[None, :] * (c_ - 1.0 - pos)[:, None]))
    qdec = lanes(np.exp(log_gamma[None, :] * (pos + 1.0)[:, None]))
    cdec = lanes(np.exp(log_gamma * c_)[None, :])
    dmask = dmask.reshape(N_GROUP_HEADS // 2, 2, c_, c_).transpose(0, 2, 1, 3).reshape(N_GROUP_HEADS // 2, c_, 2 * c_)
    tables = (cos_t, sin_t, np.tile(qdec, (rows // c_, 1)), np.tile(kdec, (rows // c_, 1)), cdec, dmask)
    return tuple(jnp.asarray(t, F32) for t in tables)


def _retention(proj, proj_b, norm_g, eb, ts):
    b_, s_, _ = proj.shape
    gw = GROUP_WIDTH
    c_ = RET_CHUNK
    cos_t, sin_t, qdec, kdec, cdec, dmask = _retention_tables(s_, ts)
    col = lambda j: pl.BlockSpec((None, ts, gw), lambda b, n: (b, n, j))
    pos_spec = pl.BlockSpec((ts, gw), lambda b, n: (n, 0))
    return pl.pallas_call(
        _retention_kernel,
        grid=(b_, s_ // ts),
        in_specs=[col(R_Q), col(R_K), col(R_V), col(R_G), pos_spec, pos_spec,
                  _const_spec((ts, gw)), _const_spec((ts, gw)), _const_spec((1, gw)),
                  _const_spec((N_GROUP_HEADS // 2, c_, 2 * c_)), _const_spec((1, gw)), _const_spec((gw, gw))],
        out_specs=pl.BlockSpec((None, ts, gw), lambda b, n: (b, n, 0)),
        out_shape=jax.ShapeDtypeStruct((b_, s_, gw), BF16),
        scratch_shapes=[pltpu.VMEM((gw, gw), F32)],
        compiler_params=_params("parallel", "arbitrary"),
        name="retention",
    )(proj, proj, proj_b, proj, cos_t, sin_t, qdec, kdec, cdec, dmask, norm_g.reshape(1, gw), eb)


def _pair_blockdiag(x, lane):
    zero = jnp.zeros_like(x)
    return jnp.concatenate([jnp.where(lane < HEAD_DIM, x, zero), jnp.where(lane >= HEAD_DIM, x, zero)], axis=0)


def _stickbreak_kernel(q_ref, k_ref, v_ref, m_ref, o_ref, kbd_ref, vbd_ref):
    tq = q_ref.shape[0]
    blk = SB_BLOCK
    n_sub = tq // blk
    pw = 2 * HEAD_DIM
    n_pairs = N_GROUP_HEADS // 2
    n_blocks = k_ref.shape[0] // blk
    i = pl.program_id(1)
    scale = HEAD_DIM ** -0.5

    @pl.when(i == 0)
    def _():
        lane_kv = lax.broadcasted_iota(jnp.int32, (blk, pw), 1)

        def build(j, _):
            rows = pl.ds(pl.multiple_of(j * blk, blk), blk)
            for p in range(n_pairs):
                dst = pl.ds(pl.multiple_of((j * n_pairs + p) * (2 * blk), 2 * blk), 2 * blk)
                kbd_ref[dst, :] = _pair_blockdiag(k_ref[rows, p * pw:(p + 1) * pw], lane_kv)
                vbd_ref[dst, :] = _pair_blockdiag(v_ref[rows, p * pw:(p + 1) * pw], lane_kv)
            return 0

        lax.fori_loop(0, n_blocks, build, 0)

    row = lax.broadcasted_iota(jnp.int32, (blk, 2 * blk), 0)
    key_off = lax.broadcasted_iota(jnp.int32, (blk, 2 * blk), 1) % blk
    strictly_before = key_off < row
    m = m_ref[...]
    qs = [[q_ref[u * blk:(u + 1) * blk, p * pw:(p + 1) * pw] * scale for p in range(n_pairs)]
          for u in range(n_sub)]

    def step(first_block, carry, depth, diagonal_first, never_negative):
        chains = [(u, p, d) for u in range(n_sub) for p in range(n_pairs) for d in range(depth)]
        z2, vbd, log2_w, tot = {}, {}, {}, {}
        for u, p, d in chains:
            j = jnp.maximum(first_block + u - d, 0)
            base = pl.multiple_of((j * n_pairs + p) * (2 * blk), 2 * blk)
            vbd[u, p, d] = vbd_ref[pl.ds(base, 2 * blk), :]
            z2[u, p, d] = lax.dot_general(qs[u][p], kbd_ref[pl.ds(base, 2 * blk), :], (((1,), (1,)), ((), ())),
                                          preferred_element_type=F32) * LOG2_E
        for c in chains:
            softplus2 = jnp.maximum(z2[c], jnp.log2(1.0 + jnp.exp2(jnp.minimum(z2[c], EXP2_CLAMP))))
            if diagonal_first and c[2] == 0:
                softplus2 = jnp.where(strictly_before, softplus2, 0.0)
            hi, lo = _split2(softplus2)
            log2_w[c] = z2[c] + jnp.dot(jnp.concatenate([hi, lo], axis=1), m, preferred_element_type=F32)
            tot[c] = (jnp.sum(softplus2[:, :blk], axis=1, keepdims=True),
                      jnp.sum(softplus2[:, blk:], axis=1, keepdims=True))
        out = {}
        for u in range(n_sub):
            for p in range(n_pairs):
                acc, run0, run1 = carry[u][p]
                for d in range(depth):
                    c = (u, p, d)
                    use0, use1 = run0, run1
                    if not never_negative(u, d):
                        finished = first_block + u - d < 0
                        use0 = jnp.where(finished, -jnp.inf, run0)
                        use1 = jnp.where(finished, -jnp.inf, run1)
                    w = jnp.concatenate([jnp.exp2(log2_w[c][:, :blk] + use0), jnp.exp2(log2_w[c][:, blk:] + use1)],
                                        axis=1)
                    if diagonal_first and d == 0:
                        w = jnp.where(strictly_before, w, 0.0)
                    acc = acc + jnp.dot(w.astype(BF16), vbd[c], preferred_element_type=F32)
                    run0 = run0 - tot[c][0]
                    run1 = run1 - tot[c][1]
                out[u, p] = (acc, run0, run1)
        return tuple(tuple(out[u, p] for p in range(n_pairs)) for u in range(n_sub))

    zero_col = jnp.zeros((blk, 1), F32)
    carry = tuple(tuple((jnp.zeros((blk, pw), F32), zero_col, zero_col) for _ in range(n_pairs))
                  for _ in range(n_sub))
    carry = step(n_sub * i, carry, SB_FIRST_DEPTH, True, lambda u, d: u >= d)

    def any_weight_left(c):
        top = functools.reduce(jnp.maximum, [r for sub in c for _, run0, run1 in sub for r in (run0, run1)])
        return (jnp.max(top) > EXP2_F32_ZERO_BELOW).astype(jnp.int32)

    def sweep(state):
        t, _, c = state
        c = step(n_sub * i - SB_FIRST_DEPTH - t, c, 1, False, lambda u, d: u == n_sub - 1)
        return t + 1, any_weight_left(c), c

    n_trips = n_sub * i + n_sub - SB_FIRST_DEPTH
    _, _, carry = lax.while_loop(lambda state: (state[0] < n_trips) & (state[1] > 0), sweep,
                                 (jnp.int32(0), any_weight_left(carry), carry))
    for u in range(n_sub):
        for p in range(n_pairs):
            o_ref[u * blk:(u + 1) * blk, p * pw:(p + 1) * pw] = carry[u][p][0].astype(BF16)


def _stickbreak(proj_b, tq):
    b_, s_, _ = proj_b.shape
    gw = GROUP_WIDTH
    blk = SB_BLOCK
    idx = np.arange(2 * blk)
    same_head = (idx[:, None] // blk) == (idx[None, :] // blk)
    m = -(same_head & (idx[:, None] >= idx[None, :])).astype(np.float32)
    m = jnp.asarray(np.concatenate([m, m], axis=0), BF16)
    return pl.pallas_call(
        _stickbreak_kernel,
        grid=(b_, s_ // tq),
        in_specs=[pl.BlockSpec((None, tq, gw), lambda b, i: (b, i, S_Q)),
                  pl.BlockSpec((None, s_, gw), lambda b, i: (b, 0, S_K)),
                  pl.BlockSpec((None, s_, gw), lambda b, i: (b, 0, S_V)),
                  _const_spec((4 * blk, 2 * blk))],
        out_specs=pl.BlockSpec((None, tq, gw), lambda b, i: (b, i, 0)),
        out_shape=jax.ShapeDtypeStruct((b_, s_, gw), BF16),
        scratch_shapes=[pltpu.VMEM((2 * s_ * (N_GROUP_HEADS // 2), 2 * HEAD_DIM), BF16),
                        pltpu.VMEM((2 * s_ * (N_GROUP_HEADS // 2), 2 * HEAD_DIM), BF16)],
        compiler_params=_params("parallel", "arbitrary"),
        name="stickbreak",
    )(proj_b, proj_b, proj_b, m)


def _split3(x):
    p1 = x.astype(BF16)
    r1 = x - p1.astype(F32)
    p2 = r1.astype(BF16)
    p3 = (r1 - p2.astype(F32)).astype(BF16)
    return p1, p2, p3


def _dot3(a, parts):
    return sum(jnp.dot(a, p, preferred_element_type=F32) for p in parts)


def _hgrn_kernel(layer, q_ref, f_ref, v_ref, g_ref, lbl_ref, ng_ref, tri_ref, sel_ref, bias_ref, eb_ref, o_ref,
                 state_ref, b2_ref, key_ref):
    ts, gw = q_ref.shape
    c = HGRN_CHUNK
    n_chunks = ts // c
    pw = 2 * HEAD_DIM
    n_pairs = N_GROUP_HEADS // 2

    @pl.when(pl.program_id(1) == 0)
    def _():
        state_ref[...] = jnp.zeros(state_ref.shape, F32)

    logits = lbl_ref[...]
    e = jnp.exp(logits - jnp.max(logits, axis=0, keepdims=True))
    lb_p = e / jnp.sum(e, axis=0, keepdims=True)
    lb = jnp.sum(lb_p[1:layer + 1, :], axis=0, keepdims=True) if layer > 0 else jnp.zeros((1, gw), F32)

    f_pre = f_ref[...]
    q = q_ref[...]
    f_gate = lb + (1.0 - lb) * jax.nn.sigmoid(f_pre)
    log_f = jnp.log(jnp.maximum(f_gate, GATE_FLOOR))
    kk = (1.0 - lb) * jax.nn.sigmoid(-f_pre)
    parts = _split3(log_f)
    blocks = [slice(r, r + HGRN_CUMSUM_ROWS) for r in range(0, ts, HGRN_CUMSUM_ROWS)]
    b = jnp.concatenate([_dot3(tri_ref[...], [p[r, :] for p in parts]) for r in blocks], axis=0)
    b_tot = jnp.concatenate([_dot3(sel_ref[...], [p[r, :] for p in parts]) for r in blocks], axis=0)
    qe = (q * jnp.exp(b)).astype(BF16)
    kd = (kk * jnp.exp(b_tot - b)).astype(BF16)
    decay = jnp.exp(b_tot)
    b2_ref[...] = b * math.log2(math.e)
    key_ref[...] = (b - jnp.log(kk)) * math.log2(math.e)

    eb = eb_ref[...]
    vb = v_ref[...].astype(BF16)
    same_head = eb[0:pw, 0:pw] > 0
    rows = [slice(n * c, (n + 1) * c) for n in range(n_chunks)]
    lanes = [slice(p * pw, (p + 1) * pw) for p in range(n_pairs)]
    kv_t = [[lax.dot_general(vb[r, l], kd[r, l], (((0,), (0,)), ((), ())), preferred_element_type=F32)
             for r in rows] for l in lanes]
    states = []
    for p in range(n_pairs):
        state = state_ref[p]
        entering = []
        for n in range(n_chunks):
            entering.append(state.astype(BF16))
            state = state * decay[n * c:n * c + 1, lanes[p]] + jnp.where(same_head, kv_t[p][n], 0.0)
        state_ref[p] = state
        states.append(entering)
    o_state = [jnp.concatenate([lax.dot_general(qe[rows[n], lanes[p]], states[p][n], (((1,), (1,)), ((), ())),
                                                preferred_element_type=F32) for n in range(n_chunks)], axis=0)
               for p in range(n_pairs)]

    half = c // 2
    o_intra = []
    for n in range(n_chunks):
        r0 = n * c
        blocks = []
        for s in range(c):
            lo = 0 if s < half else half
            key_s = key_ref[r0 + s:r0 + s + 1, :]
            exponent = (b2_ref[r0 + lo:r0 + c, :] - key_s) + bias_ref[s, lo:c, :]
            blocks.append(jnp.exp2(exponent) * q_ref[r0 + lo:r0 + c, :])
        g = jnp.dot(jnp.concatenate(blocks, axis=0).astype(BF16), eb, preferred_element_type=F32)
        top = jnp.zeros((half, gw), F32)
        bottom = jnp.zeros((half, gw), F32)
        for s in range(c):
            vs = v_ref[r0 + s:r0 + s + 1, :]
            if s < half:
                top += g[s * c:s * c + half, :] * vs
                bottom += g[s * c + half:(s + 1) * c, :] * vs
            else:
                start = half * c + (s - half) * half
                bottom += g[start:start + half, :] * vs
        o_intra.append(top)
        o_intra.append(bottom)
    o_intra = jnp.concatenate(o_intra, axis=0)

    o = jnp.concatenate(o_state, axis=1) + o_intra
    ms = _head_mean(o * o, eb)
    o = o * lax.rsqrt(ms + NORM_EPS)
    o_ref[...] = (o * ng_ref[...] * jax.nn.silu(g_ref[...])).astype(BF16)


def _hgrn(proj, lb_logits, norm_g, eb, layer, ts):
    b_, s_, _ = proj.shape
    gw = GROUP_WIDTH
    c = HGRN_CHUNK
    depth = lb_logits.shape[0]
    cs = min(HGRN_CUMSUM_ROWS, ts)
    row = np.arange(cs)
    same_chunk = (row[:, None] // c) == (row[None, :] // c)
    tri = jnp.asarray(same_chunk & (row[:, None] >= row[None, :]), BF16)
    sel = jnp.asarray(same_chunk, BF16)
    pos = np.arange(c)
    bias = np.where(pos[None, :, None] >= pos[:, None, None], 0.0, -1e30)
    bias = jnp.asarray(np.broadcast_to(bias, (c, c, gw)), F32)
    col = lambda j: pl.BlockSpec((None, ts, gw), lambda b, s: (b, s, j))
    return pl.pallas_call(
        functools.partial(_hgrn_kernel, layer),
        grid=(b_, s_ // ts),
        in_specs=[col(D_Q), col(D_F), col(D_V), col(D_G), _const_spec((depth, gw)), _const_spec((1, gw)),
                  _const_spec((cs, cs)), _const_spec((cs, cs)), _const_spec((c, c, gw)), _const_spec((gw, gw))],
        out_specs=pl.BlockSpec((None, ts, gw), lambda b, s: (b, s, 0)),
        out_shape=jax.ShapeDtypeStruct((b_, s_, gw), BF16),
        scratch_shapes=[pltpu.VMEM((N_GROUP_HEADS // 2, 2 * HEAD_DIM, 2 * HEAD_DIM), F32),
                        pltpu.VMEM((ts, gw), F32), pltpu.VMEM((ts, gw), F32)],
        compiler_params=_params("parallel", "arbitrary"),
        name="hgrn2",
    )(proj, proj, proj, proj, lb_logits, norm_g.reshape(1, gw), tri, sel, bias, eb)


def kernel(x, ln_in_g, ln_in_b, w_in, conv_w, conv_b, rg_wa, rg_ba, rg_wx, rg_bx, rg_lambda, ret_norm_g,
           hgrn_lb_logits, hgrn_norm_g, w_out, ln1_g, ln1_b, w_up, w_down, ln2_g, ln2_b):
    b_, s_, d = x.shape
    depth = w_in.shape[0]
    t = b_ * s_
    alpha = (2 * depth) ** 0.25
    proj_rows = min(PROJ_ROWS, t)
    tail_rows = min(TAIL_ROWS, t)

    head = np.arange(GROUP_WIDTH) // HEAD_DIM
    eb = jnp.asarray(head[:, None] == head[None, :], BF16)

    h = hb = None
    gw = GROUP_WIDTH
    columns = lambda w, slices: jnp.concatenate([w[:, s * gw:(s + 1) * gw] for s in slices], axis=1).astype(BF16)
    for l in range(depth):
        wf, wb = columns(w_in[l], F32_SLICES), columns(w_in[l], BF16_SLICES)
        if l == 0:
            h, pf, pb = _proj(x.reshape(t, d), wf, wb, proj_rows, ln=(ln_in_g, ln_in_b))
        else:
            pf, pb = _proj(hb, wf, wb, proj_rows)
        pf = pf.reshape(b_, s_, -1)
        pb = pb.reshape(b_, s_, -1)
        y_a = _rglru(pf, conv_w[l], conv_b[l], rg_wa[l], rg_ba[l], rg_wx[l], rg_bx[l], rg_lambda[l],
                     min(RGLRU_ROWS, s_))
        y_b = _retention(pf, pb, ret_norm_g[l], eb, min(RETENTION_ROWS, s_))
        y_c = _stickbreak(pb, min(STICKBREAK_ROWS, s_))
        y_d = _hgrn(pf, hgrn_lb_logits, hgrn_norm_g[l], eb, l, min(HGRN_ROWS, s_))
        ys = [y.reshape(t, gw) for y in (y_a, y_b, y_c, y_d)]
        h, hb = _block_tail(ys, w_out[l].astype(BF16), h, ln1_g[l], ln1_b[l], w_up[l].astype(BF16),
                            w_down[l].astype(BF16), ln2_g[l], ln2_b[l], alpha, tail_rows, FF_CHUNK)
    return h.reshape(b_, s_, d).astype(x.dtype)
```

```python
import functools
import math

import jax
import jax.numpy as jnp
import numpy as np
from jax import lax
from jax.experimental import pallas as pl
from jax.experimental.pallas import tpu as pltpu

F32 = jnp.float32
BF16 = jnp.bfloat16

HEAD_DIM = 64
N_GROUP_HEADS = 4
GROUP_WIDTH = HEAD_DIM * N_GROUP_HEADS
F32_SLICES = (0, 1, 2, 3, 5, 9, 10, 11, 12)
BF16_SLICES = (4, 6, 7, 8)
A_X, A_G, R_Q, R_K, R_G, D_Q, D_F, D_V, D_G = range(9)
R_V, S_Q, S_K, S_V = range(4)
CONV_WIDTH = 4
RG_LRU_C = 8.0
RET_CHUNK = 128
SB_BLOCK = 128
SB_FIRST_DEPTH = 3
HGRN_CHUNK = 16
ROPE_BASE = 10000.0
LN_EPS = 1e-5
NORM_EPS = 1e-6
GATE_FLOOR = 1e-30
EXP2_F32_ZERO_BELOW = -150.1
EXP2_CLAMP = 126.0
LOG2_E = math.log2(math.e)

VMEM_LIMIT_BYTES = 56 * 1024 * 1024
SUBLANES = 8
DENSE_ROW_PARTS = 2
PROJ_ROWS = 1024
TAIL_ROWS = 512
FF_CHUNK = 1024
RGLRU_ROWS = 2048
RETENTION_ROWS = 2048
STICKBREAK_ROWS = 1024
HGRN_ROWS = 2048
HGRN_CUMSUM_ROWS = 64


def _params(*semantics):
    return pltpu.CompilerParams(dimension_semantics=semantics, vmem_limit_bytes=VMEM_LIMIT_BYTES)


def _const_spec(shape):
    zeros = (0,) * len(shape)
    return pl.BlockSpec(shape, lambda *_: zeros)


def _layer_norm_rows(x, g, b):
    mu = jnp.mean(x, axis=-1, keepdims=True)
    xc = x - mu
    var = jnp.mean(xc * xc, axis=-1, keepdims=True)
    return xc * lax.rsqrt(var + LN_EPS) * g + b


def _split2(x):
    hi = x.astype(BF16)
    lo = (x - hi.astype(F32)).astype(BF16)
    return hi, lo


def _head_mean(x, eb):
    hi, lo = _split2(x)
    s = jnp.dot(hi, eb, preferred_element_type=F32) + jnp.dot(lo, eb, preferred_element_type=F32)
    return s * (1.0 / HEAD_DIM)


def _proj_kernel(h_ref, wf_ref, wb_ref, pf_ref, pb_ref):
    h = h_ref[...]
    pf_ref[...] = jnp.dot(h, wf_ref[...], preferred_element_type=F32)
    pb_ref[...] = jnp.dot(h, wb_ref[...], preferred_element_type=F32).astype(BF16)


def _ln_proj_kernel(x_ref, g_ref, b_ref, wf_ref, wb_ref, h_ref, pf_ref, pb_ref):
    tm = x_ref.shape[0]
    for r in range(DENSE_ROW_PARTS):
        rows = slice(r * (tm // DENSE_ROW_PARTS), (r + 1) * (tm // DENSE_ROW_PARTS))
        h = _layer_norm_rows(x_ref[rows, :], g_ref[...], b_ref[...])
        h_ref[rows, :] = h
        hb = h.astype(BF16)
        pf_ref[rows, :] = jnp.dot(hb, wf_ref[...], preferred_element_type=F32)
        pb_ref[rows, :] = jnp.dot(hb, wb_ref[...], preferred_element_type=F32).astype(BF16)


def _proj(h, wf, wb, tm, ln=None):
    t, d = h.shape
    nf, nb = wf.shape[1], wb.shape[1]
    resident = lambda shape: pl.BlockSpec(shape, lambda i: (0, 0), pipeline_mode=pl.Buffered(1))
    row = lambda n: pl.BlockSpec((tm, n), lambda i: (i, 0))
    proj_shapes = [jax.ShapeDtypeStruct((t, nf), F32), jax.ShapeDtypeStruct((t, nb), BF16)]
    if ln is None:
        return pl.pallas_call(
            _proj_kernel,
            grid=(t // tm,),
            in_specs=[row(d), resident((d, nf)), resident((d, nb))],
            out_specs=[row(nf), row(nb)],
            out_shape=proj_shapes,
            compiler_params=_params("parallel"),
            name="in_proj",
        )(h, wf, wb)
    g, b = ln
    return pl.pallas_call(
        _ln_proj_kernel,
        grid=(t // tm,),
        in_specs=[row(d), _const_spec((1, d)), _const_spec((1, d)), resident((d, nf)), resident((d, nb))],
        out_specs=[row(d), row(nf), row(nb)],
        out_shape=[jax.ShapeDtypeStruct((t, d), F32)] + proj_shapes,
        compiler_params=_params("parallel"),
        name="ln_in_proj",
    )(h, g.reshape(1, d), b.reshape(1, d), wf, wb)


def _block_tail_kernel(alpha, ff_chunk, ya_ref, yb_ref, yc_ref, yd_ref, wo_ref, h_ref, g1_ref, b1_ref,
                       wu_ref, wd_ref, g2_ref, b2_ref, o_ref, ob_ref):
    tm = h_ref.shape[0]
    parts = [slice(r * (tm // DENSE_ROW_PARTS), (r + 1) * (tm // DENSE_ROW_PARTS)) for r in range(DENSE_ROW_PARTS)]
    y_refs = (ya_ref, yb_ref, yc_ref, yd_ref)
    mix = [jnp.dot(jnp.concatenate([y_ref[rows, :] for y_ref in y_refs], axis=1), wo_ref[...],
                   preferred_element_type=F32) for rows in parts]
    h1 = [_layer_norm_rows(alpha * h_ref[rows, :] + mix_r, g1_ref[...], b1_ref[...]) for rows, mix_r in zip(parts, mix)]
    hb = [h.astype(BF16) for h in h1]
    d_ff = wu_ref.shape[1]
    acc = [jnp.zeros(h.shape, F32) for h in h1]
    for c in range(d_ff // ff_chunk):
        cols = slice(c * ff_chunk, (c + 1) * ff_chunk)
        for r in range(DENSE_ROW_PARTS):
            u = jnp.dot(hb[r], wu_ref[:, cols], preferred_element_type=F32)
            u = jnp.square(jnp.maximum(u, 0.0)).astype(BF16)
            acc[r] += jnp.dot(u, wd_ref[cols, :], preferred_element_type=F32)
    for rows, h1_r, acc_r in zip(parts, h1, acc):
        h2 = _layer_norm_rows(alpha * h1_r + acc_r, g2_ref[...], b2_ref[...])
        o_ref[rows, :] = h2
        ob_ref[rows, :] = h2.astype(BF16)


def _block_tail(ys, wo, h, g1, b1, wu, wd, g2, b2, alpha, tm, ff_chunk):
    t, d = h.shape
    d_ff = wu.shape[1]
    gw = GROUP_WIDTH
    y_spec = pl.BlockSpec((tm, gw), lambda i: (i, 0))
    row_spec = pl.BlockSpec((tm, d), lambda i: (i, 0))
    resident = lambda shape: pl.BlockSpec(shape, lambda i: (0, 0), pipeline_mode=pl.Buffered(1))
    vec = _const_spec((1, d))
    return pl.pallas_call(
        functools.partial(_block_tail_kernel, alpha, ff_chunk),
        grid=(t // tm,),
        in_specs=[y_spec, y_spec, y_spec, y_spec, resident((d, d)), row_spec, vec, vec,
                  resident((d, d_ff)), resident((d_ff, d)), vec, vec],
        out_specs=[row_spec, row_spec],
        out_shape=[jax.ShapeDtypeStruct((t, d), F32), jax.ShapeDtypeStruct((t, d), BF16)],
        compiler_params=_params("parallel"),
        name="out_proj_mlp",
    )(*ys, wo, h, g1.reshape(1, d), b1.reshape(1, d), wu, wd, g2.reshape(1, d), b2.reshape(1, d))


def _rglru_kernel(xa_ref, ga_ref, cw_ref, cb_ref, wg_ref, bg_ref, lam_ref, o_ref, ext_ref, hprev_ref):
    ts, gw = xa_ref.shape
    si = pl.program_id(1)

    pad = SUBLANES

    @pl.when(si == 0)
    def _():
        ext_ref[0:pad, :] = jnp.zeros((pad, gw), F32)
        hprev_ref[...] = jnp.zeros((SUBLANES, gw), F32)

    @pl.when(si > 0)
    def _():
        ext_ref[0:pad, :] = ext_ref[ts:ts + pad, :]

    x = xa_ref[...]
    ext_ref[pad:ts + pad, :] = x
    cw = cw_ref[...]
    last = CONV_WIDTH - 1
    xc = x * cw[last:last + 1, :] + cb_ref[...]
    for k in range(1, CONV_WIDTH):
        xc += ext_ref[pad - k:pad - k + ts, :] * cw[last - k:last - k + 1, :]

    gates = jnp.dot(xc.astype(BF16), wg_ref[...], preferred_element_type=F32) + bg_ref[...]
    r = jax.nn.sigmoid(gates[:, :gw])
    i = jax.nn.sigmoid(gates[:, gw:])
    lam = lam_ref[...]
    log_sig_lam = -(jnp.maximum(-lam, 0.0) + jnp.log1p(jnp.exp(-jnp.abs(lam))))
    log_a = RG_LRU_C * r * log_sig_lam
    a = jnp.exp(log_a)
    th = jnp.tanh(log_a)
    one_minus_a2 = -2.0 * th / (1.0 - th)
    u = jnp.sqrt(jnp.maximum(one_minus_a2, 0.0)) * (i * xc)

    sub = lax.broadcasted_iota(jnp.int32, (SUBLANES, gw), 0)
    h_in = hprev_ref[SUBLANES - 1:SUBLANES, :]
    tiles = []
    for t0 in range(0, ts, SUBLANES):
        a_t = a[t0:t0 + SUBLANES, :]
        u_t = u[t0:t0 + SUBLANES, :]
        k = 1
        while k < SUBLANES:
            live = sub >= k
            a_sh = jnp.where(live, pltpu.roll(a_t, k, 0), 1.0)
            u_sh = jnp.where(live, pltpu.roll(u_t, k, 0), 0.0)
            u_t = a_t * u_sh + u_t
            a_t = a_t * a_sh
            k *= 2
        h_t = u_t + a_t * h_in
        h_in = h_t[SUBLANES - 1:SUBLANES, :]
        tiles.append(h_t)
    h = jnp.concatenate(tiles, axis=0)
    hprev_ref[...] = tiles[-1]
    o_ref[...] = (jax.nn.gelu(ga_ref[...], approximate=True) * h).astype(BF16)


def _blockdiag_heads(w):
    h, di, dj = w.shape
    eye = jnp.eye(h, dtype=w.dtype)
    return (eye[:, None, :, None] * w[:, :, None, :]).reshape(h * di, h * dj)


def _rglru(proj, conv_w, conv_b, wa, ba, wx, bx, lam, ts):
    b_, s_, _ = proj.shape
    gw = GROUP_WIDTH
    wg = jnp.concatenate([_blockdiag_heads(wa), _blockdiag_heads(wx)], axis=1).astype(BF16)
    bg = jnp.concatenate([ba.reshape(1, gw), bx.reshape(1, gw)], axis=1)
    col = lambda j: pl.BlockSpec((None, ts, gw), lambda b, s: (b, s, j))
    return pl.pallas_call(
        _rglru_kernel,
        grid=(b_, s_ // ts),
        in_specs=[col(A_X), col(A_G), _const_spec((CONV_WIDTH, gw)), _const_spec((1, gw)),
                  _const_spec((gw, 2 * gw)), _const_spec((1, 2 * gw)), _const_spec((1, gw))],
        out_specs=pl.BlockSpec((None, ts, gw), lambda b, s: (b, s, 0)),
        out_shape=jax.ShapeDtypeStruct((b_, s_, gw), BF16),
        scratch_shapes=[pltpu.VMEM((ts + SUBLANES, gw), F32), pltpu.VMEM((SUBLANES, gw), F32)],
        compiler_params=_params("parallel", "arbitrary"),
        name="rglru",
    )(proj, proj, conv_w, conv_b.reshape(1, gw), wg, bg, lam.reshape(1, gw))


def _retention_kernel(q_ref, k_ref, v_ref, g_ref, cos_ref, sin_ref, qdec_ref, kdec_ref, cdec_ref,
                      dmask_ref, ng_ref, eb_ref, o_ref, state_ref):
    ts, gw = q_ref.shape
    c = RET_CHUNK
    pw = 2 * HEAD_DIM
    n_pairs = N_GROUP_HEADS // 2

    @pl.when(pl.program_id(1) == 0)
    def _():
        state_ref[...] = jnp.zeros((gw, gw), F32)

    lane = lax.broadcasted_iota(jnp.int32, (ts, gw), 1)
    first_half = (lane % HEAD_DIM) < (HEAD_DIM // 2)
    cos = cos_ref[...]
    sin = sin_ref[...]

    def rotary(t):
        partner = jnp.where(first_half, pltpu.roll(t, gw - HEAD_DIM // 2, 1), pltpu.roll(t, HEAD_DIM // 2, 1))
        return t * cos + partner * sin

    q = rotary(q_ref[...])
    k = rotary(k_ref[...]) * (HEAD_DIM ** -0.5)
    qb = q.astype(BF16)
    kb = k.astype(BF16)
    q_dec = (q * qdec_ref[...]).astype(BF16)
    k_dec = (k * kdec_ref[...]).astype(BF16)
    vb = v_ref[...]
    eb = eb_ref[...]
    same_head = eb > 0
    lane_kv = lax.broadcasted_iota(jnp.int32, (c, pw), 1)
    chunks = [slice(n * c, (n + 1) * c) for n in range(ts // c)]

    identity = jnp.where(lax.broadcasted_iota(jnp.int32, (gw, gw), 0) == lax.broadcasted_iota(jnp.int32, (gw, gw), 1),
                         1.0, 0.0).astype(BF16)
    k_t = [lax.dot_general(identity, k_dec[r, :], (((1,), (1,)), ((), ())), preferred_element_type=F32)
           for r in chunks]
    kv = [jnp.dot(kt.astype(BF16), vb[r, :], preferred_element_type=F32) for kt, r in zip(k_t, chunks)]
    state = state_ref[...]
    entering = []
    for kv_n in kv:
        entering.append(state.astype(BF16))
        state = state * cdec_ref[...] + jnp.where(same_head, kv_n, 0.0)
    state_ref[...] = state

    pair_lanes = [slice(p * pw, (p + 1) * pw) for p in range(n_pairs)]
    inter = [jnp.dot(q_dec[r, :], state_n, preferred_element_type=F32) for r, state_n in zip(chunks, entering)]
    scores = [[lax.dot_general(qb[r, l], _pair_blockdiag(kb[r, l], lane_kv), (((1,), (1,)), ((), ())),
                               preferred_element_type=F32) for l in pair_lanes] for r in chunks]
    intra = [[jnp.dot((scores[n][p] * dmask_ref[p]).astype(BF16), _pair_blockdiag(vb[r, l], lane_kv),
                      preferred_element_type=F32) for p, l in enumerate(pair_lanes)] for n, r in enumerate(chunks)]
    o = jnp.concatenate([inter[n] + jnp.concatenate(intra[n], axis=1) for n in range(len(chunks))], axis=0)

    mu = _head_mean(o, eb)
    oc = o - mu
    var = _head_mean(oc * oc, eb)
    o = oc * lax.rsqrt(var + NORM_EPS) * ng_ref[...]
    o_ref[...] = (jax.nn.silu(g_ref[...]) * o).astype(BF16)


def _retention_tables(s_, rows):
    inv_freq = ROPE_BASE ** (-np.arange(0, HEAD_DIM, 2, dtype=np.float64) / HEAD_DIM)
    ang = np.arange(s_, dtype=np.float64)[:, None] * inv_freq[None, :]
    cos, sin = np.cos(ang), np.sin(ang)
    cos_t = np.tile(np.concatenate([cos, cos], axis=-1), (1, N_GROUP_HEADS))
    sin_t = np.tile(np.concatenate([-sin, sin], axis=-1), (1, N_GROUP_HEADS))
    c_ = RET_CHUNK
    log_gamma = np.log1p(-np.exp2(-5.0 - np.arange(N_GROUP_HEADS, dtype=np.float64)))
    pos = np.arange(c_, dtype=np.float64)
    diff = pos[:, None] - pos[None, :]
    dmask = np.where(diff >= 0, np.exp(log_gamma[:, None, None] * np.maximum(diff, 0.0)), 0.0)
    lanes = lambda per_head: np.repeat(per_head, HEAD_DIM, axis=-1)
    kdec = lanes(np.exp(log_gamma[None, :] * (c_ - 1.0 - pos)[:, None]))
    qdec = lanes(np.exp(log_gamma[None, :] * (pos + 1.0)[:, None]))
    cdec = lanes(np.exp(log_gamma * c_)[None, :])
    dmask = dmask.reshape(N_GROUP_HEADS // 2, 2, c_, c_).transpose(0, 2, 1, 3).reshape(N_GROUP_HEADS // 2, c_, 2 * c_)
    tables = (cos_t, sin_t, np.tile(qdec, (rows // c_, 1)), np.tile(kdec, (rows // c_, 1)), cdec, dmask)
    return tuple(jnp.asarray(t, F32) for t in tables)


def _retention(proj, proj_b, norm_g, eb, ts):
    b_, s_, _ = proj.shape
    gw = GROUP_WIDTH
    c_ = RET_CHUNK
    cos_t, sin_t, qdec, kdec, cdec, dmask = _retention_tables(s_, ts)
    col = lambda j: pl.BlockSpec((None, ts, gw), lambda b, n: (b, n, j))
    pos_spec = pl.BlockSpec((ts, gw), lambda b, n: (n, 0))
    return pl.pallas_call(
        _retention_kernel,
        grid=(b_, s_ // ts),
        in_specs=[col(R_Q), col(R_K), col(R_V), col(R_G), pos_spec, pos_spec,
                  _const_spec((ts, gw)), _const_spec((ts, gw)), _const_spec((1, gw)),
                  _const_spec((N_GROUP_HEADS // 2, c_, 2 * c_)), _const_spec((1, gw)), _const_spec((gw, gw))],
        out_specs=pl.BlockSpec((None, ts, gw), lambda b, n: (b, n, 0)),
        out_shape=jax.ShapeDtypeStruct((b_, s_, gw), BF16),
        scratch_shapes=[pltpu.VMEM((gw, gw), F32)],
        compiler_params=_params("parallel", "arbitrary"),
        name="retention",
    )(proj, proj, proj_b, proj, cos_t, sin_t, qdec, kdec, cdec, dmask, norm_g.reshape(1, gw), eb)


def _pair_blockdiag(x, lane):
    zero = jnp.zeros_like(x)
    return jnp.concatenate([jnp.where(lane < HEAD_DIM, x, zero), jnp.where(lane >= HEAD_DIM, x, zero)], axis=0)


def _stickbreak_kernel(q_ref, k_ref, v_ref, m_ref, o_ref, kbd_ref, vbd_ref):
    tq = q_ref.shape[0]
    blk = SB_BLOCK
    n_sub = tq // blk
    pw = 2 * HEAD_DIM
    n_pairs = N_GROUP_HEADS // 2
    n_blocks = k_ref.shape[0] // blk
    i = pl.program_id(1)
    scale = HEAD_DIM ** -0.5

    @pl.when(i == 0)
    def _():
        lane_kv = lax.broadcasted_iota(jnp.int32, (blk, pw), 1)

        def build(j, _):
            rows = pl.ds(pl.multiple_of(j * blk, blk), blk)
            for p in range(n_pairs):
                dst = pl.ds(pl.multiple_of((j * n_pairs + p) * (2 * blk), 2 * blk), 2 * blk)
                kbd_ref[dst, :] = _pair_blockdiag(k_ref[rows, p * pw:(p + 1) * pw], lane_kv)
                vbd_ref[dst, :] = _pair_blockdiag(v_ref[rows, p * pw:(p + 1) * pw], lane_kv)
            return 0

        lax.fori_loop(0, n_blocks, build, 0)

    row = lax.broadcasted_iota(jnp.int32, (blk, 2 * blk), 0)
    key_off = lax.broadcasted_iota(jnp.int32, (blk, 2 * blk), 1) % blk
    strictly_before = key_off < row
    m = m_ref[...]
    qs = [[q_ref[u * blk:(u + 1) * blk, p * pw:(p + 1) * pw] * scale for p in range(n_pairs)]
          for u in range(n_sub)]

    def step(first_block, carry, depth, diagonal_first, never_negative):
        chains = [(u, p, d) for u in range(n_sub) for p in range(n_pairs) for d in range(depth)]
        z2, vbd, log2_w, tot = {}, {}, {}, {}
        for u, p, d in chains:
            j = jnp.maximum(first_block + u - d, 0)
            base = pl.multiple_of((j * n_pairs + p) * (2 * blk), 2 * blk)
            vbd[u, p, d] = vbd_ref[pl.ds(base, 2 * blk), :]
            z2[u, p, d] = lax.dot_general(qs[u][p], kbd_ref[pl.ds(base, 2 * blk), :], (((1,), (1,)), ((), ())),
                                          preferred_element_type=F32) * LOG2_E
        for c in chains:
            softplus2 = jnp.maximum(z2[c], jnp.log2(1.0 + jnp.exp2(jnp.minimum(z2[c], EXP2_CLAMP))))
            if diagonal_first and c[2] == 0:
                softplus2 = jnp.where(strictly_before, softplus2, 0.0)
            hi, lo = _split2(softplus2)
            log2_w[c] = z2[c] + jnp.dot(jnp.concatenate([hi, lo], axis=1), m, preferred_element_type=F32)
            tot[c] = (jnp.sum(softplus2[:, :blk], axis=1, keepdims=True),
                      jnp.sum(softplus2[:, blk:], axis=1, keepdims=True))
        out = {}
        for u in range(n_sub):
            for p in range(n_pairs):
                acc, run0, run1 = carry[u][p]
                for d in range(depth):
                    c = (u, p, d)
                    use0, use1 = run0, run1
                    if not never_negative(u, d):
                        finished = first_block + u - d < 0
                        use0 = jnp.where(finished, -jnp.inf, run0)
                        use1 = jnp.where(finished, -jnp.inf, run1)
                    w = jnp.concatenate([jnp.exp2(log2_w[c][:, :blk] + use0), jnp.exp2(log2_w[c][:, blk:] + use1)],
                                        axis=1)
                    if diagonal_first and d == 0:
                        w = jnp.where(strictly_before, w, 0.0)
                    acc = acc + jnp.dot(w.astype(BF16), vbd[c], preferred_element_type=F32)
                    run0 = run0 - tot[c][0]
                    run1 = run1 - tot[c][1]
                out[u, p] = (acc, run0, run1)
        return tuple(tuple(out[u, p] for p in range(n_pairs)) for u in range(n_sub))

    zero_col = jnp.zeros((blk, 1), F32)
    carry = tuple(tuple((jnp.zeros((blk, pw), F32), zero_col, zero_col) for _ in range(n_pairs))
                  for _ in range(n_sub))
    carry = step(n_sub * i, carry, SB_FIRST_DEPTH, True, lambda u, d: u >= d)

    def any_weight_left(c):
        top = functools.reduce(jnp.maximum, [r for sub in c for _, run0, run1 in sub for r in (run0, run1)])
        return (jnp.max(top) > EXP2_F32_ZERO_BELOW).astype(jnp.int32)

    def sweep(state):
        t, _, c = state
        c = step(n_sub * i - SB_FIRST_DEPTH - t, c, 1, False, lambda u, d: u == n_sub - 1)
        return t + 1, any_weight_left(c), c

    n_trips = n_sub * i + n_sub - SB_FIRST_DEPTH
    _, _, carry = lax.while_loop(lambda state: (state[0] < n_trips) & (state[1] > 0), sweep,
                                 (jnp.int32(0), any_weight_left(carry), carry))
    for u in range(n_sub):
        for p in range(n_pairs):
            o_ref[u * blk:(u + 1) * blk, p * pw:(p + 1) * pw] = carry[u][p][0].astype(BF16)


def _stickbreak(proj_b, tq):
    b_, s_, _ = proj_b.shape
    gw = GROUP_WIDTH
    blk = SB_BLOCK
    idx = np.arange(2 * blk)
    same_head = (idx[:, None] // blk) == (idx[None, :] // blk)
    m = -(same_head & (idx[:, None] >= idx[None, :])).astype(np.float32)
    m = jnp.asarray(np.concatenate([m, m], axis=0), BF16)
    return pl.pallas_call(
        _stickbreak_kernel,
        grid=(b_, s_ // tq),
        in_specs=[pl.BlockSpec((None, tq, gw), lambda b, i: (b, i, S_Q)),
                  pl.BlockSpec((None, s_, gw), lambda b, i: (b, 0, S_K)),
                  pl.BlockSpec((None, s_, gw), lambda b, i: (b, 0, S_V)),
                  _const_spec((4 * blk, 2 * blk))],
        out_specs=pl.BlockSpec((None, tq, gw), lambda b, i: (b, i, 0)),
        out_shape=jax.ShapeDtypeStruct((b_, s_, gw), BF16),
        scratch_shapes=[pltpu.VMEM((2 * s_ * (N_GROUP_HEADS // 2), 2 * HEAD_DIM), BF16),
                        pltpu.VMEM((2 * s_ * (N_GROUP_HEADS // 2), 2 * HEAD_DIM), BF16)],
        compiler_params=_params("parallel", "arbitrary"),
        name="stickbreak",
    )(proj_b, proj_b, proj_b, m)


def _split3(x):
    p1 = x.astype(BF16)
    r1 = x - p1.astype(F32)
    p2 = r1.astype(BF16)
    p3 = (r1 - p2.astype(F32)).astype(BF16)
    return p1, p2, p3


def _dot3(a, parts):
    return sum(jnp.dot(a, p, preferred_element_type=F32) for p in parts)


def _hgrn_kernel(layer, q_ref, f_ref, v_ref, g_ref, lbl_ref, ng_ref, tri_ref, sel_ref, bias_ref, eb_ref, o_ref,
                 state_ref, b2_ref, key_ref):
    ts, gw = q_ref.shape
    c = HGRN_CHUNK
    n_chunks = ts // c
    pw = 2 * HEAD_DIM
    n_pairs = N_GROUP_HEADS // 2

    @pl.when(pl.program_id(1) == 0)
    def _():
        state_ref[...] = jnp.zeros(state_ref.shape, F32)

    logits = lbl_ref[...]
    e = jnp.exp(logits - jnp.max(logits, axis=0, keepdims=True))
    lb_p = e / jnp.sum(e, axis=0, keepdims=True)
    lb = jnp.sum(lb_p[1:layer + 1, :], axis=0, keepdims=True) if layer > 0 else jnp.zeros((1, gw), F32)

    f_pre = f_ref[...]
    q = q_ref[...]
    f_gate = lb + (1.0 - lb) * jax.nn.sigmoid(f_pre)
    log_f = jnp.log(jnp.maximum(f_gate, GATE_FLOOR))
    kk = (1.0 - lb) * jax.nn.sigmoid(-f_pre)
    parts = _split3(log_f)
    blocks = [slice(r, r + HGRN_CUMSUM_ROWS) for r in range(0, ts, HGRN_CUMSUM_ROWS)]
    b = jnp.concatenate([_dot3(tri_ref[...], [p[r, :] for p in parts]) for r in blocks], axis=0)
    b_tot = jnp.concatenate([_dot3(sel_ref[...], [p[r, :] for p in parts]) for r in blocks], axis=0)
    qe = (q * jnp.exp(b)).astype(BF16)
    kd = (kk * jnp.exp(b_tot - b)).astype(BF16)
    decay = jnp.exp(b_tot)
    b2_ref[...] = b * math.log2(math.e)
    key_ref[...] = (b - jnp.log(kk)) * math.log2(math.e)

    eb = eb_ref[...]
    vb = v_ref[...].astype(BF16)
    same_head = eb[0:pw, 0:pw] > 0
    rows = [slice(n * c, (n + 1) * c) for n in range(n_chunks)]
    lanes = [slice(p * pw, (p + 1) * pw) for p in range(n_pairs)]
    kv_t = [[lax.dot_general(vb[r, l], kd[r, l], (((0,), (0,)), ((), ())), preferred_element_type=F32)
             for r in rows] for l in lanes]
    states = []
    for p in range(n_pairs):
        state = state_ref[p]
        entering = []
        for n in range(n_chunks):
            entering.append(state.astype(BF16))
            state = state * decay[n * c:n * c + 1, lanes[p]] + jnp.where(same_head, kv_t[p][n], 0.0)
        state_ref[p] = state
        states.append(entering)
    o_state = [jnp.concatenate([lax.dot_general(qe[rows[n], lanes[p]], states[p][n], (((1,), (1,)), ((), ())),
                                                preferred_element_type=F32) for n in range(n_chunks)], axis=0)
               for p in range(n_pairs)]

    half = c // 2
    o_intra = []
    for n in range(n_chunks):
        r0 = n * c
        blocks = []
        for s in range(c):
            lo = 0 if s < half else half
            key_s = key_ref[r0 + s:r0 + s + 1, :]
            exponent = (b2_ref[r0 + lo:r0 + c, :] - key_s) + bias_ref[s, lo:c, :]
            blocks.append(jnp.exp2(exponent) * q_ref[r0 + lo:r0 + c, :])
        g = jnp.dot(jnp.concatenate(blocks, axis=0).astype(BF16), eb, preferred_element_type=F32)
        top = jnp.zeros((half, gw), F32)
        bottom = jnp.zeros((half, gw), F32)
        for s in range(c):
            vs = v_ref[r0 + s:r0 + s + 1, :]
            if s < half:
                top += g[s * c:s * c + half, :] * vs
                bottom += g[s * c + half:(s + 1) * c, :] * vs
            else:
                start = half * c + (s - half) * half
                bottom += g[start:start + half, :] * vs
        o_intra.append(top)
        o_intra.append(bottom)
    o_intra = jnp.concatenate(o_intra, axis=0)

    o = jnp.concatenate(o_state, axis=1) + o_intra
    ms = _head_mean(o * o, eb)
    o = o * lax.rsqrt(ms + NORM_EPS)
    o_ref[...] = (o * ng_ref[...] * jax.nn.silu(g_ref[...])).astype(BF16)


def _hgrn(proj, lb_logits, norm_g, eb, layer, ts):
    b_, s_, _ = proj.shape
    gw = GROUP_WIDTH
    c = HGRN_CHUNK
    depth = lb_logits.shape[0]
    cs = min(HGRN_CUMSUM_ROWS, ts)
    row = np.arange(cs)
    same_chunk = (row[:, None] // c) == (row[None, :] // c)
    tri = jnp.asarray(same_chunk & (row[:, None] >= row[None, :]), BF16)
    sel = jnp.asarray(same_chunk, BF16)
    pos = np.arange(c)
    bias = np.where(pos[None, :, None] >= pos[:, None, None], 0.0, -1e30)
    bias = jnp.asarray(np.broadcast_to(bias, (c, c, gw)), F32)
    col = lambda j: pl.BlockSpec((None, ts, gw), lambda b, s: (b, s, j))
    return pl.pallas_call(
        functools.partial(_hgrn_kernel, layer),
        grid=(b_, s_ // ts),
        in_specs=[col(D_Q), col(D_F), col(D_V), col(D_G), _const_spec((depth, gw)), _const_spec((1, gw)),
                  _const_spec((cs, cs)), _const_spec((cs, cs)), _const_spec((c, c, gw)), _const_spec((gw, gw))],
        out_specs=pl.BlockSpec((None, ts, gw), lambda b, s: (b, s, 0)),
        out_shape=jax.ShapeDtypeStruct((b_, s_, gw), BF16),
        scratch_shapes=[pltpu.VMEM((N_GROUP_HEADS // 2, 2 * HEAD_DIM, 2 * HEAD_DIM), F32),
                        pltpu.VMEM((ts, gw), F32), pltpu.VMEM((ts, gw), F32)],
        compiler_params=_params("parallel", "arbitrary"),
        name="hgrn2",
    )(proj, proj, proj, proj, lb_logits, norm_g.reshape(1, gw), tri, sel, bias, eb)


def kernel(x, ln_in_g, ln_in_b, w_in, conv_w, conv_b, rg_wa, rg_ba, rg_wx, rg_bx, rg_lambda, ret_norm_g,
           hgrn_lb_logits, hgrn_norm_g, w_out, ln1_g, ln1_b, w_up, w_down, ln2_g, ln2_b):
    b_, s_, d = x.shape
    depth = w_in.shape[0]
    t = b_ * s_
    alpha = (2 * depth) ** 0.25
    proj_rows = min(PROJ_ROWS, t)
    tail_rows = min(TAIL_ROWS, t)

    head = np.arange(GROUP_WIDTH) // HEAD_DIM
    eb = jnp.asarray(head[:, None] == head[None, :], BF16)

    h = hb = None
    gw = GROUP_WIDTH
    columns = lambda w, slices: jnp.concatenate([w[:, s * gw:(s + 1) * gw] for s in slices], axis=1).astype(BF16)
    for l in range(depth):
        wf, wb = columns(w_in[l], F32_SLICES), columns(w_in[l], BF16_SLICES)
        if l == 0:
            h, pf, pb = _proj(x.reshape(t, d), wf, wb, proj_rows, ln=(ln_in_g, ln_in_b))
        else:
            pf, pb = _proj(hb, wf, wb, proj_rows)
        pf = pf.reshape(b_, s_, -1)
        pb = pb.reshape(b_, s_, -1)
        y_a = _rglru(pf, conv_w[l], conv_b[l], rg_wa[l], rg_ba[l], rg_wx[l], rg_bx[l], rg_lambda[l],
                     min(RGLRU_ROWS, s_))
        y_b = _retention(pf, pb, ret_norm_g[l], eb, min(RETENTION_ROWS, s_))
        y_c = _stickbreak(pb, min(STICKBREAK_ROWS, s_))
        y_d = _hgrn(pf, hgrn_lb_logits, hgrn_norm_g[l], eb, l, min(HGRN_ROWS, s_))
        ys = [y.reshape(t, gw) for y in (y_a, y_b, y_c, y_d)]
        h, hb = _block_tail(ys, w_out[l].astype(BF16), h, ln1_g[l], ln1_b[l], w_up[l].astype(BF16),
                            w_down[l].astype(BF16), ln2_g[l], ln2_b[l], alpha, tail_rows, FF_CHUNK)
    return h.reshape(b_, s_, d).astype(x.dtype)
```
